```python
import math
import jax, jax.numpy as jnp
from jax import lax
import numpy as np

D_MODEL = 1024
BATCH = 4
SEQ = 8192
DEPTH = 2

HEAD_DIM = 64
SWA_HEADS = 6
SWA_KV_HEADS = 2
SWA_GROUP = SWA_HEADS // SWA_KV_HEADS
WINDOW = 128
POOL_WINDOWS = (2, 4, 8, 16)
POOL_GROUP_DIM = 64
POOL_DIM = POOL_GROUP_DIM * len(POOL_WINDOWS)
MLA_HEADS = 6
Q_LORA = 256
KV_LORA = 128
QK_NOPE = 64
QK_ROPE = 32
V_HEAD = 64
QK_HEAD = QK_NOPE + QK_ROPE
ROPE_THETA = 10000.0
Q_BLOCK = 128
N_BUCKETS = 32
MAX_DISTANCE = 128
SWA_Q_DIM = SWA_HEADS * HEAD_DIM
SWA_KV_DIM = SWA_KV_HEADS * HEAD_DIM
MLA_OUT_DIM = MLA_HEADS * V_HEAD
IN_SPLITS = (SWA_Q_DIM, SWA_KV_DIM, SWA_KV_DIM, POOL_DIM, Q_LORA, KV_LORA, QK_ROPE)
IN_DIM = sum(IN_SPLITS)
MIX_DIM = SWA_Q_DIM + POOL_DIM + MLA_OUT_DIM
D_FF = -(-8 * D_MODEL // (3 * 256)) * 256
EPS = 1e-6

kernel_name = "hybrid_swa_pool_mla_block"


def rms_norm(x, g):
    xf = x.astype(jnp.float32)
    y = xf * lax.rsqrt(jnp.mean(xf * xf, axis=-1, keepdims=True) + EPS)
    return (y * g.astype(jnp.float32)).astype(x.dtype)


def t5_causal_bucket(dist):
    n = jnp.maximum(dist, 0)
    max_exact = N_BUCKETS // 2
    nf = jnp.maximum(n, 1).astype(jnp.float32)
    large = max_exact + (jnp.log(nf / max_exact) / math.log(MAX_DISTANCE / max_exact)
                         * (N_BUCKETS - max_exact)).astype(jnp.int32)
    large = jnp.minimum(large, N_BUCKETS - 1)
    return jnp.where(n < max_exact, n, large)


def apply_rope(x, cos, sin):
    half = x.shape[-1] // 2
    x1, x2 = x[..., :half], x[..., half:]
    return jnp.concatenate([x1 * cos - x2 * sin, x1 * sin + x2 * cos], axis=-1).astype(x.dtype)


def swa_mixer(q, k, v, q_gain, k_gain, sinks, rel_bias):
    B, S = q.shape[:2]
    nb = S // WINDOW
    q = rms_norm(q.reshape(B, S, SWA_HEADS, HEAD_DIM), q_gain)
    k = rms_norm(k.reshape(B, S, SWA_KV_HEADS, HEAD_DIM), k_gain)
    v = v.reshape(B, S, SWA_KV_HEADS, HEAD_DIM)
    qb = q.reshape(B, nb, WINDOW, SWA_KV_HEADS, SWA_GROUP, HEAD_DIM)

    def band(t):
        tb = t.reshape(B, nb, WINDOW, SWA_KV_HEADS, HEAD_DIM)
        prev = jnp.pad(tb, ((0, 0), (1, 0), (0, 0), (0, 0), (0, 0)))[:, :-1]
        return jnp.concatenate([prev, tb], axis=2)

    kb, vb = band(k), band(v)
    s = jnp.einsum('bnqhgd,bnkhd->bnhgqk', qb, kb).astype(jnp.float32) * (HEAD_DIM ** -0.5)
    q_loc = jnp.arange(WINDOW)
    k_loc = jnp.arange(2 * WINDOW)
    dist = q_loc[:, None] + WINDOW - k_loc[None, :]
    band_ok = (dist >= 0) & (dist < WINDOW)
    bias = rel_bias[t5_causal_bucket(dist)].astype(jnp.float32)
    bias = bias.transpose(2, 0, 1).reshape(SWA_KV_HEADS, SWA_GROUP, WINDOW, 2 * WINDOW)
    key_abs = jnp.arange(nb)[:, None] * WINDOW - WINDOW + k_loc[None, :]
    mask = band_ok[None] & (key_abs >= 0)[:, None, :]
    s = jnp.where(mask[None, :, None, None], s + bias[None, None], -jnp.inf)
    sk = sinks.astype(jnp.float32).reshape(1, 1, SWA_KV_HEADS, SWA_GROUP, 1, 1)
    lse = jnp.logaddexp(jax.nn.logsumexp(s, axis=-1, keepdims=True), sk)
    p = jnp.exp(s - lse)
    o = jnp.einsum('bnhgqk,bnkhd->bnqhgd', p.astype(v.dtype), vb)
    return o.reshape(B, S, SWA_Q_DIM)


def pool_mixer(u, pool_w, pool_scale):
    S = u.shape[1]
    uf = u.astype(jnp.float32)
    cs = jnp.pad(lax.cumsum(uf, axis=1), ((0, 0), (1, 0), (0, 0)))
    t = jnp.arange(S)
    outs = []
    for g, w in enumerate(POOL_WINDOWS):
        sl = slice(g * POOL_GROUP_DIM, (g + 1) * POOL_GROUP_DIM)
        c = cs[..., sl]
        lo = jnp.pad(c[:, :S + 1 - w], ((0, 0), (w - 1, 0), (0, 0)))
        count = jnp.minimum(t + 1, w).astype(jnp.float32)[None, :, None]
        d = (c[:, 1:] - lo) / count - uf[..., sl]
        outs.append(jnp.einsum('bsc,cd->bsd', d, pool_w[g].astype(jnp.float32)))
    y = jnp.concatenate(outs, axis=-1) * pool_scale.astype(jnp.float32)
    return y.astype(u.dtype)


def mla_mixer(c_q, c_kv, k_rope, positions, q_a_gain, w_qb, kv_a_gain, w_kvb,
              q_nope_gain, q_rope_gain, k_nope_gain, k_rope_gain):
    B, S = c_q.shape[:2]
    q = (rms_norm(c_q, q_a_gain) @ w_qb).reshape(B, S, MLA_HEADS, QK_HEAD)
    kv = (rms_norm(c_kv, kv_a_gain) @ w_kvb).reshape(B, S, MLA_HEADS, QK_NOPE + V_HEAD)
    q_nope = rms_norm(q[..., :QK_NOPE], q_nope_gain)
    q_rope = rms_norm(q[..., QK_NOPE:], q_rope_gain)
    k_nope = rms_norm(kv[..., :QK_NOPE], k_nope_gain)
    v = kv[..., QK_NOPE:]
    k_rope = rms_norm(k_rope, k_rope_gain)
    inv_freq = ROPE_THETA ** (-jnp.arange(0, QK_ROPE, 2, dtype=jnp.float32) / QK_ROPE)
    ang = positions.astype(jnp.float32)[..., None] * inv_freq
    cos, sin = jnp.cos(ang), jnp.sin(ang)
    q_rope = apply_rope(q_rope, cos[:, :, None, :], sin[:, :, None, :])
    k_rope = apply_rope(k_rope, cos, sin)
    nb = S // Q_BLOCK
    qn = q_nope.reshape(B, nb, Q_BLOCK, MLA_HEADS, QK_NOPE).transpose(1, 0, 2, 3, 4)
    qr = q_rope.reshape(B, nb, Q_BLOCK, MLA_HEADS, QK_ROPE).transpose(1, 0, 2, 3, 4)
    q_idx = jnp.arange(S).reshape(nb, Q_BLOCK)
    k_idx = jnp.arange(S)
    scale = QK_HEAD ** -0.5

    def block(args):
        qn_b, qr_b, qi = args
        s = (jnp.einsum('bqhd,bkhd->bhqk', qn_b, k_nope)
             + jnp.einsum('bqhd,bkd->bhqk', qr_b, k_rope)).astype(jnp.float32) * scale
        s = jnp.where(k_idx[None, :] <= qi[:, None], s, -jnp.inf)
        p = jax.nn.softmax(s, axis=-1)
        return jnp.einsum('bhqk,bkhd->bqhd', p.astype(v.dtype), v)

    o = lax.map(block, (qn, qr, q_idx))
    return o.transpose(1, 0, 2, 3, 4).reshape(B, S, MLA_OUT_DIM)


def setup_inputs(seed: int = 0) -> dict:
    key = jax.random.key(seed)
    ks = iter(jax.random.split(key, 32))
    f32 = jnp.float32

    def dense(shape, fan_in):
        return jax.random.normal(next(ks), shape, f32) * fan_in ** -0.5

    def gain(shape):
        return 1.0 + 0.02 * jax.random.normal(next(ks), shape, f32)

    x = jax.random.normal(next(ks), (BATCH, SEQ, D_MODEL), f32)
    offsets = jax.random.randint(next(ks), (BATCH, 1), 0, 4096, dtype=jnp.int32)
    positions = offsets + jnp.arange(SEQ, dtype=jnp.int32)[None, :]
    return {
        "x": x,
        "positions": positions,
        "rel_bias": 0.5 * jax.random.normal(next(ks), (N_BUCKETS, SWA_HEADS), f32),
        "attn_norm": gain((DEPTH, D_MODEL)),
        "w_in": dense((DEPTH, D_MODEL, IN_DIM), D_MODEL),
        "swa_q_gain": gain((DEPTH, HEAD_DIM)),
        "swa_k_gain": gain((DEPTH, HEAD_DIM)),
        "swa_sinks": 0.5 * jax.random.normal(next(ks), (DEPTH, SWA_HEADS), f32),
        "pool_w": dense((DEPTH, len(POOL_WINDOWS), POOL_GROUP_DIM, POOL_GROUP_DIM), POOL_GROUP_DIM),
        "pool_scale": gain((DEPTH, POOL_DIM)),
        "mla_q_a_gain": gain((DEPTH, Q_LORA)),
        "mla_w_qb": dense((DEPTH, Q_LORA, MLA_HEADS * QK_HEAD), Q_LORA),
        "mla_kv_a_gain": gain((DEPTH, KV_LORA)),
        "mla_w_kvb": dense((DEPTH, KV_LORA, MLA_HEADS * (QK_NOPE + V_HEAD)), KV_LORA),
        "mla_q_nope_gain": gain((DEPTH, QK_NOPE)),
        "mla_q_rope_gain": gain((DEPTH, QK_ROPE)),
        "mla_k_nope_gain": gain((DEPTH, QK_NOPE)),
        "mla_k_rope_gain": gain((DEPTH, QK_ROPE)),
        "w_out": dense((DEPTH, MIX_DIM, D_MODEL), MIX_DIM),
        "ffn_norm": gain((DEPTH, D_MODEL)),
        "w_gate": dense((DEPTH, D_MODEL, D_FF), D_MODEL),
        "w_up": dense((DEPTH, D_MODEL, D_FF), D_MODEL),
        "w_down": dense((DEPTH, D_FF, D_MODEL), D_FF),
    }


def reference(x, positions, rel_bias, attn_norm, w_in, swa_q_gain, swa_k_gain, swa_sinks,
              pool_w, pool_scale, mla_q_a_gain, mla_w_qb, mla_kv_a_gain, mla_w_kvb,
              mla_q_nope_gain, mla_q_rope_gain, mla_k_nope_gain, mla_k_rope_gain,
              w_out, ffn_norm, w_gate, w_up, w_down):
    split_idx = list(np.cumsum(IN_SPLITS)[:-1])
    for l in range(DEPTH):
        h = rms_norm(x, attn_norm[l])
        proj = h @ w_in[l]
        q_a, k_a, v_a, u_b, cq_c, ckv_c, kr_c = jnp.split(proj, split_idx, axis=-1)
        out_a = swa_mixer(q_a, k_a, v_a, swa_q_gain[l], swa_k_gain[l], swa_sinks[l], rel_bias)
        out_b = pool_mixer(u_b, pool_w[l], pool_scale[l])
        out_c = mla_mixer(cq_c, ckv_c, kr_c, positions, mla_q_a_gain[l], mla_w_qb[l],
                          mla_kv_a_gain[l], mla_w_kvb[l], mla_q_nope_gain[l],
                          mla_q_rope_gain[l], mla_k_nope_gain[l], mla_k_rope_gain[l])
        mixed = jnp.concatenate([out_a, out_b, out_c], axis=-1)
        x = x + mixed @ w_out[l]
        h = rms_norm(x, ffn_norm[l])
        x = x + (jax.nn.silu(h @ w_gate[l]) * (h @ w_up[l])) @ w_down[l]
    return x
```

```python
import functools
import math

import numpy as np
import jax
import jax.numpy as jnp
from jax import lax
from jax.experimental import pallas as pl
from jax.experimental.pallas import tpu as pltpu

D_MODEL = 1024
HEAD_DIM = 64
SWA_HEADS = 6
SWA_KV_HEADS = 2
SWA_GROUP = SWA_HEADS // SWA_KV_HEADS
WINDOW = 128
POOL_WINDOWS = (2, 4, 8, 16)
POOL_GROUP_DIM = 64
POOL_DIM = POOL_GROUP_DIM * len(POOL_WINDOWS)
MLA_HEADS = 6
Q_LORA = 256
KV_LORA = 128
QK_NOPE = 64
QK_ROPE = 32
V_HEAD = 64
QK_HEAD = QK_NOPE + QK_ROPE
ROPE_THETA = 10000.0
N_BUCKETS = 32
MAX_DISTANCE = 128
SWA_Q_DIM = SWA_HEADS * HEAD_DIM
SWA_KV_DIM = SWA_KV_HEADS * HEAD_DIM
MLA_OUT_DIM = MLA_HEADS * V_HEAD
D_FF = 2816
EPS = 1e-6

LANE = 128
MLA_HEAD_PAD = LANE
MLA_PAD_DIM = MLA_HEADS * MLA_HEAD_PAD
ROPE_HALF = QK_ROPE // 2
POOL_HALO = max(POOL_WINDOWS)
NEG = -1e30

C_QA = 0
C_KA = C_QA + SWA_Q_DIM
C_VA = C_KA + SWA_KV_DIM
C_UB = C_VA + SWA_KV_DIM
C_CQ = C_UB + POOL_DIM
C_CKV = C_CQ + Q_LORA
C_KR = C_CKV + KV_LORA
IN_PAD_DIM = C_KR + LANE

IN_TILE = 512
SWA_TILE = 512
MLA_TQ = 512
MLA_TK = 512
FFN_TILE = 512
FFN_CHUNKS = ((0, 1536), (1536, 1280))
VMEM_LIMIT = 56 * 1024 * 1024


def _dot(a, b):
    return jnp.dot(a, b, preferred_element_type=jnp.float32)


def _dot_nt(a, b):
    return lax.dot_general(a, b, (((1,), (1,)), ((), ())), preferred_element_type=jnp.float32)


def _rms(x, width):
    return x * lax.rsqrt(jnp.sum(x * x, axis=-1, keepdims=True) * (1.0 / width) + EPS)


def _bias_kernel(rel_ref, bucket_ref, out_ref):
    bucket = bucket_ref[...]
    for h in range(SWA_HEADS):
        acc = jnp.full(bucket.shape, NEG, jnp.float32)
        for b in range(N_BUCKETS):
            acc = jnp.where(bucket == b, rel_ref[b, h], acc)
        out_ref[h] = acc


def _band_buckets():
    q_loc = np.arange(WINDOW)[:, None]
    k_loc = np.arange(2 * WINDOW)[None, :]
    dist = q_loc + WINDOW - k_loc
    band_ok = (dist >= 0) & (dist < WINDOW)
    n = np.maximum(dist, 0)
    max_exact = N_BUCKETS // 2
    nf = np.maximum(n, 1).astype(np.float32)
    large = max_exact + (np.log(nf / max_exact) / math.log(MAX_DISTANCE / max_exact)
                         * (N_BUCKETS - max_exact)).astype(np.int32)
    large = np.minimum(large, N_BUCKETS - 1)
    bucket = np.where(n < max_exact, n, large)
    return np.where(band_ok, bucket, -1).astype(np.int32)


def _bias_table(rel_bias):
    return pl.pallas_call(
        _bias_kernel,
        out_shape=jax.ShapeDtypeStruct((SWA_HEADS, WINDOW, 2 * WINDOW), jnp.float32),
        in_specs=[pl.BlockSpec(memory_space=pltpu.SMEM),
                  pl.BlockSpec(memory_space=pltpu.VMEM)],
        out_specs=pl.BlockSpec(memory_space=pltpu.VMEM),
        name="bias_table",
    )(rel_bias, jnp.asarray(_band_buckets()))


def _in_proj_kernel(x_ref, pos_ref, g_attn_ref, w1_ref, g384_ref, g768_ref, v768_ref, v384_ref,
                    v256_ref, v128_ref, wp_ref, wqb_ref, wkvb_ref,
                    qs_ref, ks_ref, vs_ref, pool_ref, qm_ref, km_ref, vm_ref, carry_ref):
    i = pl.program_id(1)
    tm = x_ref.shape[0]
    bf16 = jnp.bfloat16

    x = x_ref[...]
    h = (_rms(x, D_MODEL) * g_attn_ref[...]).astype(bf16)
    proj = _dot(h, w1_ref[...])

    qa = proj[:, C_QA:C_KA]
    ss = _dot((qa * qa).astype(bf16), g384_ref[...])
    qs_ref[...] = (qa * lax.rsqrt(ss * (1.0 / HEAD_DIM) + EPS) * v384_ref[...]).astype(bf16)
    ka = proj[:, C_KA:C_VA]
    ss = _dot((ka * ka).astype(bf16), g384_ref[0:SWA_KV_DIM, 0:SWA_KV_DIM])
    ks_ref[...] = (ka * lax.rsqrt(ss * (1.0 / HEAD_DIM) + EPS) * v128_ref[0:1, :]).astype(bf16)
    vs_ref[...] = proj[:, C_VA:C_UB].astype(bf16)

    u = proj[:, C_UB:C_CQ]

    @pl.when(i == 0)
    def _():
        carry_ref[...] = jnp.zeros_like(carry_ref)

    uext = jnp.concatenate([carry_ref[...], u], axis=0)
    a1 = uext + pltpu.roll(uext, 1, 0)
    a2 = a1 + pltpu.roll(a1, 2, 0)
    a3 = a2 + pltpu.roll(a2, 4, 0)
    a4 = a3 + pltpu.roll(a3, 8, 0)
    lane = lax.broadcasted_iota(jnp.int32, (1, POOL_DIM), 1)
    wsum = jnp.where(lane < POOL_GROUP_DIM, a1,
                     jnp.where(lane < 2 * POOL_GROUP_DIM, a2,
                               jnp.where(lane < 3 * POOL_GROUP_DIM, a3, a4)))[POOL_HALO:]
    t = (i * tm + lax.broadcasted_iota(jnp.int32, (tm, 1), 0) + 1).astype(jnp.float32)
    count = jnp.minimum(t, v256_ref[2:3, :])
    d = wsum / count - u
    pool_ref[...] = (_dot(d.astype(bf16), wp_ref[...]) * v256_ref[1:2, :]).astype(bf16)
    carry_ref[...] = u[tm - POOL_HALO:, :]

    ang = pos_ref[...] * v128_ref[3:4, :]
    cos1, sin1 = jnp.cos(ang), jnp.sin(ang)
    m1 = v768_ref[3:4, :]
    m2 = v768_ref[4:5, :]

    def rope(xn, cos_t, sin_t, m1_t, m2_t):
        width = xn.shape[1]
        rot = pltpu.roll(xn, ROPE_HALF, 1) * m1_t + pltpu.roll(xn, width - ROPE_HALF, 1) * m2_t
        return xn * cos_t + rot * sin_t

    cq = proj[:, C_CQ:C_CKV]
    cqn = (_rms(cq, Q_LORA) * v256_ref[0:1, :]).astype(bf16)
    q = _dot(cqn, wqb_ref[...])
    ss = _dot((q * q).astype(bf16), g768_ref[...])
    qn = q * lax.rsqrt(ss * v768_ref[1:2, :] + EPS) * v768_ref[0:1, :]
    cos6 = jnp.concatenate([cos1] * MLA_HEADS, axis=1)
    sin6 = jnp.concatenate([sin1] * MLA_HEADS, axis=1)
    qm_ref[...] = rope(qn, cos6, sin6, m1, m2).astype(bf16)

    kr = proj[:, C_KR:IN_PAD_DIM]
    krn = _rms(kr, QK_ROPE) * v128_ref[2:3, :]
    krr = rope(krn, cos1, sin1, m1[:, 0:LANE], m2[:, 0:LANE])
    ckv = proj[:, C_CKV:C_KR]
    ckvn = (_rms(ckv, KV_LORA) * v128_ref[1:2, :]).astype(bf16)
    kv = _dot(ckvn, wkvb_ref[...])
    kn = kv[:, 0:MLA_PAD_DIM]
    ss = _dot((kn * kn).astype(bf16), g768_ref[...])
    kn = kn * lax.rsqrt(ss * (1.0 / QK_NOPE) + EPS) * v768_ref[2:3, :]
    km_ref[...] = (kn + jnp.concatenate([krr] * MLA_HEADS, axis=1)).astype(bf16)
    vm_ref[...] = kv[:, MLA_PAD_DIM:].astype(bf16)


def _const_spec(shape):
    nd = len(shape)
    return pl.BlockSpec(shape, lambda *_: (0,) * nd)


def _in_proj(x2, pos2, consts, batch, seq):
    n = x2.shape[0]
    tm = min(IN_TILE, seq)
    nt = seq // tm
    row = lambda b, i: (b * nt + i, 0)
    bf16 = jnp.bfloat16
    outs = [(SWA_Q_DIM, bf16), (SWA_KV_DIM, bf16), (SWA_KV_DIM, bf16), (POOL_DIM, bf16),
            (MLA_PAD_DIM, bf16), (MLA_PAD_DIM, bf16), (MLA_PAD_DIM, bf16)]
    return pl.pallas_call(
        _in_proj_kernel,
        grid=(batch, nt),
        in_specs=[pl.BlockSpec((tm, D_MODEL), row), pl.BlockSpec((tm, 1), row)]
                 + [_const_spec(c.shape) for c in consts],
        out_specs=[pl.BlockSpec((tm, w), row) for w, _ in outs],
        out_shape=[jax.ShapeDtypeStruct((n, w), dt) for w, dt in outs],
        scratch_shapes=[pltpu.VMEM((POOL_HALO, POOL_DIM), jnp.float32)],
        compiler_params=pltpu.CompilerParams(
            dimension_semantics=("arbitrary", "arbitrary"), vmem_limit_bytes=VMEM_LIMIT),
        name="in_proj",
    )(x2, pos2, *consts)


def _swa_kernel(sink_ref, q_ref, kc_ref, kp_ref, vc_ref, vp_ref, bias_ref, o_ref):
    i = pl.program_id(1)
    ts = q_ref.shape[0]
    bf16 = jnp.bfloat16
    q = q_ref[...]
    kext = jnp.concatenate([kp_ref[...], kc_ref[...]], axis=0)
    vext = jnp.concatenate([vp_ref[...], vc_ref[...]], axis=0)
    col = lax.broadcasted_iota(jnp.int32, (1, 2 * WINDOW), 1)
    no_prev = jnp.logical_and(i == 0, col < WINDOW)
    for c in range(ts // WINDOW):
        kb = kext[c * WINDOW:(c + 2) * WINDOW]
        vb = vext[c * WINDOW:(c + 2) * WINDOW]
        outs = []
        for h in range(SWA_HEADS):
            g = h // SWA_GROUP
            qh = q[c * WINDOW:(c + 1) * WINDOW, h * HEAD_DIM:(h + 1) * HEAD_DIM]
            s = _dot_nt(qh, kb[:, g * HEAD_DIM:(g + 1) * HEAD_DIM]) + bias_ref[h]
            if c == 0:
                s = jnp.where(no_prev, NEG, s)
            sink = sink_ref[h]
            m = jnp.maximum(jnp.max(s, axis=-1, keepdims=True), sink)
            p = jnp.exp(s - m)
            denom = jnp.sum(p, axis=-1, keepdims=True) + jnp.exp(sink - m)
            o = _dot(p.astype(bf16), vb[:, g * HEAD_DIM:(g + 1) * HEAD_DIM])
            outs.append(o / denom)
        o_ref[c * WINDOW:(c + 1) * WINDOW, :] = jnp.concatenate(outs, axis=1).astype(bf16)


def _swa(sinks, qs, ks, vs, bias, batch, seq):
    n = qs.shape[0]
    ts = min(SWA_TILE, seq)
    nt = seq // ts
    wpt = ts // WINDOW
    wps = seq // WINDOW
    row = lambda b, i: (b * nt + i, 0)
    prev = lambda b, i: (b * wps + jnp.maximum(i * wpt - 1, 0), 0)
    return pl.pallas_call(
        _swa_kernel,
        grid=(batch, nt),
        in_specs=[pl.BlockSpec(memory_space=pltpu.SMEM),
                  pl.BlockSpec((ts, SWA_Q_DIM), row),
                  pl.BlockSpec((ts, SWA_KV_DIM), row), pl.BlockSpec((WINDOW, SWA_KV_DIM), prev),
                  pl.BlockSpec((ts, SWA_KV_DIM), row), pl.BlockSpec((WINDOW, SWA_KV_DIM), prev),
                  _const_spec(bias.shape)],
        out_specs=pl.BlockSpec((ts, SWA_Q_DIM), row),
        out_shape=jax.ShapeDtypeStruct((n, SWA_Q_DIM), jnp.bfloat16),
        compiler_params=pltpu.CompilerParams(
            dimension_semantics=("arbitrary", "arbitrary"), vmem_limit_bytes=VMEM_LIMIT),
        name="swa",
    )(sinks, qs, ks, ks, vs, vs, bias)


def _mla_kernel(q_ref, k_ref, v_ref, o_ref, m_ref, l_ref, acc_ref):
    i = pl.program_id(2)
    tq = q_ref.shape[0]
    tk = tq
    bf16 = jnp.bfloat16
    m_ref[...] = jnp.full(m_ref.shape, NEG, jnp.float32)
    l_ref[...] = jnp.zeros(l_ref.shape, jnp.float32)
    acc_ref[...] = jnp.zeros(acc_ref.shape, jnp.float32)

    def step(j, masked):
        start = pl.multiple_of(j * tk, tk)
        for a in range(2):
            lanes = slice(a * LANE, (a + 1) * LANE)
            s = _dot_nt(q_ref[:, lanes], k_ref[pl.ds(start, tk), lanes])
            if masked:
                r = lax.broadcasted_iota(jnp.int32, (tq, tk), 0)
                c = lax.broadcasted_iota(jnp.int32, (tq, tk), 1)
                s = jnp.where(r >= c, s, NEG)
            m_prev = m_ref[a]
            m_new = jnp.maximum(m_prev, jnp.max(s, axis=-1, keepdims=True))
            alpha = jnp.exp(m_prev - m_new)
            p = jnp.exp(s - jnp.tile(m_new, (1, tk // LANE)))
            l_ref[a] = alpha * l_ref[a] + jnp.sum(p, axis=-1, keepdims=True)
            acc_ref[a] = alpha * acc_ref[a] + _dot(p.astype(bf16), v_ref[pl.ds(start, tk), lanes])
            m_ref[a] = m_new

    def body(j, carry):
        step(j, False)
        return carry

    lax.fori_loop(0, i, body, 0)
    step(i, True)
    o_ref[...] = (acc_ref[0] / l_ref[0] + acc_ref[1] / l_ref[1]).astype(bf16)


def _mla(qm, km, vm, batch, seq):
    n = qm.shape[0]
    tq = min(MLA_TQ, seq)
    nq = seq // tq
    pairs = MLA_HEADS // 2
    return pl.pallas_call(
        _mla_kernel,
        grid=(batch, pairs, nq),
        in_specs=[pl.BlockSpec((tq, 2 * LANE), lambda b, p, i: (b * nq + i, p)),
                  pl.BlockSpec((seq, 2 * LANE), lambda b, p, i: (b, p)),
                  pl.BlockSpec((seq, 2 * LANE), lambda b, p, i: (b, p))],
        out_specs=pl.BlockSpec((tq, LANE), lambda b, p, i: (b * nq + i, p)),
        out_shape=jax.ShapeDtypeStruct((n, MLA_OUT_DIM), jnp.bfloat16),
        scratch_shapes=[pltpu.VMEM((2, tq, LANE), jnp.float32),
                        pltpu.VMEM((2, tq, LANE), jnp.float32),
                        pltpu.VMEM((2, tq, LANE), jnp.float32)],
        compiler_params=pltpu.CompilerParams(
            dimension_semantics=("arbitrary", "arbitrary", "arbitrary"),
            vmem_limit_bytes=VMEM_LIMIT),
        name="mla",
    )(qm, km, vm)


def _out_ffn_kernel(x_ref, a_ref, b_ref, c_ref, wo_ref, g_ref, wg_ref, wu_ref, wd_ref, o_ref):
    bf16 = jnp.bfloat16
    mixed = jnp.concatenate([a_ref[...], b_ref[...], c_ref[...]], axis=1)
    x1 = x_ref[...] + _dot(mixed, wo_ref[...])
    h = (_rms(x1, D_MODEL) * g_ref[...]).astype(bf16)
    acc = x1
    for start, size in FFN_CHUNKS:
        gate = _dot(h, wg_ref[:, start:start + size])
        up = _dot(h, wu_ref[:, start:start + size])
        act = (gate * jax.nn.sigmoid(gate) * up).astype(bf16)
        acc = acc + _dot(act, wd_ref[start:start + size, :])
    o_ref[...] = acc


def _resident(shape):
    nd = len(shape)
    return pl.BlockSpec(shape, lambda *_: (0,) * nd, pipeline_mode=pl.Buffered(1))


def _out_ffn(x2, oa, ob, oc, wo, g, wg, wu, wd):
    n = x2.shape[0]
    tm = min(FFN_TILE, n)
    row = lambda i: (i, 0)
    return pl.pallas_call(
        _out_ffn_kernel,
        grid=(n // tm,),
        in_specs=[pl.BlockSpec((tm, D_MODEL), row), pl.BlockSpec((tm, SWA_Q_DIM), row),
                  pl.BlockSpec((tm, POOL_DIM), row), pl.BlockSpec((tm, MLA_OUT_DIM), row),
                  _resident(wo.shape), _resident(g.shape), _resident(wg.shape),
                  _resident(wu.shape), _resident(wd.shape)],
        out_specs=pl.BlockSpec((tm, D_MODEL), row),
        out_shape=jax.ShapeDtypeStruct((n, D_MODEL), jnp.float32),
        compiler_params=pltpu.CompilerParams(
            dimension_semantics=("arbitrary",), vmem_limit_bytes=VMEM_LIMIT),
        name="out_ffn",
    )(x2, oa, ob, oc, wo, g, wg, wu, wd)


def _block_diag_ones(groups, width):
    g = np.zeros((width, width), np.float32)
    for lo, hi in groups:
        g[lo:hi, lo:hi] = 1.0
    return jnp.asarray(g, jnp.bfloat16)


def _lane_rows(rows, width):
    out = jnp.zeros((8, width), jnp.float32)
    for r, v in enumerate(rows):
        out = out.at[r].set(v.astype(jnp.float32))
    return out


def _head_lanes(nope, rope):
    pad = jnp.zeros((MLA_HEAD_PAD - QK_HEAD,), jnp.float32)
    return jnp.tile(jnp.concatenate([nope.astype(jnp.float32), rope.astype(jnp.float32), pad]),
                    MLA_HEADS)


def _layer_consts(l, attn_norm, w_in, swa_q_gain, swa_k_gain, pool_w, pool_scale, mla_q_a_gain,
                  mla_w_qb, mla_kv_a_gain, mla_w_kvb, mla_q_nope_gain, mla_q_rope_gain,
                  mla_k_nope_gain, mla_k_rope_gain):
    f32, bf16 = jnp.float32, jnp.bfloat16
    zeros = lambda r, c: jnp.zeros((r, c), f32)
    split = C_KR
    w1 = jnp.concatenate([w_in[l][:, :split], zeros(D_MODEL, QK_NOPE), w_in[l][:, split:],
                          zeros(D_MODEL, LANE - QK_HEAD)], axis=1).astype(bf16)
    wqb = mla_w_qb[l].reshape(Q_LORA, MLA_HEADS, QK_HEAD)
    wqb = jnp.pad(wqb, ((0, 0), (0, 0), (0, MLA_HEAD_PAD - QK_HEAD))).reshape(Q_LORA, MLA_PAD_DIM)
    wkvb = mla_w_kvb[l].reshape(KV_LORA, MLA_HEADS, QK_NOPE + V_HEAD)
    wk = jnp.pad(wkvb[:, :, :QK_NOPE], ((0, 0), (0, 0), (0, MLA_HEAD_PAD - QK_NOPE)))
    wv = wkvb[:, :, QK_NOPE:]
    zv = jnp.zeros_like(wv)
    odd = (jnp.arange(MLA_HEADS) % 2 == 1)[None, :, None]
    wv = jnp.concatenate([jnp.where(odd, zv, wv), jnp.where(odd, wv, zv)], axis=2)
    wkvb_p = jnp.concatenate([wk.reshape(KV_LORA, MLA_PAD_DIM), wv.reshape(KV_LORA, MLA_PAD_DIM)],
                             axis=1)
    wp = jnp.zeros((POOL_DIM, POOL_DIM), f32)
    for g in range(len(POOL_WINDOWS)):
        sl = slice(g * POOL_GROUP_DIM, (g + 1) * POOL_GROUP_DIM)
        wp = wp.at[sl, sl].set(pool_w[l, g])
    ones_n, ones_r = jnp.ones((QK_NOPE,), f32), jnp.ones((QK_ROPE,), f32)
    half = jnp.concatenate([jnp.zeros((ROPE_HALF,), f32), jnp.ones((ROPE_HALF,), f32)])
    v768 = _lane_rows([
        _head_lanes(mla_q_nope_gain[l], mla_q_rope_gain[l]) * (QK_HEAD ** -0.5),
        _head_lanes(ones_n / QK_NOPE, ones_r / QK_ROPE),
        _head_lanes(mla_k_nope_gain[l], 0.0 * ones_r),
        _head_lanes(0.0 * ones_n, half),
        _head_lanes(0.0 * ones_n, half - 1.0),
    ], MLA_PAD_DIM)
    v384 = (jnp.tile(swa_q_gain[l], SWA_HEADS) * (HEAD_DIM ** -0.5)).reshape(1, SWA_Q_DIM)
    windows = jnp.repeat(jnp.asarray(POOL_WINDOWS, f32), POOL_GROUP_DIM)
    v256 = _lane_rows([mla_q_a_gain[l], pool_scale[l], windows], POOL_DIM)
    inv_freq = ROPE_THETA ** (-jnp.arange(0, QK_ROPE, 2, dtype=f32) / QK_ROPE)
    one_head = lambda rope: _head_lanes(jnp.zeros((QK_NOPE,), f32), rope)[:LANE]
    v128 = _lane_rows([jnp.tile(swa_k_gain[l], SWA_KV_HEADS), mla_kv_a_gain[l],
                       one_head(mla_k_rope_gain[l]), one_head(jnp.tile(inv_freq, 2))], LANE)
    g384 = _block_diag_ones([(h * HEAD_DIM, (h + 1) * HEAD_DIM) for h in range(SWA_HEADS)],
                            SWA_Q_DIM)
    groups = []
    for h in range(MLA_HEADS):
        base = h * MLA_HEAD_PAD
        groups += [(base, base + QK_NOPE), (base + QK_NOPE, base + QK_HEAD)]
    g768 = _block_diag_ones(groups, MLA_PAD_DIM)
    return [attn_norm[l].reshape(1, D_MODEL), w1, g384, g768, v768, v384, v256, v128,
            wp.astype(bf16), wqb.astype(bf16), wkvb_p.astype(bf16)]


def kernel(x, positions, rel_bias, attn_norm, w_in, swa_q_gain, swa_k_gain, swa_sinks, pool_w,
           pool_scale, mla_q_a_gain, mla_w_qb, mla_kv_a_gain, mla_w_kvb, mla_q_nope_gain,
           mla_q_rope_gain, mla_k_nope_gain, mla_k_rope_gain, w_out, ffn_norm, w_gate, w_up,
           w_down):
    batch, seq, _ = x.shape
    n = batch * seq
    depth = w_in.shape[0]
    bf16 = jnp.bfloat16
    x2 = x.reshape(n, D_MODEL)
    pos2 = positions.astype(jnp.float32).reshape(n, 1)
    bias = _bias_table(rel_bias)
    for l in range(depth):
        consts = _layer_consts(l, attn_norm, w_in, swa_q_gain, swa_k_gain, pool_w, pool_scale,
                               mla_q_a_gain, mla_w_qb, mla_kv_a_gain, mla_w_kvb, mla_q_nope_gain,
                               mla_q_rope_gain, mla_k_nope_gain, mla_k_rope_gain)
        qs, ks, vs, pool_o, qm, km, vm = _in_proj(x2, pos2, consts, batch, seq)
        swa_o = _swa(swa_sinks[l], qs, ks, vs, bias, batch, seq)
        mla_o = _mla(qm, km, vm, batch, seq)
        x2 = _out_ffn(x2, swa_o, pool_o, mla_o, w_out[l].astype(bf16),
                      ffn_norm[l].reshape(1, D_MODEL), w_gate[l].astype(bf16),
                      w_up[l].astype(bf16), w_down[l].astype(bf16))
    return x2.reshape(batch, seq, D_MODEL)
```

```python
import functools
import math

import numpy as np
import jax
import jax.numpy as jnp
from jax import lax
from jax.experimental import pallas as pl
from jax.experimental.pallas import tpu as pltpu

D_MODEL = 1024
HEAD_DIM = 64
SWA_HEADS = 6
SWA_KV_HEADS = 2
SWA_GROUP = SWA_HEADS // SWA_KV_HEADS
WINDOW = 128
POOL_WINDOWS = (2, 4, 8, 16)
POOL_GROUP_DIM = 64
POOL_DIM = POOL_GROUP_DIM * len(POOL_WINDOWS)
MLA_HEADS = 6
Q_LORA = 256
KV_LORA = 128
QK_NOPE = 64
QK_ROPE = 32
V_HEAD = 64
QK_HEAD = QK_NOPE + QK_ROPE
ROPE_THETA = 10000.0
N_BUCKETS = 32
MAX_DISTANCE = 128
SWA_Q_DIM = SWA_HEADS * HEAD_DIM
SWA_KV_DIM = SWA_KV_HEADS * HEAD_DIM
MLA_OUT_DIM = MLA_HEADS * V_HEAD
D_FF = 2816
EPS = 1e-6

LANE = 128
MLA_HEAD_PAD = LANE
MLA_PAD_DIM = MLA_HEADS * MLA_HEAD_PAD
ROPE_HALF = QK_ROPE // 2
POOL_HALO = max(POOL_WINDOWS)
NEG = -1e30

C_QA = 0
C_KA = C_QA + SWA_Q_DIM
C_VA = C_KA + SWA_KV_DIM
C_UB = C_VA + SWA_KV_DIM
C_CQ = C_UB + POOL_DIM
C_CKV = C_CQ + Q_LORA
C_KR = C_CKV + KV_LORA
IN_PAD_DIM = C_KR + LANE

IN_TILE = 512
SWA_TILE = 512
MLA_TQ = 512
FFN_TILE = 512
FFN_CHUNKS = ((0, 1536), (1536, 1280))
VMEM_LIMIT = 56 * 1024 * 1024


def _dot(a, b):
    return jnp.dot(a, b, preferred_element_type=jnp.float32)


def _dot_nt(a, b):
    return lax.dot_general(a, b, (((1,), (1,)), ((), ())), preferred_element_type=jnp.float32)


def _rms(x, width):
    return x * lax.rsqrt(jnp.sum(x * x, axis=-1, keepdims=True) * (1.0 / width) + EPS)


def _bias_kernel(rel_ref, bucket_ref, out_ref):
    bucket = bucket_ref[...]
    for h in range(SWA_HEADS):
        acc = jnp.full(bucket.shape, NEG, jnp.float32)
        for b in range(N_BUCKETS):
            acc = jnp.where(bucket == b, rel_ref[b, h], acc)
        out_ref[h] = acc


def _band_buckets():
    q_loc = np.arange(WINDOW)[:, None]
    k_loc = np.arange(2 * WINDOW)[None, :]
    dist = q_loc + WINDOW - k_loc
    band_ok = (dist >= 0) & (dist < WINDOW)
    n = np.maximum(dist, 0)
    max_exact = N_BUCKETS // 2
    nf = np.maximum(n, 1).astype(np.float32)
    large = max_exact + (np.log(nf / max_exact) / math.log(MAX_DISTANCE / max_exact)
                         * (N_BUCKETS - max_exact)).astype(np.int32)
    large = np.minimum(large, N_BUCKETS - 1)
    bucket = np.where(n < max_exact, n, large)
    return np.where(band_ok, bucket, -1).astype(np.int32)


def _bias_table(rel_bias):
    return pl.pallas_call(
        _bias_kernel,
        out_shape=jax.ShapeDtypeStruct((SWA_HEADS, WINDOW, 2 * WINDOW), jnp.float32),
        in_specs=[pl.BlockSpec(memory_space=pltpu.SMEM),
                  pl.BlockSpec(memory_space=pltpu.VMEM)],
        out_specs=pl.BlockSpec(memory_space=pltpu.VMEM),
        name="bias_table",
    )(rel_bias, jnp.asarray(_band_buckets()))


def _in_proj_kernel(x_ref, pos_ref, g_attn_ref, w1_ref, g384_ref, g768_ref, v768_ref, v384_ref,
                    v256_ref, v128_ref, wp_ref, wqb_ref, wkvb_ref,
                    qs_ref, ks_ref, vs_ref, pool_ref, qm_ref, km_ref, vm_ref, carry_ref):
    i = pl.program_id(1)
    tm = x_ref.shape[0]
    bf16 = jnp.bfloat16

    x = x_ref[...]
    h = (_rms(x, D_MODEL) * g_attn_ref[...]).astype(bf16)
    proj = _dot(h, w1_ref[...])

    qa = proj[:, C_QA:C_KA]
    ss = _dot((qa * qa).astype(bf16), g384_ref[...])
    qs_ref[...] = (qa * lax.rsqrt(ss * (1.0 / HEAD_DIM) + EPS) * v384_ref[...]).astype(bf16)
    ka = proj[:, C_KA:C_VA]
    ss = _dot((ka * ka).astype(bf16), g384_ref[0:SWA_KV_DIM, 0:SWA_KV_DIM])
    ks_ref[...] = (ka * lax.rsqrt(ss * (1.0 / HEAD_DIM) + EPS) * v128_ref[0:1, :]).astype(bf16)
    vs_ref[...] = proj[:, C_VA:C_UB].astype(bf16)

    u = proj[:, C_UB:C_CQ]

    @pl.when(i == 0)
    def _():
        carry_ref[...] = jnp.zeros_like(carry_ref)

    uext = jnp.concatenate([carry_ref[...], u], axis=0)
    a1 = uext + pltpu.roll(uext, 1, 0)
    a2 = a1 + pltpu.roll(a1, 2, 0)
    a3 = a2 + pltpu.roll(a2, 4, 0)
    a4 = a3 + pltpu.roll(a3, 8, 0)
    lane = lax.broadcasted_iota(jnp.int32, (1, POOL_DIM), 1)
    wsum = jnp.where(lane < POOL_GROUP_DIM, a1,
                     jnp.where(lane < 2 * POOL_GROUP_DIM, a2,
                               jnp.where(lane < 3 * POOL_GROUP_DIM, a3, a4)))[POOL_HALO:]
    t = (i * tm + lax.broadcasted_iota(jnp.int32, (tm, 1), 0) + 1).astype(jnp.float32)
    count = jnp.minimum(t, v256_ref[2:3, :])
    d = wsum / count - u
    pool_ref[...] = (_dot(d.astype(bf16), wp_ref[...]) * v256_ref[1:2, :]).astype(bf16)
    carry_ref[...] = u[tm - POOL_HALO:, :]

    ang = pos_ref[...] * v128_ref[3:4, :]
    cos1, sin1 = jnp.cos(ang), jnp.sin(ang)
    m1 = v768_ref[3:4, :]
    m2 = v768_ref[4:5, :]

    def rope(xn, cos_t, sin_t, m1_t, m2_t):
        width = xn.shape[1]
        rot = pltpu.roll(xn, ROPE_HALF, 1) * m1_t + pltpu.roll(xn, width - ROPE_HALF, 1) * m2_t
        return xn * cos_t + rot * sin_t

    cq = proj[:, C_CQ:C_CKV]
    cqn = (_rms(cq, Q_LORA) * v256_ref[0:1, :]).astype(bf16)
    q = _dot(cqn, wqb_ref[...])
    ss = _dot((q * q).astype(bf16), g768_ref[...])
    qn = q * lax.rsqrt(ss * v768_ref[1:2, :] + EPS) * v768_ref[0:1, :]
    cos6 = jnp.concatenate([cos1] * MLA_HEADS, axis=1)
    sin6 = jnp.concatenate([sin1] * MLA_HEADS, axis=1)
    qm_ref[...] = rope(qn, cos6, sin6, m1, m2).astype(bf16)

    kr = proj[:, C_KR:IN_PAD_DIM]
    krn = _rms(kr, QK_ROPE) * v128_ref[2:3, :]
    krr = rope(krn, cos1, sin1, m1[:, 0:LANE], m2[:, 0:LANE])
    ckv = proj[:, C_CKV:C_KR]
    ckvn = (_rms(ckv, KV_LORA) * v128_ref[1:2, :]).astype(bf16)
    kv = _dot(ckvn, wkvb_ref[...])
    kn = kv[:, 0:MLA_PAD_DIM]
    ss = _dot((kn * kn).astype(bf16), g768_ref[...])
    kn = kn * lax.rsqrt(ss * (1.0 / QK_NOPE) + EPS) * v768_ref[2:3, :]
    km_ref[...] = (kn + jnp.concatenate([krr] * MLA_HEADS, axis=1)).astype(bf16)
    vm_ref[...] = kv[:, MLA_PAD_DIM:].astype(bf16)


def _const_spec(shape):
    nd = len(shape)
    return pl.BlockSpec(shape, lambda *_: (0,) * nd)


def _in_proj(x2, pos2, consts, batch, seq):
    n = x2.shape[0]
    tm = min(IN_TILE, seq)
    nt = seq // tm
    row = lambda b, i: (b * nt + i, 0)
    bf16 = jnp.bfloat16
    outs = [(SWA_Q_DIM, bf16), (SWA_KV_DIM, bf16), (SWA_KV_DIM, bf16), (POOL_DIM, bf16),
            (MLA_PAD_DIM, bf16), (MLA_PAD_DIM, bf16), (MLA_PAD_DIM, bf16)]
    return pl.pallas_call(
        _in_proj_kernel,
        grid=(batch, nt),
        in_specs=[pl.BlockSpec((tm, D_MODEL), row), pl.BlockSpec((tm, 1), row)]
                 + [_const_spec(c.shape) for c in consts],
        out_specs=[pl.BlockSpec((tm, w), row) for w, _ in outs],
        out_shape=[jax.ShapeDtypeStruct((n, w), dt) for w, dt in outs],
        scratch_shapes=[pltpu.VMEM((POOL_HALO, POOL_DIM), jnp.float32)],
        compiler_params=pltpu.CompilerParams(
            dimension_semantics=("arbitrary", "arbitrary"), vmem_limit_bytes=VMEM_LIMIT),
        name="in_proj",
    )(x2, pos2, *consts)


def _swa_kernel(sink_ref, q_ref, kc_ref, kp_ref, vc_ref, vp_ref, bias_ref, o_ref):
    i = pl.program_id(1)
    ts = q_ref.shape[0]
    bf16 = jnp.bfloat16
    q = q_ref[...]
    kext = jnp.concatenate([kp_ref[...], kc_ref[...]], axis=0)
    vext = jnp.concatenate([vp_ref[...], vc_ref[...]], axis=0)
    col = lax.broadcasted_iota(jnp.int32, (1, 2 * WINDOW), 1)
    no_prev = jnp.logical_and(i == 0, col < WINDOW)
    for c in range(ts // WINDOW):
        kb = kext[c * WINDOW:(c + 2) * WINDOW]
        vb = vext[c * WINDOW:(c + 2) * WINDOW]
        outs = []
        for h in range(SWA_HEADS):
            g = h // SWA_GROUP
            qh = q[c * WINDOW:(c + 1) * WINDOW, h * HEAD_DIM:(h + 1) * HEAD_DIM]
            s = _dot_nt(qh, kb[:, g * HEAD_DIM:(g + 1) * HEAD_DIM]) + bias_ref[h]
            if c == 0:
                s = jnp.where(no_prev, NEG, s)
            sink = sink_ref[h]
            m = jnp.maximum(jnp.max(s, axis=-1, keepdims=True), sink)
            p = jnp.exp(s - m)
            denom = jnp.sum(p, axis=-1, keepdims=True) + jnp.exp(sink - m)
            o = _dot(p.astype(bf16), vb[:, g * HEAD_DIM:(g + 1) * HEAD_DIM])
            outs.append(o / denom)
        o_ref[c * WINDOW:(c + 1) * WINDOW, :] = jnp.concatenate(outs, axis=1).astype(bf16)


def _swa(sinks, qs, ks, vs, bias, batch, seq):
    n = qs.shape[0]
    ts = min(SWA_TILE, seq)
    nt = seq // ts
    wpt = ts // WINDOW
    wps = seq // WINDOW
    row = lambda b, i: (b * nt + i, 0)
    prev = lambda b, i: (b * wps + jnp.maximum(i * wpt - 1, 0), 0)
    return pl.pallas_call(
        _swa_kernel,
        grid=(batch, nt),
        in_specs=[pl.BlockSpec(memory_space=pltpu.SMEM),
                  pl.BlockSpec((ts, SWA_Q_DIM), row),
                  pl.BlockSpec((ts, SWA_KV_DIM), row), pl.BlockSpec((WINDOW, SWA_KV_DIM), prev),
                  pl.BlockSpec((ts, SWA_KV_DIM), row), pl.BlockSpec((WINDOW, SWA_KV_DIM), prev),
                  _const_spec(bias.shape)],
        out_specs=pl.BlockSpec((ts, SWA_Q_DIM), row),
        out_shape=jax.ShapeDtypeStruct((n, SWA_Q_DIM), jnp.bfloat16),
        compiler_params=pltpu.CompilerParams(
            dimension_semantics=("arbitrary", "arbitrary"), vmem_limit_bytes=VMEM_LIMIT),
        name="swa",
    )(sinks, qs, ks, ks, vs, vs, bias)


def _mla_kernel(q_ref, k_ref, v_ref, o_ref, m_ref, l_ref, acc_ref):
    i = pl.program_id(2)
    tq = q_ref.shape[0]
    tk = tq
    bf16 = jnp.bfloat16
    m_ref[...] = jnp.full(m_ref.shape, NEG, jnp.float32)
    l_ref[...] = jnp.zeros(l_ref.shape, jnp.float32)
    acc_ref[...] = jnp.zeros(acc_ref.shape, jnp.float32)

    def step(start, size, masked):
        for a in range(2):
            lanes = slice(a * LANE, (a + 1) * LANE)
            s = _dot_nt(q_ref[:, lanes], k_ref[pl.ds(start, size), lanes])
            if masked:
                r = lax.broadcasted_iota(jnp.int32, (tq, size), 0)
                c = lax.broadcasted_iota(jnp.int32, (tq, size), 1)
                s = jnp.where(r >= c, s, NEG)
            m_prev = m_ref[a]
            m_new = jnp.maximum(m_prev, jnp.max(s, axis=-1, keepdims=True))
            alpha = jnp.exp2(m_prev - m_new)
            p = jnp.exp2(s - jnp.tile(m_new, (1, size // LANE)))
            part = p[:, 0:LANE]
            for t in range(1, size // LANE):
                part = part + p[:, t * LANE:(t + 1) * LANE]
            l_ref[a] = alpha * l_ref[a] + part
            acc_ref[a] = alpha * acc_ref[a] + _dot(p.astype(bf16),
                                                   v_ref[pl.ds(start, size), lanes])
            m_ref[a] = m_new

    def body(j, carry):
        step(pl.multiple_of(j * (2 * tk), 2 * tk), 2 * tk, False)
        return carry

    lax.fori_loop(0, i // 2, body, 0)

    @pl.when(i % 2 == 1)
    def _():
        step(pl.multiple_of((i - 1) * tk, tk), tk, False)

    step(pl.multiple_of(i * tk, tk), tk, True)
    l0 = jnp.sum(l_ref[0], axis=-1, keepdims=True)
    l1 = jnp.sum(l_ref[1], axis=-1, keepdims=True)
    o_ref[...] = (acc_ref[0] / l0 + acc_ref[1] / l1).astype(bf16)


def _mla(qm, km, vm, batch, seq):
    n = qm.shape[0]
    tq = min(MLA_TQ, seq)
    nq = seq // tq
    pairs = MLA_HEADS // 2
    return pl.pallas_call(
        _mla_kernel,
        grid=(batch, pairs, nq),
        in_specs=[pl.BlockSpec((tq, 2 * LANE), lambda b, p, i: (b * nq + i, p)),
                  pl.BlockSpec((seq, 2 * LANE), lambda b, p, i: (b, p)),
                  pl.BlockSpec((seq, 2 * LANE), lambda b, p, i: (b, p))],
        out_specs=pl.BlockSpec((tq, LANE), lambda b, p, i: (b * nq + i, p)),
        out_shape=jax.ShapeDtypeStruct((n, MLA_OUT_DIM), jnp.bfloat16),
        scratch_shapes=[pltpu.VMEM((2, tq, LANE), jnp.float32),
                        pltpu.VMEM((2, tq, LANE), jnp.float32),
                        pltpu.VMEM((2, tq, LANE), jnp.float32)],
        compiler_params=pltpu.CompilerParams(
            dimension_semantics=("arbitrary", "arbitrary", "arbitrary"),
            vmem_limit_bytes=VMEM_LIMIT),
        name="mla",
    )(qm, km, vm)


def _out_ffn_kernel(x_ref, a_ref, b_ref, c_ref, wo_ref, g_ref, wg_ref, wu_ref, wd_ref, o_ref):
    bf16 = jnp.bfloat16
    mixed = jnp.concatenate([a_ref[...], b_ref[...], c_ref[...]], axis=1)
    x1 = x_ref[...] + _dot(mixed, wo_ref[...])
    h = (_rms(x1, D_MODEL) * g_ref[...]).astype(bf16)
    acc = x1
    for start, size in FFN_CHUNKS:
        gate = _dot(h, wg_ref[:, start:start + size])
        up = _dot(h, wu_ref[:, start:start + size])
        act = (gate * jax.nn.sigmoid(gate) * up).astype(bf16)
        acc = acc + _dot(act, wd_ref[start:start + size, :])
    o_ref[...] = acc


def _resident(shape):
    nd = len(shape)
    return pl.BlockSpec(shape, lambda *_: (0,) * nd, pipeline_mode=pl.Buffered(1))


def _out_ffn(x2, oa, ob, oc, wo, g, wg, wu, wd):
    n = x2.shape[0]
    tm = min(FFN_TILE, n)
    row = lambda i: (i, 0)
    return pl.pallas_call(
        _out_ffn_kernel,
        grid=(n // tm,),
        in_specs=[pl.BlockSpec((tm, D_MODEL), row), pl.BlockSpec((tm, SWA_Q_DIM), row),
                  pl.BlockSpec((tm, POOL_DIM), row), pl.BlockSpec((tm, MLA_OUT_DIM), row),
                  _resident(wo.shape), _resident(g.shape), _resident(wg.shape),
                  _resident(wu.shape), _resident(wd.shape)],
        out_specs=pl.BlockSpec((tm, D_MODEL), row),
        out_shape=jax.ShapeDtypeStruct((n, D_MODEL), jnp.float32),
        compiler_params=pltpu.CompilerParams(
            dimension_semantics=("arbitrary",), vmem_limit_bytes=VMEM_LIMIT),
        name="out_ffn",
    )(x2, oa, ob, oc, wo, g, wg, wu, wd)


def _block_diag_ones(groups, width):
    g = np.zeros((width, width), np.float32)
    for lo, hi in groups:
        g[lo:hi, lo:hi] = 1.0
    return jnp.asarray(g, jnp.bfloat16)


def _lane_rows(rows, width):
    out = jnp.zeros((8, width), jnp.float32)
    for r, v in enumerate(rows):
        out = out.at[r].set(v.astype(jnp.float32))
    return out


def _head_lanes(nope, rope):
    pad = jnp.zeros((MLA_HEAD_PAD - QK_HEAD,), jnp.float32)
    return jnp.tile(jnp.concatenate([nope.astype(jnp.float32), rope.astype(jnp.float32), pad]),
                    MLA_HEADS)


def _layer_consts(l, attn_norm, w_in, swa_q_gain, swa_k_gain, pool_w, pool_scale, mla_q_a_gain,
                  mla_w_qb, mla_kv_a_gain, mla_w_kvb, mla_q_nope_gain, mla_q_rope_gain,
                  mla_k_nope_gain, mla_k_rope_gain):
    f32, bf16 = jnp.float32, jnp.bfloat16
    zeros = lambda r, c: jnp.zeros((r, c), f32)
    split = C_KR
    w1 = jnp.concatenate([w_in[l][:, :split], zeros(D_MODEL, QK_NOPE), w_in[l][:, split:],
                          zeros(D_MODEL, LANE - QK_HEAD)], axis=1).astype(bf16)
    wqb = mla_w_qb[l].reshape(Q_LORA, MLA_HEADS, QK_HEAD)
    wqb = jnp.pad(wqb, ((0, 0), (0, 0), (0, MLA_HEAD_PAD - QK_HEAD))).reshape(Q_LORA, MLA_PAD_DIM)
    wkvb = mla_w_kvb[l].reshape(KV_LORA, MLA_HEADS, QK_NOPE + V_HEAD)
    wk = jnp.pad(wkvb[:, :, :QK_NOPE], ((0, 0), (0, 0), (0, MLA_HEAD_PAD - QK_NOPE)))
    wv = wkvb[:, :, QK_NOPE:]
    zv = jnp.zeros_like(wv)
    odd = (jnp.arange(MLA_HEADS) % 2 == 1)[None, :, None]
    wv = jnp.concatenate([jnp.where(odd, zv, wv), jnp.where(odd, wv, zv)], axis=2)
    wkvb_p = jnp.concatenate([wk.reshape(KV_LORA, MLA_PAD_DIM), wv.reshape(KV_LORA, MLA_PAD_DIM)],
                             axis=1)
    wp = jnp.zeros((POOL_DIM, POOL_DIM), f32)
    for g in range(len(POOL_WINDOWS)):
        sl = slice(g * POOL_GROUP_DIM, (g + 1) * POOL_GROUP_DIM)
        wp = wp.at[sl, sl].set(pool_w[l, g])
    ones_n, ones_r = jnp.ones((QK_NOPE,), f32), jnp.ones((QK_ROPE,), f32)
    half = jnp.concatenate([jnp.zeros((ROPE_HALF,), f32), jnp.ones((ROPE_HALF,), f32)])
    v768 = _lane_rows([
        _head_lanes(mla_q_nope_gain[l], mla_q_rope_gain[l]) * (QK_HEAD ** -0.5 * math.log2(math.e)),
        _head_lanes(ones_n / QK_NOPE, ones_r / QK_ROPE),
        _head_lanes(mla_k_nope_gain[l], 0.0 * ones_r),
        _head_lanes(0.0 * ones_n, half),
        _head_lanes(0.0 * ones_n, half - 1.0),
    ], MLA_PAD_DIM)
    v384 = (jnp.tile(swa_q_gain[l], SWA_HEADS) * (HEAD_DIM ** -0.5)).reshape(1, SWA_Q_DIM)
    windows = jnp.repeat(jnp.asarray(POOL_WINDOWS, f32), POOL_GROUP_DIM)
    v256 = _lane_rows([mla_q_a_gain[l], pool_scale[l], windows], POOL_DIM)
    inv_freq = ROPE_THETA ** (-jnp.arange(0, QK_ROPE, 2, dtype=f32) / QK_ROPE)
    one_head = lambda rope: _head_lanes(jnp.zeros((QK_NOPE,), f32), rope)[:LANE]
    v128 = _lane_rows([jnp.tile(swa_k_gain[l], SWA_KV_HEADS), mla_kv_a_gain[l],
                       one_head(mla_k_rope_gain[l]), one_head(jnp.tile(inv_freq, 2))], LANE)
    g384 = _block_diag_ones([(h * HEAD_DIM, (h + 1) * HEAD_DIM) for h in range(SWA_HEADS)],
                            SWA_Q_DIM)
    groups = []
    for h in range(MLA_HEADS):
        base = h * MLA_HEAD_PAD
        groups += [(base, base + QK_NOPE), (base + QK_NOPE, base + QK_HEAD)]
    g768 = _block_diag_ones(groups, MLA_PAD_DIM)
    return [attn_norm[l].reshape(1, D_MODEL), w1, g384, g768, v768, v384, v256, v128,
            wp.astype(bf16), wqb.astype(bf16), wkvb_p.astype(bf16)]


def kernel(x, positions, rel_bias, attn_norm, w_in, swa_q_gain, swa_k_gain, swa_sinks, pool_w,
           pool_scale, mla_q_a_gain, mla_w_qb, mla_kv_a_gain, mla_w_kvb, mla_q_nope_gain,
           mla_q_rope_gain, mla_k_nope_gain, mla_k_rope_gain, w_out, ffn_norm, w_gate, w_up,
           w_down):
    batch, seq, _ = x.shape
    n = batch * seq
    depth = w_in.shape[0]
    bf16 = jnp.bfloat16
    x2 = x.reshape(n, D_MODEL)
    pos2 = positions.astype(jnp.float32).reshape(n, 1)
    bias = _bias_table(rel_bias)
    for l in range(depth):
        consts = _layer_consts(l, attn_norm, w_in, swa_q_gain, swa_k_gain, pool_w, pool_scale,
                               mla_q_a_gain, mla_w_qb, mla_kv_a_gain, mla_w_kvb, mla_q_nope_gain,
                               mla_q_rope_gain, mla_k_nope_gain, mla_k_rope_gain)
        qs, ks, vs, pool_o, qm, km, vm = _in_proj(x2, pos2, consts, batch, seq)
        swa_o = _swa(swa_sinks[l], qs, ks, vs, bias, batch, seq)
        mla_o = _mla(qm, km, vm, batch, seq)
        x2 = _out_ffn(x2, swa_o, pool_o, mla_o, w_out[l].astype(bf16),
                      ffn_norm[l].reshape(1, D_MODEL), w_gate[l].astype(bf16),
                      w_up[l].astype(bf16), w_down[l].astype(bf16))
    return x2.reshape(batch, seq, D_MODEL)
```

```python
import functools
import math

import numpy as np
import jax
import jax.numpy as jnp
from jax import lax
from jax.experimental import pallas as pl
from jax.experimental.pallas import tpu as pltpu

D_MODEL = 1024
HEAD_DIM = 64
SWA_HEADS = 6
SWA_KV_HEADS = 2
SWA_GROUP = SWA_HEADS // SWA_KV_HEADS
WINDOW = 128
POOL_WINDOWS = (2, 4, 8, 16)
POOL_GROUP_DIM = 64
POOL_DIM = POOL_GROUP_DIM * len(POOL_WINDOWS)
MLA_HEADS = 6
Q_LORA = 256
KV_LORA = 128
QK_NOPE = 64
QK_ROPE = 32
V_HEAD = 64
QK_HEAD = QK_NOPE + QK_ROPE
ROPE_THETA = 10000.0
N_BUCKETS = 32
MAX_DISTANCE = 128
SWA_Q_DIM = SWA_HEADS * HEAD_DIM
SWA_KV_DIM = SWA_KV_HEADS * HEAD_DIM
MLA_OUT_DIM = MLA_HEADS * V_HEAD
D_FF = 2816
EPS = 1e-6

LANE = 128
MLA_HEAD_PAD = LANE
MLA_PAD_DIM = MLA_HEADS * MLA_HEAD_PAD
ROPE_HALF = QK_ROPE // 2
POOL_HALO = max(POOL_WINDOWS)
NEG = -1e30

C_QA = 0
C_KA = C_QA + SWA_Q_DIM
C_VA = C_KA + SWA_KV_DIM
C_UB = C_VA + SWA_KV_DIM
C_CQ = C_UB + POOL_DIM
C_CKV = C_CQ + Q_LORA
C_KR = C_CKV + KV_LORA
IN_PAD_DIM = C_KR + LANE

IN_TILE = 512
SWA_TILE = 512
MLA_TQ = 512
FFN_TILE = 512
FFN_CHUNKS = ((0, 1536), (1536, 1280))
VMEM_LIMIT = 56 * 1024 * 1024


def _dot(a, b):
    return jnp.dot(a, b, preferred_element_type=jnp.float32)


def _dot_nt(a, b):
    return lax.dot_general(a, b, (((1,), (1,)), ((), ())), preferred_element_type=jnp.float32)


def _rms(x, width):
    return x * lax.rsqrt(jnp.sum(x * x, axis=-1, keepdims=True) * (1.0 / width) + EPS)


def _bias_kernel(rel_ref, bucket_ref, out_ref):
    bucket = bucket_ref[...]
    for h in range(SWA_HEADS):
        acc = jnp.full(bucket.shape, NEG, jnp.float32)
        for b in range(N_BUCKETS):
            acc = jnp.where(bucket == b, rel_ref[b, h], acc)
        out_ref[h] = acc


def _band_buckets():
    q_loc = np.arange(WINDOW)[:, None]
    k_loc = np.arange(2 * WINDOW)[None, :]
    dist = q_loc + WINDOW - k_loc
    band_ok = (dist >= 0) & (dist < WINDOW)
    n = np.maximum(dist, 0)
    max_exact = N_BUCKETS // 2
    nf = np.maximum(n, 1).astype(np.float32)
    large = max_exact + (np.log(nf / max_exact) / math.log(MAX_DISTANCE / max_exact)
                         * (N_BUCKETS - max_exact)).astype(np.int32)
    large = np.minimum(large, N_BUCKETS - 1)
    bucket = np.where(n < max_exact, n, large)
    return np.where(band_ok, bucket, -1).astype(np.int32)


def _bias_table(rel_bias):
    return pl.pallas_call(
        _bias_kernel,
        out_shape=jax.ShapeDtypeStruct((SWA_HEADS, WINDOW, 2 * WINDOW), jnp.float32),
        in_specs=[pl.BlockSpec(memory_space=pltpu.SMEM),
                  pl.BlockSpec(memory_space=pltpu.VMEM)],
        out_specs=pl.BlockSpec(memory_space=pltpu.VMEM),
        name="bias_table",
    )(rel_bias, jnp.asarray(_band_buckets()))


def _in_proj_kernel(x_ref, pos_ref, g_attn_ref, w1_ref, g384_ref, gsum_ref, gexp_ref, v768_ref,
                    v384_ref, v256_ref, v128_ref, invf_ref, wp_ref, wqb_ref, wkvb_ref,
                    qs_ref, ks_ref, vs_ref, pool_ref, qm_ref, km_ref, vm_ref, carry_ref):
    i = pl.program_id(1)
    tm = x_ref.shape[0]
    bf16 = jnp.bfloat16

    x = x_ref[...]
    h = (_rms(x, D_MODEL) * g_attn_ref[...]).astype(bf16)
    proj = _dot(h, w1_ref[...])

    qa = proj[:, C_QA:C_KA]
    ss = _dot((qa * qa).astype(bf16), g384_ref[...])
    qs_ref[...] = (qa * lax.rsqrt(ss * (1.0 / HEAD_DIM) + EPS) * v384_ref[...]).astype(bf16)
    ka = proj[:, C_KA:C_VA]
    ss = _dot((ka * ka).astype(bf16), g384_ref[0:SWA_KV_DIM, 0:SWA_KV_DIM])
    ks_ref[...] = (ka * lax.rsqrt(ss * (1.0 / HEAD_DIM) + EPS) * v128_ref[0:1, :]).astype(bf16)
    vs_ref[...] = proj[:, C_VA:C_UB].astype(bf16)

    u = proj[:, C_UB:C_CQ]

    @pl.when(i == 0)
    def _():
        carry_ref[...] = jnp.zeros_like(carry_ref)

    uext = jnp.concatenate([carry_ref[...], u], axis=0)
    a1 = uext + pltpu.roll(uext, 1, 0)
    a2 = a1 + pltpu.roll(a1, 2, 0)
    a3 = a2 + pltpu.roll(a2, 4, 0)
    a4 = a3 + pltpu.roll(a3, 8, 0)
    lane = lax.broadcasted_iota(jnp.int32, (1, POOL_DIM), 1)
    wsum = jnp.where(lane < POOL_GROUP_DIM, a1,
                     jnp.where(lane < 2 * POOL_GROUP_DIM, a2,
                               jnp.where(lane < 3 * POOL_GROUP_DIM, a3, a4)))[POOL_HALO:]
    t = (i * tm + lax.broadcasted_iota(jnp.int32, (tm, 1), 0) + 1).astype(jnp.float32)
    count = jnp.minimum(t, v256_ref[2:3, :])
    d = wsum / count - u
    pool_ref[...] = (_dot(d.astype(bf16), wp_ref[...]) * v256_ref[1:2, :]).astype(bf16)
    carry_ref[...] = u[tm - POOL_HALO:, :]

    ang_t = invf_ref[...] * pos_ref[...]
    cos_t, sin_t = jnp.cos(ang_t), jnp.sin(ang_t)
    one_t = jnp.ones((QK_NOPE, tm), jnp.float32)
    zero_t = jnp.zeros((QK_NOPE, tm), jnp.float32)
    tail = LANE - QK_HEAD
    cos1 = jnp.concatenate([one_t, cos_t, cos_t, one_t[0:tail]], axis=0).T
    sin1 = jnp.concatenate([zero_t, sin_t, sin_t, zero_t[0:tail]], axis=0).T
    m1 = v768_ref[3:4, :]
    m2 = v768_ref[4:5, :]

    def rope(xn, cos_t, sin_t, m1_t, m2_t):
        width = xn.shape[1]
        rot = pltpu.roll(xn, ROPE_HALF, 1) * m1_t + pltpu.roll(xn, width - ROPE_HALF, 1) * m2_t
        return xn * cos_t + rot * sin_t

    def group_sums(v):
        compact = _dot((v * v).astype(bf16), gsum_ref[...])
        return _dot(compact.astype(bf16), gexp_ref[...])

    cq = proj[:, C_CQ:C_CKV]
    cqn = (_rms(cq, Q_LORA) * v256_ref[0:1, :]).astype(bf16)
    q = _dot(cqn, wqb_ref[...])
    qn = q * lax.rsqrt(group_sums(q) * v768_ref[1:2, :] + EPS) * v768_ref[0:1, :]
    cos6 = jnp.concatenate([cos1] * MLA_HEADS, axis=1)
    sin6 = jnp.concatenate([sin1] * MLA_HEADS, axis=1)
    qm_ref[...] = rope(qn, cos6, sin6, m1, m2).astype(bf16)

    kr = proj[:, C_KR:IN_PAD_DIM]
    krn = _rms(kr, QK_ROPE) * v128_ref[2:3, :]
    krr = rope(krn, cos1, sin1, m1[:, 0:LANE], m2[:, 0:LANE])
    ckv = proj[:, C_CKV:C_KR]
    ckvn = (_rms(ckv, KV_LORA) * v128_ref[1:2, :]).astype(bf16)
    kv = _dot(ckvn, wkvb_ref[...])
    kn = kv[:, 0:MLA_PAD_DIM]
    kn = kn * lax.rsqrt(group_sums(kn) * (1.0 / QK_NOPE) + EPS) * v768_ref[2:3, :]
    km_ref[...] = (kn + jnp.concatenate([krr] * MLA_HEADS, axis=1)).astype(bf16)
    vm_ref[...] = (kv[:, MLA_PAD_DIM:] + v768_ref[5:6, :]).astype(bf16)


def _const_spec(shape):
    nd = len(shape)
    return pl.BlockSpec(shape, lambda *_: (0,) * nd)


def _in_proj(x2, pos2, consts, batch, seq):
    n = x2.shape[0]
    tm = min(IN_TILE, seq)
    nt = seq // tm
    row = lambda b, i: (b * nt + i, 0)
    bf16 = jnp.bfloat16
    outs = [(SWA_Q_DIM, bf16), (SWA_KV_DIM, bf16), (SWA_KV_DIM, bf16), (POOL_DIM, bf16),
            (MLA_PAD_DIM, bf16), (MLA_PAD_DIM, bf16), (MLA_PAD_DIM, bf16)]
    return pl.pallas_call(
        _in_proj_kernel,
        grid=(batch, nt),
        in_specs=[pl.BlockSpec((tm, D_MODEL), row),
                  pl.BlockSpec((1, tm), lambda b, i: (0, b * nt + i))]
                 + [_const_spec(c.shape) for c in consts],
        out_specs=[pl.BlockSpec((tm, w), row) for w, _ in outs],
        out_shape=[jax.ShapeDtypeStruct((n, w), dt) for w, dt in outs],
        scratch_shapes=[pltpu.VMEM((POOL_HALO, POOL_DIM), jnp.float32)],
        compiler_params=pltpu.CompilerParams(
            dimension_semantics=("arbitrary", "arbitrary"), vmem_limit_bytes=VMEM_LIMIT),
        name="in_proj",
    )(x2, pos2, *consts)


def _swa_kernel(sink_ref, q_ref, kc_ref, kp_ref, vc_ref, vp_ref, bias_ref, o_ref):
    i = pl.program_id(1)
    ts = q_ref.shape[0]
    bf16 = jnp.bfloat16
    q = q_ref[...]
    kext = jnp.concatenate([kp_ref[...], kc_ref[...]], axis=0)
    vext = jnp.concatenate([vp_ref[...], vc_ref[...]], axis=0)
    col = lax.broadcasted_iota(jnp.int32, (1, 2 * WINDOW), 1)
    no_prev = jnp.logical_and(i == 0, col < WINDOW)
    for c in range(ts // WINDOW):
        kb = kext[c * WINDOW:(c + 2) * WINDOW]
        vb = vext[c * WINDOW:(c + 2) * WINDOW]
        outs = []
        for h in range(SWA_HEADS):
            g = h // SWA_GROUP
            qh = q[c * WINDOW:(c + 1) * WINDOW, h * HEAD_DIM:(h + 1) * HEAD_DIM]
            s = _dot_nt(qh, kb[:, g * HEAD_DIM:(g + 1) * HEAD_DIM]) + bias_ref[h]
            if c == 0:
                s = jnp.where(no_prev, NEG, s)
            sink = sink_ref[h]
            m = jnp.maximum(jnp.max(s, axis=-1, keepdims=True), sink)
            p = jnp.exp(s - m)
            denom = jnp.sum(p, axis=-1, keepdims=True) + jnp.exp(sink - m)
            o = _dot(p.astype(bf16), vb[:, g * HEAD_DIM:(g + 1) * HEAD_DIM])
            outs.append(o / denom)
        o_ref[c * WINDOW:(c + 1) * WINDOW, :] = jnp.concatenate(outs, axis=1).astype(bf16)


def _swa(sinks, qs, ks, vs, bias, batch, seq):
    n = qs.shape[0]
    ts = min(SWA_TILE, seq)
    nt = seq // ts
    wpt = ts // WINDOW
    wps = seq // WINDOW
    row = lambda b, i: (b * nt + i, 0)
    prev = lambda b, i: (b * wps + jnp.maximum(i * wpt - 1, 0), 0)
    return pl.pallas_call(
        _swa_kernel,
        grid=(batch, nt),
        in_specs=[pl.BlockSpec(memory_space=pltpu.SMEM),
                  pl.BlockSpec((ts, SWA_Q_DIM), row),
                  pl.BlockSpec((ts, SWA_KV_DIM), row), pl.BlockSpec((WINDOW, SWA_KV_DIM), prev),
                  pl.BlockSpec((ts, SWA_KV_DIM), row), pl.BlockSpec((WINDOW, SWA_KV_DIM), prev),
                  _const_spec(bias.shape)],
        out_specs=pl.BlockSpec((ts, SWA_Q_DIM), row),
        out_shape=jax.ShapeDtypeStruct((n, SWA_Q_DIM), jnp.bfloat16),
        compiler_params=pltpu.CompilerParams(
            dimension_semantics=("arbitrary", "arbitrary"), vmem_limit_bytes=VMEM_LIMIT),
        name="swa",
    )(sinks, qs, ks, ks, vs, vs, bias)


def _mla_kernel(q_ref, k_ref, v_ref, o_ref, m_ref, acc_ref):
    i = pl.program_id(2)
    tq = q_ref.shape[0]
    tk = tq
    bf16 = jnp.bfloat16
    m_ref[...] = jnp.full(m_ref.shape, NEG, jnp.float32)
    acc_ref[...] = jnp.zeros(acc_ref.shape, jnp.float32)

    def step(start, size, masked):
        for a in range(2):
            lanes = slice(a * LANE, (a + 1) * LANE)
            s = _dot_nt(q_ref[:, lanes], k_ref[pl.ds(start, size), lanes])
            if masked:
                r = lax.broadcasted_iota(jnp.int32, (tq, size), 0)
                c = lax.broadcasted_iota(jnp.int32, (tq, size), 1)
                s = jnp.where(r >= c, s, NEG)
            m_prev = m_ref[a]
            m_new = jnp.maximum(m_prev, jnp.max(s, axis=-1, keepdims=True))
            alpha = jnp.exp2(m_prev - m_new)
            p = jnp.exp2(s - jnp.tile(m_new, (1, size // LANE)))
            acc_ref[a] = alpha * acc_ref[a] + _dot(p.astype(bf16),
                                                   v_ref[pl.ds(start, size), lanes])
            m_ref[a] = m_new

    def body(j, carry):
        step(pl.multiple_of(j * (2 * tk), 2 * tk), 2 * tk, False)
        return carry

    lax.fori_loop(0, i // 2, body, 0)

    @pl.when(i % 2 == 1)
    def _():
        step(pl.multiple_of((i - 1) * tk, tk), tk, False)

    step(pl.multiple_of(i * tk, tk), tk, True)
    acc0, acc1 = acc_ref[0], acc_ref[1]
    o0 = acc0 / acc0[:, V_HEAD:V_HEAD + 1]
    o1 = acc1 / acc1[:, 0:1]
    lane = lax.broadcasted_iota(jnp.int32, (1, LANE), 1)
    o_ref[...] = jnp.where(lane < V_HEAD, o0, o1).astype(bf16)


def _mla(qm, km, vm, batch, seq):
    n = qm.shape[0]
    tq = min(MLA_TQ, seq)
    nq = seq // tq
    pairs = MLA_HEADS // 2
    return pl.pallas_call(
        _mla_kernel,
        grid=(batch, pairs, nq),
        in_specs=[pl.BlockSpec((tq, 2 * LANE), lambda b, p, i: (b * nq + i, p)),
                  pl.BlockSpec((seq, 2 * LANE), lambda b, p, i: (b, p)),
                  pl.BlockSpec((seq, 2 * LANE), lambda b, p, i: (b, p))],
        out_specs=pl.BlockSpec((tq, LANE), lambda b, p, i: (b * nq + i, p)),
        out_shape=jax.ShapeDtypeStruct((n, MLA_OUT_DIM), jnp.bfloat16),
        scratch_shapes=[pltpu.VMEM((2, tq, LANE), jnp.float32),
                        pltpu.VMEM((2, tq, LANE), jnp.float32)],
        compiler_params=pltpu.CompilerParams(
            dimension_semantics=("arbitrary", "arbitrary", "arbitrary"),
            vmem_limit_bytes=VMEM_LIMIT),
        name="mla",
    )(qm, km, vm)


def _out_ffn_kernel(x_ref, a_ref, b_ref, c_ref, wo_ref, g_ref, wg_ref, wu_ref, wd_ref, o_ref):
    bf16 = jnp.bfloat16
    mixed = jnp.concatenate([a_ref[...], b_ref[...], c_ref[...]], axis=1)
    x1 = x_ref[...] + _dot(mixed, wo_ref[...])
    h = (_rms(x1, D_MODEL) * g_ref[...]).astype(bf16)
    acc = x1
    for start, size in FFN_CHUNKS:
        gate = _dot(h, wg_ref[:, start:start + size])
        up = _dot(h, wu_ref[:, start:start + size])
        act = (gate * jax.nn.sigmoid(gate) * up).astype(bf16)
        acc = acc + _dot(act, wd_ref[start:start + size, :])
    o_ref[...] = acc


def _resident(shape):
    nd = len(shape)
    return pl.BlockSpec(shape, lambda *_: (0,) * nd, pipeline_mode=pl.Buffered(1))


def _out_ffn(x2, oa, ob, oc, wo, g, wg, wu, wd):
    n = x2.shape[0]
    tm = min(FFN_TILE, n)
    row = lambda i: (i, 0)
    return pl.pallas_call(
        _out_ffn_kernel,
        grid=(n // tm,),
        in_specs=[pl.BlockSpec((tm, D_MODEL), row), pl.BlockSpec((tm, SWA_Q_DIM), row),
                  pl.BlockSpec((tm, POOL_DIM), row), pl.BlockSpec((tm, MLA_OUT_DIM), row),
                  _resident(wo.shape), _resident(g.shape), _resident(wg.shape),
                  _resident(wu.shape), _resident(wd.shape)],
        out_specs=pl.BlockSpec((tm, D_MODEL), row),
        out_shape=jax.ShapeDtypeStruct((n, D_MODEL), jnp.float32),
        compiler_params=pltpu.CompilerParams(
            dimension_semantics=("arbitrary",), vmem_limit_bytes=VMEM_LIMIT),
        name="out_ffn",
    )(x2, oa, ob, oc, wo, g, wg, wu, wd)


def _block_diag_ones(groups, width):
    g = np.zeros((width, width), np.float32)
    for lo, hi in groups:
        g[lo:hi, lo:hi] = 1.0
    return jnp.asarray(g, jnp.bfloat16)


def _lane_rows(rows, width):
    out = jnp.zeros((8, width), jnp.float32)
    for r, v in enumerate(rows):
        out = out.at[r].set(v.astype(jnp.float32))
    return out


def _head_lanes(nope, rope):
    pad = jnp.zeros((MLA_HEAD_PAD - QK_HEAD,), jnp.float32)
    return jnp.tile(jnp.concatenate([nope.astype(jnp.float32), rope.astype(jnp.float32), pad]),
                    MLA_HEADS)


def _layer_consts(l, attn_norm, w_in, swa_q_gain, swa_k_gain, pool_w, pool_scale, mla_q_a_gain,
                  mla_w_qb, mla_kv_a_gain, mla_w_kvb, mla_q_nope_gain, mla_q_rope_gain,
                  mla_k_nope_gain, mla_k_rope_gain):
    f32, bf16 = jnp.float32, jnp.bfloat16
    zeros = lambda r, c: jnp.zeros((r, c), f32)
    split = C_KR
    w1 = jnp.concatenate([w_in[l][:, :split], zeros(D_MODEL, QK_NOPE), w_in[l][:, split:],
                          zeros(D_MODEL, LANE - QK_HEAD)], axis=1).astype(bf16)
    wqb = mla_w_qb[l].reshape(Q_LORA, MLA_HEADS, QK_HEAD)
    wqb = jnp.pad(wqb, ((0, 0), (0, 0), (0, MLA_HEAD_PAD - QK_HEAD))).reshape(Q_LORA, MLA_PAD_DIM)
    wkvb = mla_w_kvb[l].reshape(KV_LORA, MLA_HEADS, QK_NOPE + V_HEAD)
    wk = jnp.pad(wkvb[:, :, :QK_NOPE], ((0, 0), (0, 0), (0, MLA_HEAD_PAD - QK_NOPE)))
    wv = wkvb[:, :, QK_NOPE:]
    zv = jnp.zeros_like(wv)
    odd = (jnp.arange(MLA_HEADS) % 2 == 1)[None, :, None]
    wv = jnp.concatenate([jnp.where(odd, zv, wv), jnp.where(odd, wv, zv)], axis=2)
    wkvb_p = jnp.concatenate([wk.reshape(KV_LORA, MLA_PAD_DIM), wv.reshape(KV_LORA, MLA_PAD_DIM)],
                             axis=1)
    wp = jnp.zeros((POOL_DIM, POOL_DIM), f32)
    for g in range(len(POOL_WINDOWS)):
        sl = slice(g * POOL_GROUP_DIM, (g + 1) * POOL_GROUP_DIM)
        wp = wp.at[sl, sl].set(pool_w[l, g])
    ones_n, ones_r = jnp.ones((QK_NOPE,), f32), jnp.ones((QK_ROPE,), f32)
    half = jnp.concatenate([jnp.zeros((ROPE_HALF,), f32), jnp.ones((ROPE_HALF,), f32)])
    v768 = _lane_rows([
        _head_lanes(mla_q_nope_gain[l], mla_q_rope_gain[l]) * (QK_HEAD ** -0.5 * math.log2(math.e)),
        _head_lanes(ones_n / QK_NOPE, ones_r / QK_ROPE),
        _head_lanes(mla_k_nope_gain[l], 0.0 * ones_r),
        _head_lanes(0.0 * ones_n, half),
        _head_lanes(0.0 * ones_n, half - 1.0),
        jnp.tile(jnp.zeros((2 * MLA_HEAD_PAD,), f32).at[V_HEAD].set(1.0).at[MLA_HEAD_PAD].set(1.0),
                 MLA_HEADS // 2),
    ], MLA_PAD_DIM)
    v384 = (jnp.tile(swa_q_gain[l], SWA_HEADS) * (HEAD_DIM ** -0.5)).reshape(1, SWA_Q_DIM)
    windows = jnp.repeat(jnp.asarray(POOL_WINDOWS, f32), POOL_GROUP_DIM)
    v256 = _lane_rows([mla_q_a_gain[l], pool_scale[l], windows], POOL_DIM)
    inv_freq = ROPE_THETA ** (-jnp.arange(0, QK_ROPE, 2, dtype=f32) / QK_ROPE)
    one_head = lambda rope: _head_lanes(jnp.zeros((QK_NOPE,), f32), rope)[:LANE]
    v128 = _lane_rows([jnp.tile(swa_k_gain[l], SWA_KV_HEADS), mla_kv_a_gain[l],
                       one_head(mla_k_rope_gain[l])], LANE)
    g384 = _block_diag_ones([(h * HEAD_DIM, (h + 1) * HEAD_DIM) for h in range(SWA_HEADS)],
                            SWA_Q_DIM)
    gsum = np.zeros((MLA_PAD_DIM, LANE), np.float32)
    for h in range(MLA_HEADS):
        base = h * MLA_HEAD_PAD
        gsum[base:base + QK_NOPE, 2 * h] = 1.0
        gsum[base + QK_NOPE:base + QK_HEAD, 2 * h + 1] = 1.0
    return [attn_norm[l].reshape(1, D_MODEL), w1, g384, jnp.asarray(gsum, bf16),
            jnp.asarray(gsum.T, bf16), v768, v384, v256, v128, inv_freq.reshape(ROPE_HALF, 1),
            wp.astype(bf16), wqb.astype(bf16), wkvb_p.astype(bf16)]


def kernel(x, positions, rel_bias, attn_norm, w_in, swa_q_gain, swa_k_gain, swa_sinks, pool_w,
           pool_scale, mla_q_a_gain, mla_w_qb, mla_kv_a_gain, mla_w_kvb, mla_q_nope_gain,
           mla_q_rope_gain, mla_k_nope_gain, mla_k_rope_gain, w_out, ffn_norm, w_gate, w_up,
           w_down):
    batch, seq, _ = x.shape
    n = batch * seq
    depth = w_in.shape[0]
    bf16 = jnp.bfloat16
    x2 = x.reshape(n, D_MODEL)
    pos2 = positions.astype(jnp.float32).reshape(1, n)
    bias = _bias_table(rel_bias)
    for l in range(depth):
        consts = _layer_consts(l, attn_norm, w_in, swa_q_gain, swa_k_gain, pool_w, pool_scale,
                               mla_q_a_gain, mla_w_qb, mla_kv_a_gain, mla_w_kvb, mla_q_nope_gain,
                               mla_q_rope_gain, mla_k_nope_gain, mla_k_rope_gain)
        qs, ks, vs, pool_o, qm, km, vm = _in_proj(x2, pos2, consts, batch, seq)
        swa_o = _swa(swa_sinks[l], qs, ks, vs, bias, batch, seq)
        mla_o = _mla(qm, km, vm, batch, seq)
        x2 = _out_ffn(x2, swa_o, pool_o, mla_o, w_out[l].astype(bf16),
                      ffn_norm[l].reshape(1, D_MODEL), w_gate[l].astype(bf16),
                      w_up[l].astype(bf16), w_down[l].astype(bf16))
    return x2.reshape(batch, seq, D_MODEL)
```

```python
import functools
import math

import numpy as np
import jax
import jax.numpy as jnp
from jax import lax
from jax.experimental import pallas as pl
from jax.experimental.pallas import tpu as pltpu

D_MODEL = 1024
HEAD_DIM = 64
SWA_HEADS = 6
SWA_KV_HEADS = 2
SWA_GROUP = SWA_HEADS // SWA_KV_HEADS
WINDOW = 128
POOL_WINDOWS = (2, 4, 8, 16)
POOL_GROUP_DIM = 64
POOL_DIM = POOL_GROUP_DIM * len(POOL_WINDOWS)
MLA_HEADS = 6
Q_LORA = 256
KV_LORA = 128
QK_NOPE = 64
QK_ROPE = 32
V_HEAD = 64
QK_HEAD = QK_NOPE + QK_ROPE
ROPE_THETA = 10000.0
N_BUCKETS = 32
MAX_DISTANCE = 128
SWA_Q_DIM = SWA_HEADS * HEAD_DIM
SWA_KV_DIM = SWA_KV_HEADS * HEAD_DIM
MLA_OUT_DIM = MLA_HEADS * V_HEAD
D_FF = 2816
EPS = 1e-6

LANE = 128
MLA_HEAD_PAD = LANE
MLA_PAD_DIM = MLA_HEADS * MLA_HEAD_PAD
ROPE_HALF = QK_ROPE // 2
MLA_V_ROWS = 80
POOL_HALO = max(POOL_WINDOWS)
NEG = -1e30

C_QA = 0
C_KA = C_QA + SWA_Q_DIM
C_VA = C_KA + SWA_KV_DIM
C_UB = C_VA + SWA_KV_DIM
C_CQ = C_UB + POOL_DIM
C_CKV = C_CQ + Q_LORA
C_KR = C_CKV + KV_LORA
IN_PAD_DIM = C_KR + LANE

IN_TILE = 512
SWA_TILE = 512
MLA_TQ = 512
MLA_SLAB = 256
FFN_TILE = 512
FFN_CHUNKS = ((0, 1536), (1536, 1280))
VMEM_LIMIT = 56 * 1024 * 1024


def _dot(a, b):
    return jnp.dot(a, b, preferred_element_type=jnp.float32)


def _dot_nt(a, b):
    return lax.dot_general(a, b, (((1,), (1,)), ((), ())), preferred_element_type=jnp.float32)


def _rms(x, width):
    return x * lax.rsqrt(jnp.sum(x * x, axis=-1, keepdims=True) * (1.0 / width) + EPS)


def _bias_kernel(rel_ref, bucket_ref, out_ref):
    bucket = bucket_ref[...]
    for h in range(SWA_HEADS):
        acc = jnp.full(bucket.shape, NEG, jnp.float32)
        for b in range(N_BUCKETS):
            acc = jnp.where(bucket == b, rel_ref[b, h], acc)
        out_ref[h] = acc


def _band_buckets():
    q_loc = np.arange(WINDOW)[:, None]
    k_loc = np.arange(2 * WINDOW)[None, :]
    dist = q_loc + WINDOW - k_loc
    band_ok = (dist >= 0) & (dist < WINDOW)
    n = np.maximum(dist, 0)
    max_exact = N_BUCKETS // 2
    nf = np.maximum(n, 1).astype(np.float32)
    large = max_exact + (np.log(nf / max_exact) / math.log(MAX_DISTANCE / max_exact)
                         * (N_BUCKETS - max_exact)).astype(np.int32)
    large = np.minimum(large, N_BUCKETS - 1)
    bucket = np.where(n < max_exact, n, large)
    return np.where(band_ok, bucket, -1).astype(np.int32)


def _bias_table(rel_bias):
    return pl.pallas_call(
        _bias_kernel,
        out_shape=jax.ShapeDtypeStruct((SWA_HEADS, WINDOW, 2 * WINDOW), jnp.float32),
        in_specs=[pl.BlockSpec(memory_space=pltpu.SMEM),
                  pl.BlockSpec(memory_space=pltpu.VMEM)],
        out_specs=pl.BlockSpec(memory_space=pltpu.VMEM),
        name="bias_table",
    )(rel_bias, jnp.asarray(_band_buckets()))


def _in_proj_kernel(x_ref, pos_ref, g_attn_ref, w1_ref, g384_ref, gsum_ref, gexp_ref, v768_ref,
                    v384_ref, v256_ref, v128_ref, invf_ref, wp_ref, wqb_ref, wkvb_ref,
                    qs_ref, ks_ref, vs_ref, pool_ref, qm_ref, km_ref, vm_ref, carry_ref):
    i = pl.program_id(1)
    tm = x_ref.shape[0]
    bf16 = jnp.bfloat16

    x = x_ref[...]
    h = (_rms(x, D_MODEL) * g_attn_ref[...]).astype(bf16)
    proj = _dot(h, w1_ref[...])

    qa = proj[:, C_QA:C_KA]
    ss = _dot((qa * qa).astype(bf16), g384_ref[...])
    qs_ref[...] = (qa * lax.rsqrt(ss * (1.0 / HEAD_DIM) + EPS) * v384_ref[...]).astype(bf16)
    ka = proj[:, C_KA:C_VA]
    ss = _dot((ka * ka).astype(bf16), g384_ref[0:SWA_KV_DIM, 0:SWA_KV_DIM])
    ks_ref[...] = (ka * lax.rsqrt(ss * (1.0 / HEAD_DIM) + EPS) * v128_ref[0:1, :]).astype(bf16)
    vs_ref[...] = proj[:, C_VA:C_UB].astype(bf16)

    u = proj[:, C_UB:C_CQ]

    @pl.when(i == 0)
    def _():
        carry_ref[...] = jnp.zeros_like(carry_ref)

    uext = jnp.concatenate([carry_ref[...], u], axis=0)
    a1 = uext + pltpu.roll(uext, 1, 0)
    a2 = a1 + pltpu.roll(a1, 2, 0)
    a3 = a2 + pltpu.roll(a2, 4, 0)
    a4 = a3 + pltpu.roll(a3, 8, 0)
    lane = lax.broadcasted_iota(jnp.int32, (1, POOL_DIM), 1)
    wsum = jnp.where(lane < POOL_GROUP_DIM, a1,
                     jnp.where(lane < 2 * POOL_GROUP_DIM, a2,
                               jnp.where(lane < 3 * POOL_GROUP_DIM, a3, a4)))[POOL_HALO:]
    t = (i * tm + lax.broadcasted_iota(jnp.int32, (tm, 1), 0) + 1).astype(jnp.float32)
    count = jnp.minimum(t, v256_ref[2:3, :])
    d = wsum / count - u
    pool_ref[...] = (_dot(d.astype(bf16), wp_ref[...]) * v256_ref[1:2, :]).astype(bf16)
    carry_ref[...] = u[tm - POOL_HALO:, :]

    ang_t = invf_ref[...] * pos_ref[...]
    cos_t, sin_t = jnp.cos(ang_t), jnp.sin(ang_t)
    one_t = jnp.ones((QK_NOPE, tm), jnp.float32)
    zero_t = jnp.zeros((QK_NOPE, tm), jnp.float32)
    tail = LANE - QK_HEAD
    cos1 = jnp.concatenate([one_t, cos_t, cos_t, one_t[0:tail]], axis=0).T
    sin1 = jnp.concatenate([zero_t, sin_t, sin_t, zero_t[0:tail]], axis=0).T
    m1 = v768_ref[3:4, :]
    m2 = v768_ref[4:5, :]

    def rope(xn, cos_t, sin_t, m1_t, m2_t):
        width = xn.shape[1]
        rot = pltpu.roll(xn, ROPE_HALF, 1) * m1_t + pltpu.roll(xn, width - ROPE_HALF, 1) * m2_t
        return xn * cos_t + rot * sin_t

    def group_sums(v):
        compact = _dot((v * v).astype(bf16), gsum_ref[...])
        return _dot(compact.astype(bf16), gexp_ref[...])

    cq = proj[:, C_CQ:C_CKV]
    cqn = (_rms(cq, Q_LORA) * v256_ref[0:1, :]).astype(bf16)
    q = _dot(cqn, wqb_ref[...])
    qn = q * lax.rsqrt(group_sums(q) * v768_ref[1:2, :] + EPS) * v768_ref[0:1, :]
    cos6 = jnp.concatenate([cos1] * MLA_HEADS, axis=1)
    sin6 = jnp.concatenate([sin1] * MLA_HEADS, axis=1)
    qm_ref[...] = rope(qn, cos6, sin6, m1, m2).astype(bf16)

    kr = proj[:, C_KR:IN_PAD_DIM]
    krn = _rms(kr, QK_ROPE) * v128_ref[2:3, :]
    krr = rope(krn, cos1, sin1, m1[:, 0:LANE], m2[:, 0:LANE])
    ckv = proj[:, C_CKV:C_KR]
    ckvn = (_rms(ckv, KV_LORA) * v128_ref[1:2, :]).astype(bf16)
    kv = _dot(ckvn, wkvb_ref[...])
    kn = kv[:, 0:MLA_PAD_DIM]
    kn = kn * lax.rsqrt(group_sums(kn) * (1.0 / QK_NOPE) + EPS) * v768_ref[2:3, :]
    km_ref[...] = (kn + jnp.concatenate([krr] * MLA_HEADS, axis=1)).astype(bf16)
    vt = kv[:, MLA_PAD_DIM:].T
    fill = (lax.broadcasted_iota(jnp.int32, (MLA_V_ROWS - V_HEAD, tm), 0) == 0).astype(jnp.float32)
    rows = []
    for hd in range(MLA_HEADS):
        rows += [vt[hd * V_HEAD:(hd + 1) * V_HEAD], fill]
    vm_ref[...] = jnp.concatenate(rows, axis=0).astype(bf16)


def _const_spec(shape):
    nd = len(shape)
    return pl.BlockSpec(shape, lambda *_: (0,) * nd)


def _in_proj(x2, pos2, consts, batch, seq):
    n = x2.shape[0]
    tm = min(IN_TILE, seq)
    nt = seq // tm
    row = lambda b, i: (b * nt + i, 0)
    bf16 = jnp.bfloat16
    col = lambda b, i: (0, b * nt + i)
    outs = [(SWA_Q_DIM, bf16), (SWA_KV_DIM, bf16), (SWA_KV_DIM, bf16), (POOL_DIM, bf16),
            (MLA_PAD_DIM, bf16), (MLA_PAD_DIM, bf16)]
    vt_rows = MLA_HEADS * MLA_V_ROWS
    return pl.pallas_call(
        _in_proj_kernel,
        grid=(batch, nt),
        in_specs=[pl.BlockSpec((tm, D_MODEL), row), pl.BlockSpec((1, tm), col)]
                 + [_const_spec(c.shape) for c in consts],
        out_specs=[pl.BlockSpec((tm, w), row) for w, _ in outs]
                  + [pl.BlockSpec((vt_rows, tm), col)],
        out_shape=[jax.ShapeDtypeStruct((n, w), dt) for w, dt in outs]
                  + [jax.ShapeDtypeStruct((vt_rows, n), bf16)],
        scratch_shapes=[pltpu.VMEM((POOL_HALO, POOL_DIM), jnp.float32)],
        compiler_params=pltpu.CompilerParams(
            dimension_semantics=("arbitrary", "arbitrary"), vmem_limit_bytes=VMEM_LIMIT),
        name="in_proj",
    )(x2, pos2, *consts)


def _swa_kernel(sink_ref, q_ref, kc_ref, kp_ref, vc_ref, vp_ref, bias_ref, o_ref):
    i = pl.program_id(1)
    ts = q_ref.shape[0]
    bf16 = jnp.bfloat16
    q = q_ref[...]
    kext = jnp.concatenate([kp_ref[...], kc_ref[...]], axis=0)
    vext = jnp.concatenate([vp_ref[...], vc_ref[...]], axis=0)
    col = lax.broadcasted_iota(jnp.int32, (1, 2 * WINDOW), 1)
    no_prev = jnp.logical_and(i == 0, col < WINDOW)
    for c in range(ts // WINDOW):
        kb = kext[c * WINDOW:(c + 2) * WINDOW]
        vb = vext[c * WINDOW:(c + 2) * WINDOW]
        outs = []
        for h in range(SWA_HEADS):
            g = h // SWA_GROUP
            qh = q[c * WINDOW:(c + 1) * WINDOW, h * HEAD_DIM:(h + 1) * HEAD_DIM]
            s = _dot_nt(qh, kb[:, g * HEAD_DIM:(g + 1) * HEAD_DIM]) + bias_ref[h]
            if c == 0:
                s = jnp.where(no_prev, NEG, s)
            sink = sink_ref[h]
            m = jnp.maximum(jnp.max(s, axis=-1, keepdims=True), sink)
            p = jnp.exp(s - m)
            denom = jnp.sum(p, axis=-1, keepdims=True) + jnp.exp(sink - m)
            o = _dot(p.astype(bf16), vb[:, g * HEAD_DIM:(g + 1) * HEAD_DIM])
            outs.append(o / denom)
        o_ref[c * WINDOW:(c + 1) * WINDOW, :] = jnp.concatenate(outs, axis=1).astype(bf16)


def _swa(sinks, qs, ks, vs, bias, batch, seq):
    n = qs.shape[0]
    ts = min(SWA_TILE, seq)
    nt = seq // ts
    wpt = ts // WINDOW
    wps = seq // WINDOW
    row = lambda b, i: (b * nt + i, 0)
    prev = lambda b, i: (b * wps + jnp.maximum(i * wpt - 1, 0), 0)
    return pl.pallas_call(
        _swa_kernel,
        grid=(batch, nt),
        in_specs=[pl.BlockSpec(memory_space=pltpu.SMEM),
                  pl.BlockSpec((ts, SWA_Q_DIM), row),
                  pl.BlockSpec((ts, SWA_KV_DIM), row), pl.BlockSpec((WINDOW, SWA_KV_DIM), prev),
                  pl.BlockSpec((ts, SWA_KV_DIM), row), pl.BlockSpec((WINDOW, SWA_KV_DIM), prev),
                  _const_spec(bias.shape)],
        out_specs=pl.BlockSpec((ts, SWA_Q_DIM), row),
        out_shape=jax.ShapeDtypeStruct((n, SWA_Q_DIM), jnp.bfloat16),
        compiler_params=pltpu.CompilerParams(
            dimension_semantics=("arbitrary", "arbitrary"), vmem_limit_bytes=VMEM_LIMIT),
        name="swa",
    )(sinks, qs, ks, ks, vs, vs, bias)


def _mla_kernel(q_ref, k_ref, vt_ref, o_ref, m_ref, acc_ref, sa_ref, sb_ref, ma_ref, mb_ref):
    i = pl.program_id(2)
    tq = q_ref.shape[0]
    blk = 2 * tq
    bf16 = jnp.bfloat16
    chains = [(a, c) for a in range(2) for c in range(tq // MLA_SLAB)]
    m_ref[...] = jnp.full(m_ref.shape, NEG, jnp.float32)
    acc_ref[...] = jnp.zeros(acc_ref.shape, jnp.float32)

    def score(a, c, start, nk):
        lanes = slice(a * LANE, (a + 1) * LANE)
        return _dot_nt(k_ref[pl.ds(start, nk), lanes],
                       q_ref[c * MLA_SLAB:(c + 1) * MLA_SLAB, lanes])

    def full_scores(j, s_ref, mx_ref):
        start = pl.multiple_of(j * blk, blk)
        for ch, (a, c) in enumerate(chains):
            s = score(a, c, start, blk)
            s_ref[ch] = s
            mx_ref[ch] = jnp.max(s, axis=0, keepdims=True)

    def tail_scores(start, nfree):
        out = []
        for a, c in chains:
            nk = nfree + (c + 1) * MLA_SLAB
            s = score(a, c, start, nk)
            row = lax.broadcasted_iota(jnp.int32, (nk, MLA_SLAB), 0)
            col = lax.broadcasted_iota(jnp.int32, (nk, MLA_SLAB), 1) + (c * MLA_SLAB + nfree)
            s = jnp.where(row <= col, s, NEG)
            out.append((s, jnp.max(s, axis=0, keepdims=True)))
        return out

    def soft_pv(scores, start):
        for (a, c), (s, mx) in zip(chains, scores):
            cols = slice(c * MLA_SLAB, (c + 1) * MLA_SLAB)
            m_prev = m_ref[a, :, cols]
            m_new = jnp.maximum(m_prev, mx)
            alpha = jnp.exp2(m_prev - m_new)
            p = jnp.exp2(s - m_new).astype(bf16)
            vt = vt_ref[a * MLA_V_ROWS:(a + 1) * MLA_V_ROWS, pl.ds(start, s.shape[0])]
            acc_ref[a, :, cols] = alpha * acc_ref[a, :, cols] + _dot(vt, p)
            m_ref[a, :, cols] = m_new

    def buffered(s_ref, mx_ref):
        return [(s_ref[ch], mx_ref[ch]) for ch in range(len(chains))]

    bufs = ((sa_ref, ma_ref), (sb_ref, mb_ref))

    def pipe_step(j, cur, nxt):
        full_scores(j + 1, *bufs[nxt])
        soft_pv(buffered(*bufs[cur]), pl.multiple_of(j * blk, blk))

    nfull = i // 2
    odd = i % 2

    @pl.when(nfull > 0)
    def _():
        full_scores(0, *bufs[0])

    def pair_body(t, carry):
        pipe_step(2 * t, 0, 1)
        pipe_step(2 * t + 1, 1, 0)
        return carry

    lax.fori_loop(0, jnp.maximum(nfull - 1, 0) // 2, pair_body, 0)

    def finish(has_full, last, nfree):
        if has_full and last == 1:
            pipe_step(nfull - 2, 0, 1)
        tail_start = pl.multiple_of(nfull * blk, blk)
        tail = tail_scores(tail_start, nfree)
        if has_full:
            soft_pv(buffered(*bufs[last]), pl.multiple_of((nfull - 1) * blk, blk))
        soft_pv(tail, tail_start)

    for nfree in (0, tq):
        is_odd = odd == (1 if nfree else 0)
        pl.when(jnp.logical_and(nfull == 0, is_odd))(functools.partial(finish, False, 0, nfree))
        for last in (0, 1):
            cond = jnp.logical_and(jnp.logical_and(nfull > 0, (nfull - 1) % 2 == last), is_odd)
            pl.when(cond)(functools.partial(finish, True, last, nfree))

    outs = []
    for a in range(2):
        acc = acc_ref[a]
        outs.append(acc[0:V_HEAD] / acc[V_HEAD:V_HEAD + 1])
    o_ref[...] = jnp.concatenate(outs, axis=0).T.astype(bf16)


def _mla(qm, km, vm, batch, seq):
    n = qm.shape[0]
    tq = min(MLA_TQ, seq)
    nq = seq // tq
    pairs = MLA_HEADS // 2
    nchain = 2 * (tq // MLA_SLAB)
    return pl.pallas_call(
        _mla_kernel,
        grid=(batch, pairs, nq),
        in_specs=[pl.BlockSpec((tq, 2 * LANE), lambda b, p, i: (b * nq + i, p)),
                  pl.BlockSpec((seq, 2 * LANE), lambda b, p, i: (b, p)),
                  pl.BlockSpec((2 * MLA_V_ROWS, seq), lambda b, p, i: (p, b))],
        out_specs=pl.BlockSpec((tq, LANE), lambda b, p, i: (b * nq + i, p)),
        out_shape=jax.ShapeDtypeStruct((n, MLA_OUT_DIM), jnp.bfloat16),
        scratch_shapes=[pltpu.VMEM((2, 1, tq), jnp.float32),
                        pltpu.VMEM((2, MLA_V_ROWS, tq), jnp.float32)]
                       + [pltpu.VMEM((nchain, 2 * tq, MLA_SLAB), jnp.float32)] * 2
                       + [pltpu.VMEM((nchain, 1, MLA_SLAB), jnp.float32)] * 2,
        compiler_params=pltpu.CompilerParams(
            dimension_semantics=("arbitrary", "arbitrary", "arbitrary"),
            vmem_limit_bytes=VMEM_LIMIT),
        name="mla",
    )(qm, km, vm)


def _out_ffn_kernel(x_ref, a_ref, b_ref, c_ref, wo_ref, g_ref, wg_ref, wu_ref, wd_ref, o_ref):
    bf16 = jnp.bfloat16
    mixed = jnp.concatenate([a_ref[...], b_ref[...], c_ref[...]], axis=1)
    x1 = x_ref[...] + _dot(mixed, wo_ref[...])
    h = (_rms(x1, D_MODEL) * g_ref[...]).astype(bf16)
    acc = x1
    for start, size in FFN_CHUNKS:
        gate = _dot(h, wg_ref[:, start:start + size])
        up = _dot(h, wu_ref[:, start:start + size])
        act = (gate * jax.nn.sigmoid(gate) * up).astype(bf16)
        acc = acc + _dot(act, wd_ref[start:start + size, :])
    o_ref[...] = acc


def _resident(shape):
    nd = len(shape)
    return pl.BlockSpec(shape, lambda *_: (0,) * nd, pipeline_mode=pl.Buffered(1))


def _out_ffn(x2, oa, ob, oc, wo, g, wg, wu, wd):
    n = x2.shape[0]
    tm = min(FFN_TILE, n)
    row = lambda i: (i, 0)
    return pl.pallas_call(
        _out_ffn_kernel,
        grid=(n // tm,),
        in_specs=[pl.BlockSpec((tm, D_MODEL), row), pl.BlockSpec((tm, SWA_Q_DIM), row),
                  pl.BlockSpec((tm, POOL_DIM), row), pl.BlockSpec((tm, MLA_OUT_DIM), row),
                  _resident(wo.shape), _resident(g.shape), _resident(wg.shape),
                  _resident(wu.shape), _resident(wd.shape)],
        out_specs=pl.BlockSpec((tm, D_MODEL), row),
        out_shape=jax.ShapeDtypeStruct((n, D_MODEL), jnp.float32),
        compiler_params=pltpu.CompilerParams(
            dimension_semantics=("arbitrary",), vmem_limit_bytes=VMEM_LIMIT),
        name="out_ffn",
    )(x2, oa, ob, oc, wo, g, wg, wu, wd)


def _block_diag_ones(groups, width):
    g = np.zeros((width, width), np.float32)
    for lo, hi in groups:
        g[lo:hi, lo:hi] = 1.0
    return jnp.asarray(g, jnp.bfloat16)


def _lane_rows(rows, width):
    out = jnp.zeros((8, width), jnp.float32)
    for r, v in enumerate(rows):
        out = out.at[r].set(v.astype(jnp.float32))
    return out


def _head_lanes(nope, rope):
    pad = jnp.zeros((MLA_HEAD_PAD - QK_HEAD,), jnp.float32)
    return jnp.tile(jnp.concatenate([nope.astype(jnp.float32), rope.astype(jnp.float32), pad]),
                    MLA_HEADS)


def _layer_consts(l, attn_norm, w_in, swa_q_gain, swa_k_gain, pool_w, pool_scale, mla_q_a_gain,
                  mla_w_qb, mla_kv_a_gain, mla_w_kvb, mla_q_nope_gain, mla_q_rope_gain,
                  mla_k_nope_gain, mla_k_rope_gain):
    f32, bf16 = jnp.float32, jnp.bfloat16
    zeros = lambda r, c: jnp.zeros((r, c), f32)
    split = C_KR
    w1 = jnp.concatenate([w_in[l][:, :split], zeros(D_MODEL, QK_NOPE), w_in[l][:, split:],
                          zeros(D_MODEL, LANE - QK_HEAD)], axis=1).astype(bf16)
    wqb = mla_w_qb[l].reshape(Q_LORA, MLA_HEADS, QK_HEAD)
    wqb = jnp.pad(wqb, ((0, 0), (0, 0), (0, MLA_HEAD_PAD - QK_HEAD))).reshape(Q_LORA, MLA_PAD_DIM)
    wkvb = mla_w_kvb[l].reshape(KV_LORA, MLA_HEADS, QK_NOPE + V_HEAD)
    wk = jnp.pad(wkvb[:, :, :QK_NOPE], ((0, 0), (0, 0), (0, MLA_HEAD_PAD - QK_NOPE)))
    wv = wkvb[:, :, QK_NOPE:]
    wkvb_p = jnp.concatenate([wk.reshape(KV_LORA, MLA_PAD_DIM), wv.reshape(KV_LORA, MLA_OUT_DIM)],
                             axis=1)
    wp = jnp.zeros((POOL_DIM, POOL_DIM), f32)
    for g in range(len(POOL_WINDOWS)):
        sl = slice(g * POOL_GROUP_DIM, (g + 1) * POOL_GROUP_DIM)
        wp = wp.at[sl, sl].set(pool_w[l, g])
    ones_n, ones_r = jnp.ones((QK_NOPE,), f32), jnp.ones((QK_ROPE,), f32)
    half = jnp.concatenate([jnp.zeros((ROPE_HALF,), f32), jnp.ones((ROPE_HALF,), f32)])
    v768 = _lane_rows([
        _head_lanes(mla_q_nope_gain[l], mla_q_rope_gain[l]) * (QK_HEAD ** -0.5 * math.log2(math.e)),
        _head_lanes(ones_n / QK_NOPE, ones_r / QK_ROPE),
        _head_lanes(mla_k_nope_gain[l], 0.0 * ones_r),
        _head_lanes(0.0 * ones_n, half),
        _head_lanes(0.0 * ones_n, half - 1.0),
    ], MLA_PAD_DIM)
    v384 = (jnp.tile(swa_q_gain[l], SWA_HEADS) * (HEAD_DIM ** -0.5)).reshape(1, SWA_Q_DIM)
    windows = jnp.repeat(jnp.asarray(POOL_WINDOWS, f32), POOL_GROUP_DIM)
    v256 = _lane_rows([mla_q_a_gain[l], pool_scale[l], windows], POOL_DIM)
    inv_freq = ROPE_THETA ** (-jnp.arange(0, QK_ROPE, 2, dtype=f32) / QK_ROPE)
    one_head = lambda rope: _head_lanes(jnp.zeros((QK_NOPE,), f32), rope)[:LANE]
    v128 = _lane_rows([jnp.tile(swa_k_gain[l], SWA_KV_HEADS), mla_kv_a_gain[l],
                       one_head(mla_k_rope_gain[l])], LANE)
    g384 = _block_diag_ones([(h * HEAD_DIM, (h + 1) * HEAD_DIM) for h in range(SWA_HEADS)],
                            SWA_Q_DIM)
    gsum = np.zeros((MLA_PAD_DIM, LANE), np.float32)
    for h in range(MLA_HEADS):
        base = h * MLA_HEAD_PAD
        gsum[base:base + QK_NOPE, 2 * h] = 1.0
        gsum[base + QK_NOPE:base + QK_HEAD, 2 * h + 1] = 1.0
    return [attn_norm[l].reshape(1, D_MODEL), w1, g384, jnp.asarray(gsum, bf16),
            jnp.asarray(gsum.T, bf16), v768, v384, v256, v128, inv_freq.reshape(ROPE_HALF, 1),
            wp.astype(bf16), wqb.astype(bf16), wkvb_p.astype(bf16)]


def kernel(x, positions, rel_bias, attn_norm, w_in, swa_q_gain, swa_k_gain, swa_sinks, pool_w,
           pool_scale, mla_q_a_gain, mla_w_qb, mla_kv_a_gain, mla_w_kvb, mla_q_nope_gain,
           mla_q_rope_gain, mla_k_nope_gain, mla_k_rope_gain, w_out, ffn_norm, w_gate, w_up,
           w_down):
    batch, seq, _ = x.shape
    n = batch * seq
    depth = w_in.shape[0]
    bf16 = jnp.bfloat16
    x2 = x.reshape(n, D_MODEL)
    pos2 = positions.astype(jnp.float32).reshape(1, n)
    bias = _bias_table(rel_bias)
    for l in range(depth):
        consts = _layer_consts(l, attn_norm, w_in, swa_q_gain, swa_k_gain, pool_w, pool_scale,
                               mla_q_a_gain, mla_w_qb, mla_kv_a_gain, mla_w_kvb, mla_q_nope_gain,
                               mla_q_rope_gain, mla_k_nope_gain, mla_k_rope_gain)
        qs, ks, vs, pool_o, qm, km, vm = _in_proj(x2, pos2, consts, batch, seq)
        swa_o = _swa(swa_sinks[l], qs, ks, vs, bias, batch, seq)
        mla_o = _mla(qm, km, vm, batch, seq)
        x2 = _out_ffn(x2, swa_o, pool_o, mla_o, w_out[l].astype(bf16),
                      ffn_norm[l].reshape(1, D_MODEL), w_gate[l].astype(bf16),
                      w_up[l].astype(bf16), w_down[l].astype(bf16))
    return x2.reshape(batch, seq, D_MODEL)
```

```python
import functools
import math

import numpy as np
import jax
import jax.numpy as jnp
from jax import lax
from jax.experimental import pallas as pl
from jax.experimental.pallas import tpu as pltpu

D_MODEL = 1024
HEAD_DIM = 64
SWA_HEADS = 6
SWA_KV_HEADS = 2
SWA_GROUP = SWA_HEADS // SWA_KV_HEADS
WINDOW = 128
POOL_WINDOWS = (2, 4, 8, 16)
POOL_GROUP_DIM = 64
POOL_DIM = POOL_GROUP_DIM * len(POOL_WINDOWS)
MLA_HEADS = 6
Q_LORA = 256
KV_LORA = 128
QK_NOPE = 64
QK_ROPE = 32
V_HEAD = 64
QK_HEAD = QK_NOPE + QK_ROPE
ROPE_THETA = 10000.0
N_BUCKETS = 32
MAX_DISTANCE = 128
SWA_Q_DIM = SWA_HEADS * HEAD_DIM
SWA_KV_DIM = SWA_KV_HEADS * HEAD_DIM
MLA_OUT_DIM = MLA_HEADS * V_HEAD
D_FF = 2816
EPS = 1e-6

LANE = 128
MLA_HEAD_PAD = LANE
MLA_PAD_DIM = MLA_HEADS * MLA_HEAD_PAD
ROPE_HALF = QK_ROPE // 2
MLA_V_ROWS = 80
POOL_HALO = max(POOL_WINDOWS)
NEG = -1e30

C_QA = 0
C_KA = C_QA + SWA_Q_DIM
C_VA = C_KA + SWA_KV_DIM
C_UB = C_VA + SWA_KV_DIM
C_CQ = C_UB + POOL_DIM
C_CKV = C_CQ + Q_LORA
C_KR = C_CKV + KV_LORA
IN_PAD_DIM = C_KR + LANE

IN_TILE = 512
SWA_TILE = 512
MLA_TQ = 1024
MLA_BLK = 1024
MLA_SLAB = 256
FFN_TILE = 512
FFN_CHUNKS = ((0, 1536), (1536, 1280))
VMEM_LIMIT = 56 * 1024 * 1024


def _dot(a, b):
    return jnp.dot(a, b, preferred_element_type=jnp.float32)


def _dot_nt(a, b):
    return lax.dot_general(a, b, (((1,), (1,)), ((), ())), preferred_element_type=jnp.float32)


def _rms(x, width):
    return x * lax.rsqrt(jnp.sum(x * x, axis=-1, keepdims=True) * (1.0 / width) + EPS)


def _bias_kernel(rel_ref, bucket_ref, out_ref):
    bucket = bucket_ref[...]
    for h in range(SWA_HEADS):
        acc = jnp.full(bucket.shape, NEG, jnp.float32)
        for b in range(N_BUCKETS):
            acc = jnp.where(bucket == b, rel_ref[b, h], acc)
        out_ref[h] = acc


def _band_buckets():
    q_loc = np.arange(WINDOW)[:, None]
    k_loc = np.arange(2 * WINDOW)[None, :]
    dist = q_loc + WINDOW - k_loc
    band_ok = (dist >= 0) & (dist < WINDOW)
    n = np.maximum(dist, 0)
    max_exact = N_BUCKETS // 2
    nf = np.maximum(n, 1).astype(np.float32)
    large = max_exact + (np.log(nf / max_exact) / math.log(MAX_DISTANCE / max_exact)
                         * (N_BUCKETS - max_exact)).astype(np.int32)
    large = np.minimum(large, N_BUCKETS - 1)
    bucket = np.where(n < max_exact, n, large)
    return np.where(band_ok, bucket, -1).astype(np.int32)


def _bias_table(rel_bias):
    return pl.pallas_call(
        _bias_kernel,
        out_shape=jax.ShapeDtypeStruct((SWA_HEADS, WINDOW, 2 * WINDOW), jnp.float32),
        in_specs=[pl.BlockSpec(memory_space=pltpu.SMEM),
                  pl.BlockSpec(memory_space=pltpu.VMEM)],
        out_specs=pl.BlockSpec(memory_space=pltpu.VMEM),
        name="bias_table",
    )(rel_bias, jnp.asarray(_band_buckets()))


def _in_proj_kernel(x_ref, pos_ref, g_attn_ref, w1_ref, g384_ref, gsum_ref, gexp_ref, v768_ref,
                    v384_ref, v256_ref, v128_ref, invf_ref, wp_ref, wqb_ref, wkvb_ref,
                    qs_ref, ks_ref, vs_ref, pool_ref, qm_ref, km_ref, vm_ref, carry_ref):
    i = pl.program_id(1)
    tm = x_ref.shape[0]
    bf16 = jnp.bfloat16

    x = x_ref[...]
    h = (_rms(x, D_MODEL) * g_attn_ref[...]).astype(bf16)
    proj = _dot(h, w1_ref[...])

    qa = proj[:, C_QA:C_KA]
    ss = _dot((qa * qa).astype(bf16), g384_ref[...])
    qs_ref[...] = (qa * lax.rsqrt(ss * (1.0 / HEAD_DIM) + EPS) * v384_ref[...]).astype(bf16)
    ka = proj[:, C_KA:C_VA]
    ss = _dot((ka * ka).astype(bf16), g384_ref[0:SWA_KV_DIM, 0:SWA_KV_DIM])
    ks_ref[...] = (ka * lax.rsqrt(ss * (1.0 / HEAD_DIM) + EPS) * v128_ref[0:1, :]).astype(bf16)
    vs_ref[...] = proj[:, C_VA:C_UB].astype(bf16)

    u = proj[:, C_UB:C_CQ]

    @pl.when(i == 0)
    def _():
        carry_ref[...] = jnp.zeros_like(carry_ref)

    uext = jnp.concatenate([carry_ref[...], u], axis=0)
    a1 = uext + pltpu.roll(uext, 1, 0)
    a2 = a1 + pltpu.roll(a1, 2, 0)
    a3 = a2 + pltpu.roll(a2, 4, 0)
    a4 = a3 + pltpu.roll(a3, 8, 0)
    lane = lax.broadcasted_iota(jnp.int32, (1, POOL_DIM), 1)
    wsum = jnp.where(lane < POOL_GROUP_DIM, a1,
                     jnp.where(lane < 2 * POOL_GROUP_DIM, a2,
                               jnp.where(lane < 3 * POOL_GROUP_DIM, a3, a4)))[POOL_HALO:]
    t = (i * tm + lax.broadcasted_iota(jnp.int32, (tm, 1), 0) + 1).astype(jnp.float32)
    count = jnp.minimum(t, v256_ref[2:3, :])
    d = wsum / count - u
    pool_ref[...] = (_dot(d.astype(bf16), wp_ref[...]) * v256_ref[1:2, :]).astype(bf16)
    carry_ref[...] = u[tm - POOL_HALO:, :]

    ang_t = invf_ref[...] * pos_ref[...]
    cos_t, sin_t = jnp.cos(ang_t), jnp.sin(ang_t)
    one_t = jnp.ones((QK_NOPE, tm), jnp.float32)
    zero_t = jnp.zeros((QK_NOPE, tm), jnp.float32)
    tail = LANE - QK_HEAD
    cos1 = jnp.concatenate([one_t, cos_t, cos_t, one_t[0:tail]], axis=0).T
    sin1 = jnp.concatenate([zero_t, sin_t, sin_t, zero_t[0:tail]], axis=0).T
    m1 = v768_ref[3:4, :]
    m2 = v768_ref[4:5, :]

    def rope(xn, cos_t, sin_t, m1_t, m2_t):
        width = xn.shape[1]
        rot = pltpu.roll(xn, ROPE_HALF, 1) * m1_t + pltpu.roll(xn, width - ROPE_HALF, 1) * m2_t
        return xn * cos_t + rot * sin_t

    def group_sums(v):
        compact = _dot((v * v).astype(bf16), gsum_ref[...])
        return _dot(compact.astype(bf16), gexp_ref[...])

    cq = proj[:, C_CQ:C_CKV]
    cqn = (_rms(cq, Q_LORA) * v256_ref[0:1, :]).astype(bf16)
    q = _dot(cqn, wqb_ref[...])
    qn = q * lax.rsqrt(group_sums(q) * v768_ref[1:2, :] + EPS) * v768_ref[0:1, :]
    cos6 = jnp.concatenate([cos1] * MLA_HEADS, axis=1)
    sin6 = jnp.concatenate([sin1] * MLA_HEADS, axis=1)
    qm_ref[...] = rope(qn, cos6, sin6, m1, m2).astype(bf16)

    kr = proj[:, C_KR:IN_PAD_DIM]
    krn = _rms(kr, QK_ROPE) * v128_ref[2:3, :]
    krr = rope(krn, cos1, sin1, m1[:, 0:LANE], m2[:, 0:LANE])
    ckv = proj[:, C_CKV:C_KR]
    ckvn = (_rms(ckv, KV_LORA) * v128_ref[1:2, :]).astype(bf16)
    kv = _dot(ckvn, wkvb_ref[...])
    kn = kv[:, 0:MLA_PAD_DIM]
    kn = kn * lax.rsqrt(group_sums(kn) * (1.0 / QK_NOPE) + EPS) * v768_ref[2:3, :]
    km_ref[...] = (kn + jnp.concatenate([krr] * MLA_HEADS, axis=1)).astype(bf16)
    vt = kv[:, MLA_PAD_DIM:].T
    fill = (lax.broadcasted_iota(jnp.int32, (MLA_V_ROWS - V_HEAD, tm), 0) == 0).astype(jnp.float32)
    rows = []
    for hd in range(MLA_HEADS):
        rows += [vt[hd * V_HEAD:(hd + 1) * V_HEAD], fill]
    vm_ref[...] = jnp.concatenate(rows, axis=0).astype(bf16)


def _const_spec(shape):
    nd = len(shape)
    return pl.BlockSpec(shape, lambda *_: (0,) * nd)


def _in_proj(x2, pos2, consts, batch, seq):
    n = x2.shape[0]
    tm = min(IN_TILE, seq)
    nt = seq // tm
    row = lambda b, i: (b * nt + i, 0)
    bf16 = jnp.bfloat16
    col = lambda b, i: (0, b * nt + i)
    outs = [(SWA_Q_DIM, bf16), (SWA_KV_DIM, bf16), (SWA_KV_DIM, bf16), (POOL_DIM, bf16),
            (MLA_PAD_DIM, bf16), (MLA_PAD_DIM, bf16)]
    vt_rows = MLA_HEADS * MLA_V_ROWS
    return pl.pallas_call(
        _in_proj_kernel,
        grid=(batch, nt),
        in_specs=[pl.BlockSpec((tm, D_MODEL), row), pl.BlockSpec((1, tm), col)]
                 + [_const_spec(c.shape) for c in consts],
        out_specs=[pl.BlockSpec((tm, w), row) for w, _ in outs]
                  + [pl.BlockSpec((vt_rows, tm), col)],
        out_shape=[jax.ShapeDtypeStruct((n, w), dt) for w, dt in outs]
                  + [jax.ShapeDtypeStruct((vt_rows, n), bf16)],
        scratch_shapes=[pltpu.VMEM((POOL_HALO, POOL_DIM), jnp.float32)],
        compiler_params=pltpu.CompilerParams(
            dimension_semantics=("arbitrary", "arbitrary"), vmem_limit_bytes=VMEM_LIMIT),
        name="in_proj",
    )(x2, pos2, *consts)


def _swa_kernel(sink_ref, q_ref, kc_ref, kp_ref, vc_ref, vp_ref, bias_ref, o_ref):
    i = pl.program_id(1)
    ts = q_ref.shape[0]
    bf16 = jnp.bfloat16
    q = q_ref[...]
    kext = jnp.concatenate([kp_ref[...], kc_ref[...]], axis=0)
    vext = jnp.concatenate([vp_ref[...], vc_ref[...]], axis=0)
    col = lax.broadcasted_iota(jnp.int32, (1, 2 * WINDOW), 1)
    no_prev = jnp.logical_and(i == 0, col < WINDOW)
    for c in range(ts // WINDOW):
        kb = kext[c * WINDOW:(c + 2) * WINDOW]
        vb = vext[c * WINDOW:(c + 2) * WINDOW]
        outs = []
        for h in range(SWA_HEADS):
            g = h // SWA_GROUP
            qh = q[c * WINDOW:(c + 1) * WINDOW, h * HEAD_DIM:(h + 1) * HEAD_DIM]
            s = _dot_nt(qh, kb[:, g * HEAD_DIM:(g + 1) * HEAD_DIM]) + bias_ref[h]
            if c == 0:
                s = jnp.where(no_prev, NEG, s)
            sink = sink_ref[h]
            m = jnp.maximum(jnp.max(s, axis=-1, keepdims=True), sink)
            p = jnp.exp(s - m)
            denom = jnp.sum(p, axis=-1, keepdims=True) + jnp.exp(sink - m)
            o = _dot(p.astype(bf16), vb[:, g * HEAD_DIM:(g + 1) * HEAD_DIM])
            outs.append(o / denom)
        o_ref[c * WINDOW:(c + 1) * WINDOW, :] = jnp.concatenate(outs, axis=1).astype(bf16)


def _swa(sinks, qs, ks, vs, bias, batch, seq):
    n = qs.shape[0]
    ts = min(SWA_TILE, seq)
    nt = seq // ts
    wpt = ts // WINDOW
    wps = seq // WINDOW
    row = lambda b, i: (b * nt + i, 0)
    prev = lambda b, i: (b * wps + jnp.maximum(i * wpt - 1, 0), 0)
    return pl.pallas_call(
        _swa_kernel,
        grid=(batch, nt),
        in_specs=[pl.BlockSpec(memory_space=pltpu.SMEM),
                  pl.BlockSpec((ts, SWA_Q_DIM), row),
                  pl.BlockSpec((ts, SWA_KV_DIM), row), pl.BlockSpec((WINDOW, SWA_KV_DIM), prev),
                  pl.BlockSpec((ts, SWA_KV_DIM), row), pl.BlockSpec((WINDOW, SWA_KV_DIM), prev),
                  _const_spec(bias.shape)],
        out_specs=pl.BlockSpec((ts, SWA_Q_DIM), row),
        out_shape=jax.ShapeDtypeStruct((n, SWA_Q_DIM), jnp.bfloat16),
        compiler_params=pltpu.CompilerParams(
            dimension_semantics=("arbitrary", "arbitrary"), vmem_limit_bytes=VMEM_LIMIT),
        name="swa",
    )(sinks, qs, ks, ks, vs, vs, bias)


def _mla_kernel(q_ref, k_ref, vt_ref, o_ref, m_ref, acc_ref, sa_ref, sb_ref, ma_ref, mb_ref):
    i = pl.program_id(2)
    tq = q_ref.shape[0]
    blk = max(MLA_BLK, tq)
    ratio = blk // tq
    bf16 = jnp.bfloat16
    chains = [(a, c) for a in range(2) for c in range(tq // MLA_SLAB)]
    m_ref[...] = jnp.full(m_ref.shape, NEG, jnp.float32)
    acc_ref[...] = jnp.zeros(acc_ref.shape, jnp.float32)

    def score(a, c, start, nk):
        lanes = slice(a * LANE, (a + 1) * LANE)
        return _dot_nt(k_ref[pl.ds(start, nk), lanes],
                       q_ref[c * MLA_SLAB:(c + 1) * MLA_SLAB, lanes])

    def full_score(ch, j, s_ref, mx_ref):
        a, c = chains[ch]
        s = score(a, c, pl.multiple_of(j * blk, blk), blk)
        s_ref[ch] = s
        mx_ref[ch] = jnp.max(s, axis=0, keepdims=True)

    def tail_score(ch, start, nfree):
        a, c = chains[ch]
        nk = nfree + (c + 1) * MLA_SLAB
        s = score(a, c, start, nk)
        row = lax.broadcasted_iota(jnp.int32, (nk, MLA_SLAB), 0)
        col = lax.broadcasted_iota(jnp.int32, (nk, MLA_SLAB), 1) + (c * MLA_SLAB + nfree)
        s = jnp.where(row <= col, s, NEG)
        return s, jnp.max(s, axis=0, keepdims=True)

    def soft_pv(ch, s, mx, start):
        a, c = chains[ch]
        cols = slice(c * MLA_SLAB, (c + 1) * MLA_SLAB)
        m_prev = m_ref[a, :, cols]
        m_new = jnp.maximum(m_prev, mx)
        alpha = jnp.exp2(m_prev - m_new)
        p = jnp.exp2(s - m_new).astype(bf16)
        vt = vt_ref[a * MLA_V_ROWS:(a + 1) * MLA_V_ROWS, pl.ds(start, s.shape[0])]
        acc_ref[a, :, cols] = alpha * acc_ref[a, :, cols] + _dot(vt, p)
        m_ref[a, :, cols] = m_new

    bufs = ((sa_ref, ma_ref), (sb_ref, mb_ref))
    nchain = len(chains)

    def pipe_step(j, cur, nxt):
        s_cur, mx_cur = bufs[cur]
        for ch in range(nchain):
            full_score(ch, j + 1, *bufs[nxt])
            soft_pv(ch, s_cur[ch], mx_cur[ch], pl.multiple_of(j * blk, blk))

    nfull = i // ratio

    @pl.when(nfull > 0)
    def _():
        for ch in range(nchain):
            full_score(ch, 0, *bufs[0])

    def pair_body(t, carry):
        pipe_step(2 * t, 0, 1)
        pipe_step(2 * t + 1, 1, 0)
        return carry

    lax.fori_loop(0, jnp.maximum(nfull - 1, 0) // 2, pair_body, 0)

    def finish(has_full, last, nfree):
        if has_full and last == 1:
            pipe_step(nfull - 2, 0, 1)
        tail_start = pl.multiple_of(nfull * blk, blk)
        tail = []
        for ch in range(nchain):
            tail.append(tail_score(ch, tail_start, nfree))
            if has_full:
                s_last, mx_last = bufs[last]
                soft_pv(ch, s_last[ch], mx_last[ch], pl.multiple_of((nfull - 1) * blk, blk))
        for ch in range(nchain):
            soft_pv(ch, *tail[ch], tail_start)

    for r in range(ratio):
        nfree = r * tq
        is_odd = (i % ratio) == r
        pl.when(jnp.logical_and(nfull == 0, is_odd))(functools.partial(finish, False, 0, nfree))
        for last in (0, 1):
            cond = jnp.logical_and(jnp.logical_and(nfull > 0, (nfull - 1) % 2 == last), is_odd)
            pl.when(cond)(functools.partial(finish, True, last, nfree))

    outs = []
    for a in range(2):
        acc = acc_ref[a]
        outs.append(acc[0:V_HEAD] / acc[V_HEAD:V_HEAD + 1])
    o_ref[...] = jnp.concatenate(outs, axis=0).T.astype(bf16)


def _mla(qm, km, vm, batch, seq):
    n = qm.shape[0]
    tq = min(MLA_TQ, seq)
    nq = seq // tq
    pairs = MLA_HEADS // 2
    nchain = 2 * (tq // MLA_SLAB)
    return pl.pallas_call(
        _mla_kernel,
        grid=(batch, pairs, nq),
        in_specs=[pl.BlockSpec((tq, 2 * LANE), lambda b, p, i: (b * nq + i, p)),
                  pl.BlockSpec((seq, 2 * LANE), lambda b, p, i: (b, p)),
                  pl.BlockSpec((2 * MLA_V_ROWS, seq), lambda b, p, i: (p, b))],
        out_specs=pl.BlockSpec((tq, LANE), lambda b, p, i: (b * nq + i, p)),
        out_shape=jax.ShapeDtypeStruct((n, MLA_OUT_DIM), jnp.bfloat16),
        scratch_shapes=[pltpu.VMEM((2, 1, tq), jnp.float32),
                        pltpu.VMEM((2, MLA_V_ROWS, tq), jnp.float32)]
                       + [pltpu.VMEM((nchain, max(MLA_BLK, tq), MLA_SLAB), jnp.float32)] * 2
                       + [pltpu.VMEM((nchain, 1, MLA_SLAB), jnp.float32)] * 2,
        compiler_params=pltpu.CompilerParams(
            dimension_semantics=("arbitrary", "arbitrary", "arbitrary"),
            vmem_limit_bytes=VMEM_LIMIT),
        name="mla",
    )(qm, km, vm)


def _out_ffn_kernel(x_ref, a_ref, b_ref, c_ref, wo_ref, g_ref, wg_ref, wu_ref, wd_ref, o_ref):
    bf16 = jnp.bfloat16
    mixed = jnp.concatenate([a_ref[...], b_ref[...], c_ref[...]], axis=1)
    x1 = x_ref[...] + _dot(mixed, wo_ref[...])
    h = (_rms(x1, D_MODEL) * g_ref[...]).astype(bf16)
    acc = x1
    for start, size in FFN_CHUNKS:
        gate = _dot(h, wg_ref[:, start:start + size])
        up = _dot(h, wu_ref[:, start:start + size])
        act = (gate * jax.nn.sigmoid(gate) * up).astype(bf16)
        acc = acc + _dot(act, wd_ref[start:start + size, :])
    o_ref[...] = acc


def _resident(shape):
    nd = len(shape)
    return pl.BlockSpec(shape, lambda *_: (0,) * nd, pipeline_mode=pl.Buffered(1))


def _out_ffn(x2, oa, ob, oc, wo, g, wg, wu, wd):
    n = x2.shape[0]
    tm = min(FFN_TILE, n)
    row = lambda i: (i, 0)
    return pl.pallas_call(
        _out_ffn_kernel,
        grid=(n // tm,),
        in_specs=[pl.BlockSpec((tm, D_MODEL), row), pl.BlockSpec((tm, SWA_Q_DIM), row),
                  pl.BlockSpec((tm, POOL_DIM), row), pl.BlockSpec((tm, MLA_OUT_DIM), row),
                  _resident(wo.shape), _resident(g.shape), _resident(wg.shape),
                  _resident(wu.shape), _resident(wd.shape)],
        out_specs=pl.BlockSpec((tm, D_MODEL), row),
        out_shape=jax.ShapeDtypeStruct((n, D_MODEL), jnp.float32),
        compiler_params=pltpu.CompilerParams(
            dimension_semantics=("arbitrary",), vmem_limit_bytes=VMEM_LIMIT),
        name="out_ffn",
    )(x2, oa, ob, oc, wo, g, wg, wu, wd)


def _block_diag_ones(groups, width):
    g = np.zeros((width, width), np.float32)
    for lo, hi in groups:
        g[lo:hi, lo:hi] = 1.0
    return jnp.asarray(g, jnp.bfloat16)


def _lane_rows(rows, width):
    out = jnp.zeros((8, width), jnp.float32)
    for r, v in enumerate(rows):
        out = out.at[r].set(v.astype(jnp.float32))
    return out


def _head_lanes(nope, rope):
    pad = jnp.zeros((MLA_HEAD_PAD - QK_HEAD,), jnp.float32)
    return jnp.tile(jnp.concatenate([nope.astype(jnp.float32), rope.astype(jnp.float32), pad]),
                    MLA_HEADS)


def _layer_consts(l, attn_norm, w_in, swa_q_gain, swa_k_gain, pool_w, pool_scale, mla_q_a_gain,
                  mla_w_qb, mla_kv_a_gain, mla_w_kvb, mla_q_nope_gain, mla_q_rope_gain,
                  mla_k_nope_gain, mla_k_rope_gain):
    f32, bf16 = jnp.float32, jnp.bfloat16
    zeros = lambda r, c: jnp.zeros((r, c), f32)
    split = C_KR
    w1 = jnp.concatenate([w_in[l][:, :split], zeros(D_MODEL, QK_NOPE), w_in[l][:, split:],
                          zeros(D_MODEL, LANE - QK_HEAD)], axis=1).astype(bf16)
    wqb = mla_w_qb[l].reshape(Q_LORA, MLA_HEADS, QK_HEAD)
    wqb = jnp.pad(wqb, ((0, 0), (0, 0), (0, MLA_HEAD_PAD - QK_HEAD))).reshape(Q_LORA, MLA_PAD_DIM)
    wkvb = mla_w_kvb[l].reshape(KV_LORA, MLA_HEADS, QK_NOPE + V_HEAD)
    wk = jnp.pad(wkvb[:, :, :QK_NOPE], ((0, 0), (0, 0), (0, MLA_HEAD_PAD - QK_NOPE)))
    wv = wkvb[:, :, QK_NOPE:]
    wkvb_p = jnp.concatenate([wk.reshape(KV_LORA, MLA_PAD_DIM), wv.reshape(KV_LORA, MLA_OUT_DIM)],
                             axis=1)
    wp = jnp.zeros((POOL_DIM, POOL_DIM), f32)
    for g in range(len(POOL_WINDOWS)):
        sl = slice(g * POOL_GROUP_DIM, (g + 1) * POOL_GROUP_DIM)
        wp = wp.at[sl, sl].set(pool_w[l, g])
    ones_n, ones_r = jnp.ones((QK_NOPE,), f32), jnp.ones((QK_ROPE,), f32)
    half = jnp.concatenate([jnp.zeros((ROPE_HALF,), f32), jnp.ones((ROPE_HALF,), f32)])
    v768 = _lane_rows([
        _head_lanes(mla_q_nope_gain[l], mla_q_rope_gain[l]) * (QK_HEAD ** -0.5 * math.log2(math.e)),
        _head_lanes(ones_n / QK_NOPE, ones_r / QK_ROPE),
        _head_lanes(mla_k_nope_gain[l], 0.0 * ones_r),
        _head_lanes(0.0 * ones_n, half),
        _head_lanes(0.0 * ones_n, half - 1.0),
    ], MLA_PAD_DIM)
    v384 = (jnp.tile(swa_q_gain[l], SWA_HEADS) * (HEAD_DIM ** -0.5)).reshape(1, SWA_Q_DIM)
    windows = jnp.repeat(jnp.asarray(POOL_WINDOWS, f32), POOL_GROUP_DIM)
    v256 = _lane_rows([mla_q_a_gain[l], pool_scale[l], windows], POOL_DIM)
    inv_freq = ROPE_THETA ** (-jnp.arange(0, QK_ROPE, 2, dtype=f32) / QK_ROPE)
    one_head = lambda rope: _head_lanes(jnp.zeros((QK_NOPE,), f32), rope)[:LANE]
    v128 = _lane_rows([jnp.tile(swa_k_gain[l], SWA_KV_HEADS), mla_kv_a_gain[l],
                       one_head(mla_k_rope_gain[l])], LANE)
    g384 = _block_diag_ones([(h * HEAD_DIM, (h + 1) * HEAD_DIM) for h in range(SWA_HEADS)],
                            SWA_Q_DIM)
    gsum = np.zeros((MLA_PAD_DIM, LANE), np.float32)
    for h in range(MLA_HEADS):
        base = h * MLA_HEAD_PAD
        gsum[base:base + QK_NOPE, 2 * h] = 1.0
        gsum[base + QK_NOPE:base + QK_HEAD, 2 * h + 1] = 1.0
    return [attn_norm[l].reshape(1, D_MODEL), w1, g384, jnp.asarray(gsum, bf16),
            jnp.asarray(gsum.T, bf16), v768, v384, v256, v128, inv_freq.reshape(ROPE_HALF, 1),
            wp.astype(bf16), wqb.astype(bf16), wkvb_p.astype(bf16)]


def kernel(x, positions, rel_bias, attn_norm, w_in, swa_q_gain, swa_k_gain, swa_sinks, pool_w,
           pool_scale, mla_q_a_gain, mla_w_qb, mla_kv_a_gain, mla_w_kvb, mla_q_nope_gain,
           mla_q_rope_gain, mla_k_nope_gain, mla_k_rope_gain, w_out, ffn_norm, w_gate, w_up,
           w_down):
    batch, seq, _ = x.shape
    n = batch * seq
    depth = w_in.shape[0]
    bf16 = jnp.bfloat16
    x2 = x.reshape(n, D_MODEL)
    pos2 = positions.astype(jnp.float32).reshape(1, n)
    bias = _bias_table(rel_bias)
    for l in range(depth):
        consts = _layer_consts(l, attn_norm, w_in, swa_q_gain, swa_k_gain, pool_w, pool_scale,
                               mla_q_a_gain, mla_w_qb, mla_kv_a_gain, mla_w_kvb, mla_q_nope_gain,
                               mla_q_rope_gain, mla_k_nope_gain, mla_k_rope_gain)
        qs, ks, vs, pool_o, qm, km, vm = _in_proj(x2, pos2, consts, batch, seq)
        swa_o = _swa(swa_sinks[l], qs, ks, vs, bias, batch, seq)
        mla_o = _mla(qm, km, vm, batch, seq)
        x2 = _out_ffn(x2, swa_o, pool_o, mla_o, w_out[l].astype(bf16),
                      ffn_norm[l].reshape(1, D_MODEL), w_gate[l].astype(bf16),
                      w_up[l].astype(bf16), w_down[l].astype(bf16))
    return x2.reshape(batch, seq, D_MODEL)
```

```python
import functools
import math

import numpy as np
import jax
import jax.numpy as jnp
from jax import lax
from jax.experimental import pallas as pl
from jax.experimental.pallas import tpu as pltpu

D_MODEL = 1024
HEAD_DIM = 64
SWA_HEADS = 6
SWA_KV_HEADS = 2
SWA_GROUP = SWA_HEADS // SWA_KV_HEADS
WINDOW = 128
POOL_WINDOWS = (2, 4, 8, 16)
POOL_GROUP_DIM = 64
POOL_DIM = POOL_GROUP_DIM * len(POOL_WINDOWS)
MLA_HEADS = 6
Q_LORA = 256
KV_LORA = 128
QK_NOPE = 64
QK_ROPE = 32
V_HEAD = 64
QK_HEAD = QK_NOPE + QK_ROPE
ROPE_THETA = 10000.0
N_BUCKETS = 32
MAX_DISTANCE = 128
SWA_Q_DIM = SWA_HEADS * HEAD_DIM
SWA_KV_DIM = SWA_KV_HEADS * HEAD_DIM
MLA_OUT_DIM = MLA_HEADS * V_HEAD
D_FF = 2816
EPS = 1e-6

LANE = 128
MLA_HEAD_PAD = LANE
MLA_PAD_DIM = MLA_HEADS * MLA_HEAD_PAD
ROPE_HALF = QK_ROPE // 2
MLA_V_ROWS = 80
POOL_HALO = max(POOL_WINDOWS)
NEG = -1e30
LOG2E = math.log2(math.e)

C_QA = 0
C_KA = C_QA + SWA_Q_DIM
C_VA = C_KA + SWA_KV_DIM
C_UB = C_VA + SWA_KV_DIM
C_CQ = C_UB + POOL_DIM
C_CKV = C_CQ + Q_LORA
C_KR = C_CKV + KV_LORA
IN_PAD_DIM = C_KR + LANE

IN_TILE = 512
IN_SPLIT = 2
SWA_TILE = 512
MLA_TQ = 1024
MLA_BLK = 1024
MLA_SLAB = 256
FFN_TILE = 512
FFN_CHUNKS = ((0, 1536), (1536, 1280))
VMEM_LIMIT = 56 * 1024 * 1024


def _dot(a, b):
    return jnp.dot(a, b, preferred_element_type=jnp.float32)


def _dot_nt(a, b):
    return lax.dot_general(a, b, (((1,), (1,)), ((), ())), preferred_element_type=jnp.float32)


def _rms(x, width):
    return x * lax.rsqrt(jnp.sum(x * x, axis=-1, keepdims=True) * (1.0 / width) + EPS)


def _bias_kernel(rel_ref, bucket_ref, out_ref):
    bucket = bucket_ref[...]
    for h in range(SWA_HEADS):
        acc = jnp.full(bucket.shape, NEG, jnp.float32)
        for b in range(N_BUCKETS):
            acc = jnp.where(bucket == b, rel_ref[b, h] * LOG2E, acc)
        out_ref[h] = acc


def _band_buckets():
    q_loc = np.arange(WINDOW)[:, None]
    k_loc = np.arange(2 * WINDOW)[None, :]
    dist = q_loc + WINDOW - k_loc
    band_ok = (dist >= 0) & (dist < WINDOW)
    n = np.maximum(dist, 0)
    max_exact = N_BUCKETS // 2
    nf = np.maximum(n, 1).astype(np.float32)
    large = max_exact + (np.log(nf / max_exact) / math.log(MAX_DISTANCE / max_exact)
                         * (N_BUCKETS - max_exact)).astype(np.int32)
    large = np.minimum(large, N_BUCKETS - 1)
    bucket = np.where(n < max_exact, n, large)
    return np.where(band_ok, bucket, -1).astype(np.int32)


def _bias_table(rel_bias):
    return pl.pallas_call(
        _bias_kernel,
        out_shape=jax.ShapeDtypeStruct((SWA_HEADS, WINDOW, 2 * WINDOW), jnp.float32),
        in_specs=[pl.BlockSpec(memory_space=pltpu.SMEM),
                  pl.BlockSpec(memory_space=pltpu.VMEM)],
        out_specs=pl.BlockSpec(memory_space=pltpu.VMEM),
        name="bias_table",
    )(rel_bias, jnp.asarray(_band_buckets()))


def _in_proj_kernel(x_ref, pos_ref, g_attn_ref, w1_ref, g384_ref, gsum_ref, gexp_ref, v768_ref,
                    v384_ref, v256_ref, v128_ref, invf_ref, wp_ref, wqb_ref, wkvb_ref,
                    qs_ref, ks_ref, vs_ref, pool_ref, qm_ref, km_ref, vm_ref, carry_ref):
    i = pl.program_id(1)
    tm = x_ref.shape[0]
    bf16 = jnp.bfloat16

    x = x_ref[...]
    h = (_rms(x, D_MODEL) * g_attn_ref[...]).astype(bf16)
    bounds = [(r * tm // IN_SPLIT, (r + 1) * tm // IN_SPLIT) for r in range(IN_SPLIT)]
    projs = [_dot(h[r0:r1], w1_ref[...]) for r0, r1 in bounds]

    m1 = v768_ref[3:4, :]
    m2 = v768_ref[4:5, :]

    def rope(xn, cos_t, sin_t, m1_t, m2_t):
        width = xn.shape[1]
        rot = pltpu.roll(xn, ROPE_HALF, 1) * m1_t + pltpu.roll(xn, width - ROPE_HALF, 1) * m2_t
        return xn * cos_t + rot * sin_t

    def group_sums(v):
        compact = _dot((v * v).astype(bf16), gsum_ref[...])
        return _dot(compact.astype(bf16), gexp_ref[...])

    def prepare(proj, r0, r1):
        rows = r1 - r0
        qa = proj[:, C_QA:C_KA]
        ss = _dot((qa * qa).astype(bf16), g384_ref[...])
        qs_ref[r0:r1, :] = (qa * lax.rsqrt(ss * (1.0 / HEAD_DIM) + EPS) * v384_ref[...]).astype(bf16)
        ka = proj[:, C_KA:C_VA]
        ss = _dot((ka * ka).astype(bf16), g384_ref[0:SWA_KV_DIM, 0:SWA_KV_DIM])
        ks_ref[r0:r1, :] = (ka * lax.rsqrt(ss * (1.0 / HEAD_DIM) + EPS)
                            * v128_ref[0:1, :]).astype(bf16)
        vs_ref[r0:r1, :] = proj[:, C_VA:C_UB].astype(bf16)

        ang_t = invf_ref[...] * pos_ref[:, r0:r1]
        cos_t, sin_t = jnp.cos(ang_t), jnp.sin(ang_t)
        one_t = jnp.ones((QK_NOPE, rows), jnp.float32)
        zero_t = jnp.zeros((QK_NOPE, rows), jnp.float32)
        tail = LANE - QK_HEAD
        cos1 = jnp.concatenate([one_t, cos_t, cos_t, one_t[0:tail]], axis=0).T
        sin1 = jnp.concatenate([zero_t, sin_t, sin_t, zero_t[0:tail]], axis=0).T

        cq = proj[:, C_CQ:C_CKV]
        cqn = (_rms(cq, Q_LORA) * v256_ref[0:1, :]).astype(bf16)
        q = _dot(cqn, wqb_ref[...])
        qn = q * lax.rsqrt(group_sums(q) * v768_ref[1:2, :] + EPS) * v768_ref[0:1, :]
        cos6 = jnp.concatenate([cos1] * MLA_HEADS, axis=1)
        sin6 = jnp.concatenate([sin1] * MLA_HEADS, axis=1)
        qm_ref[r0:r1, :] = rope(qn, cos6, sin6, m1, m2).astype(bf16)

        kr = proj[:, C_KR:IN_PAD_DIM]
        krn = _rms(kr, QK_ROPE) * v128_ref[2:3, :]
        krr = rope(krn, cos1, sin1, m1[:, 0:LANE], m2[:, 0:LANE])
        ckv = proj[:, C_CKV:C_KR]
        ckvn = (_rms(ckv, KV_LORA) * v128_ref[1:2, :]).astype(bf16)
        kv = _dot(ckvn, wkvb_ref[...])
        kn = kv[:, 0:MLA_PAD_DIM]
        kn = kn * lax.rsqrt(group_sums(kn) * (1.0 / QK_NOPE) + EPS) * v768_ref[2:3, :]
        km_ref[r0:r1, :] = (kn + jnp.concatenate([krr] * MLA_HEADS, axis=1)).astype(bf16)
        vt = kv[:, MLA_PAD_DIM:].T
        fill = (lax.broadcasted_iota(jnp.int32, (MLA_V_ROWS - V_HEAD, rows), 0) == 0
                ).astype(jnp.float32)
        pieces = []
        for hd in range(MLA_HEADS):
            pieces += [vt[hd * V_HEAD:(hd + 1) * V_HEAD], fill]
        vm_ref[:, r0:r1] = jnp.concatenate(pieces, axis=0).astype(bf16)

    for (r0, r1), proj in zip(bounds, projs):
        prepare(proj, r0, r1)

    u = jnp.concatenate([proj[:, C_UB:C_CQ] for proj in projs], axis=0)

    @pl.when(i == 0)
    def _():
        carry_ref[...] = jnp.zeros_like(carry_ref)

    uext = jnp.concatenate([carry_ref[...], u], axis=0)
    a1 = uext + pltpu.roll(uext, 1, 0)
    a2 = a1 + pltpu.roll(a1, 2, 0)
    a3 = a2 + pltpu.roll(a2, 4, 0)
    a4 = a3 + pltpu.roll(a3, 8, 0)
    lane = lax.broadcasted_iota(jnp.int32, (1, POOL_DIM), 1)
    wsum = jnp.where(lane < POOL_GROUP_DIM, a1,
                     jnp.where(lane < 2 * POOL_GROUP_DIM, a2,
                               jnp.where(lane < 3 * POOL_GROUP_DIM, a3, a4)))[POOL_HALO:]
    t = (i * tm + lax.broadcasted_iota(jnp.int32, (tm, 1), 0) + 1).astype(jnp.float32)
    count = jnp.minimum(t, v256_ref[2:3, :])
    d = wsum / count - u
    pool_ref[...] = (_dot(d.astype(bf16), wp_ref[...]) * v256_ref[1:2, :]).astype(bf16)
    carry_ref[...] = u[tm - POOL_HALO:, :]


def _const_spec(shape):
    nd = len(shape)
    return pl.BlockSpec(shape, lambda *_: (0,) * nd)


def _in_proj(x2, pos2, consts, batch, seq):
    n = x2.shape[0]
    tm = min(IN_TILE, seq)
    nt = seq // tm
    row = lambda b, i: (b * nt + i, 0)
    bf16 = jnp.bfloat16
    col = lambda b, i: (0, b * nt + i)
    outs = [(SWA_Q_DIM, bf16), (SWA_KV_DIM, bf16), (SWA_KV_DIM, bf16), (POOL_DIM, bf16),
            (MLA_PAD_DIM, bf16), (MLA_PAD_DIM, bf16)]
    vt_rows = MLA_HEADS * MLA_V_ROWS
    return pl.pallas_call(
        _in_proj_kernel,
        grid=(batch, nt),
        in_specs=[pl.BlockSpec((tm, D_MODEL), row), pl.BlockSpec((1, tm), col)]
                 + [_const_spec(c.shape) for c in consts],
        out_specs=[pl.BlockSpec((tm, w), row) for w, _ in outs]
                  + [pl.BlockSpec((vt_rows, tm), col)],
        out_shape=[jax.ShapeDtypeStruct((n, w), dt) for w, dt in outs]
                  + [jax.ShapeDtypeStruct((vt_rows, n), bf16)],
        scratch_shapes=[pltpu.VMEM((POOL_HALO, POOL_DIM), jnp.float32)],
        compiler_params=pltpu.CompilerParams(
            dimension_semantics=("arbitrary", "arbitrary"), vmem_limit_bytes=VMEM_LIMIT),
        name="in_proj",
    )(x2, pos2, *consts)


def _swa_kernel(sink_ref, q_ref, kc_ref, kp_ref, vc_ref, vp_ref, bias_ref, o_ref):
    i = pl.program_id(1)
    ts = q_ref.shape[0]
    nwin = ts // WINDOW
    bf16 = jnp.bfloat16
    q = q_ref[...]
    kext = jnp.concatenate([kp_ref[...], kc_ref[...]], axis=0)
    vext = jnp.concatenate([vp_ref[...], vc_ref[...]], axis=0)
    col = lax.broadcasted_iota(jnp.int32, (1, 2 * WINDOW), 1)
    no_prev = jnp.logical_and(i == 0, col < WINDOW)
    ones = jnp.ones((2 * WINDOW, LANE), bf16)
    heads = range(SWA_HEADS)
    sinks = [sink_ref[h] * LOG2E for h in heads]

    def scores(c, h):
        g = h // SWA_GROUP
        qh = q[c * WINDOW:(c + 1) * WINDOW, h * HEAD_DIM:(h + 1) * HEAD_DIM]
        kb = kext[c * WINDOW:(c + 2) * WINDOW, g * HEAD_DIM:(g + 1) * HEAD_DIM]
        s = _dot_nt(qh, kb) + bias_ref[h]
        if c == 0:
            s = jnp.where(no_prev, NEG, s)
        return s

    pending = [scores(0, h) for h in heads]
    for c in range(nwin):
        s_all = pending
        ms = [jnp.maximum(jnp.max(s_all[h], axis=-1, keepdims=True), sinks[h]) for h in heads]
        pending = [scores(c + 1, h) for h in heads] if c + 1 < nwin else []
        ps = [jnp.exp2(s_all[h] - ms[h]).astype(bf16) for h in heads]
        vaug = jnp.concatenate([vext[c * WINDOW:(c + 2) * WINDOW], ones], axis=1)
        os_ = [_dot(ps[h], vaug) for h in heads]
        outs = []
        for h in heads:
            lo = (h // SWA_GROUP) * HEAD_DIM
            denom = os_[h][:, SWA_KV_DIM + lo:SWA_KV_DIM + lo + HEAD_DIM] + jnp.exp2(sinks[h] - ms[h])
            outs.append(os_[h][:, lo:lo + HEAD_DIM] / denom)
        o_ref[c * WINDOW:(c + 1) * WINDOW, :] = jnp.concatenate(outs, axis=1).astype(bf16)


def _swa(sinks, qs, ks, vs, bias, batch, seq):
    n = qs.shape[0]
    ts = min(SWA_TILE, seq)
    nt = seq // ts
    wpt = ts // WINDOW
    wps = seq // WINDOW
    row = lambda b, i: (b * nt + i, 0)
    prev = lambda b, i: (b * wps + jnp.maximum(i * wpt - 1, 0), 0)
    return pl.pallas_call(
        _swa_kernel,
        grid=(batch, nt),
        in_specs=[pl.BlockSpec(memory_space=pltpu.SMEM),
                  pl.BlockSpec((ts, SWA_Q_DIM), row),
                  pl.BlockSpec((ts, SWA_KV_DIM), row), pl.BlockSpec((WINDOW, SWA_KV_DIM), prev),
                  pl.BlockSpec((ts, SWA_KV_DIM), row), pl.BlockSpec((WINDOW, SWA_KV_DIM), prev),
                  _const_spec(bias.shape)],
        out_specs=pl.BlockSpec((ts, SWA_Q_DIM), row),
        out_shape=jax.ShapeDtypeStruct((n, SWA_Q_DIM), jnp.bfloat16),
        compiler_params=pltpu.CompilerParams(
            dimension_semantics=("arbitrary", "arbitrary"), vmem_limit_bytes=VMEM_LIMIT),
        name="swa",
    )(sinks, qs, ks, ks, vs, vs, bias)


def _mla_kernel(q_ref, k_ref, vt_ref, o_ref, m_ref, acc_ref, sa_ref, sb_ref, ma_ref, mb_ref):
    i = pl.program_id(2)
    tq = q_ref.shape[0]
    blk = max(MLA_BLK, tq)
    ratio = blk // tq
    bf16 = jnp.bfloat16
    chains = [(a, c) for a in range(2) for c in range(tq // MLA_SLAB)]
    m_ref[...] = jnp.full(m_ref.shape, NEG, jnp.float32)
    acc_ref[...] = jnp.zeros(acc_ref.shape, jnp.float32)

    def score(a, c, start, nk):
        lanes = slice(a * LANE, (a + 1) * LANE)
        return _dot_nt(k_ref[pl.ds(start, nk), lanes],
                       q_ref[c * MLA_SLAB:(c + 1) * MLA_SLAB, lanes])

    def full_score(ch, j, s_ref, mx_ref):
        a, c = chains[ch]
        s = score(a, c, pl.multiple_of(j * blk, blk), blk)
        s_ref[ch] = s
        mx_ref[ch] = jnp.max(s, axis=0, keepdims=True)

    def tail_score(ch, start, nfree):
        a, c = chains[ch]
        nk = nfree + (c + 1) * MLA_SLAB
        s = score(a, c, start, nk)
        row = lax.broadcasted_iota(jnp.int32, (nk, MLA_SLAB), 0)
        col = lax.broadcasted_iota(jnp.int32, (nk, MLA_SLAB), 1) + (c * MLA_SLAB + nfree)
        s = jnp.where(row <= col, s, NEG)
        return s, jnp.max(s, axis=0, keepdims=True)

    def soft_pv(ch, s, mx, start):
        a, c = chains[ch]
        cols = slice(c * MLA_SLAB, (c + 1) * MLA_SLAB)
        m_prev = m_ref[a, :, cols]
        m_new = jnp.maximum(m_prev, mx)
        alpha = jnp.exp2(m_prev - m_new)
        p = jnp.exp2(s - m_new).astype(bf16)
        vt = vt_ref[a * MLA_V_ROWS:(a + 1) * MLA_V_ROWS, pl.ds(start, s.shape[0])]
        acc_ref[a, :, cols] = alpha * acc_ref[a, :, cols] + _dot(vt, p)
        m_ref[a, :, cols] = m_new

    bufs = ((sa_ref, ma_ref), (sb_ref, mb_ref))
    nchain = len(chains)

    def pipe_step(j, cur, nxt):
        s_cur, mx_cur = bufs[cur]
        for ch in range(nchain):
            full_score(ch, j + 1, *bufs[nxt])
            soft_pv(ch, s_cur[ch], mx_cur[ch], pl.multiple_of(j * blk, blk))

    nfull = i // ratio

    @pl.when(nfull > 0)
    def _():
        for ch in range(nchain):
            full_score(ch, 0, *bufs[0])

    def pair_body(t, carry):
        pipe_step(2 * t, 0, 1)
        pipe_step(2 * t + 1, 1, 0)
        return carry

    lax.fori_loop(0, jnp.maximum(nfull - 1, 0) // 2, pair_body, 0)

    def finish(has_full, last, nfree):
        if has_full and last == 1:
            pipe_step(nfull - 2, 0, 1)
        tail_start = pl.multiple_of(nfull * blk, blk)
        tail = []
        for ch in range(nchain):
            tail.append(tail_score(ch, tail_start, nfree))
            if has_full:
                s_last, mx_last = bufs[last]
                soft_pv(ch, s_last[ch], mx_last[ch], pl.multiple_of((nfull - 1) * blk, blk))
        for ch in range(nchain):
            soft_pv(ch, *tail[ch], tail_start)

    for r in range(ratio):
        nfree = r * tq
        is_odd = (i % ratio) == r
        pl.when(jnp.logical_and(nfull == 0, is_odd))(functools.partial(finish, False, 0, nfree))
        for last in (0, 1):
            cond = jnp.logical_and(jnp.logical_and(nfull > 0, (nfull - 1) % 2 == last), is_odd)
            pl.when(cond)(functools.partial(finish, True, last, nfree))

    outs = []
    for a in range(2):
        acc = acc_ref[a]
        outs.append(acc[0:V_HEAD] / acc[V_HEAD:V_HEAD + 1])
    o_ref[...] = jnp.concatenate(outs, axis=0).T.astype(bf16)


def _mla(qm, km, vm, batch, seq):
    n = qm.shape[0]
    tq = min(MLA_TQ, seq)
    nq = seq // tq
    pairs = MLA_HEADS // 2
    nchain = 2 * (tq // MLA_SLAB)
    return pl.pallas_call(
        _mla_kernel,
        grid=(batch, pairs, nq),
        in_specs=[pl.BlockSpec((tq, 2 * LANE), lambda b, p, i: (b * nq + i, p)),
                  pl.BlockSpec((seq, 2 * LANE), lambda b, p, i: (b, p)),
                  pl.BlockSpec((2 * MLA_V_ROWS, seq), lambda b, p, i: (p, b))],
        out_specs=pl.BlockSpec((tq, LANE), lambda b, p, i: (b * nq + i, p)),
        out_shape=jax.ShapeDtypeStruct((n, MLA_OUT_DIM), jnp.bfloat16),
        scratch_shapes=[pltpu.VMEM((2, 1, tq), jnp.float32),
                        pltpu.VMEM((2, MLA_V_ROWS, tq), jnp.float32)]
                       + [pltpu.VMEM((nchain, max(MLA_BLK, tq), MLA_SLAB), jnp.float32)] * 2
                       + [pltpu.VMEM((nchain, 1, MLA_SLAB), jnp.float32)] * 2,
        compiler_params=pltpu.CompilerParams(
            dimension_semantics=("arbitrary", "arbitrary", "arbitrary"),
            vmem_limit_bytes=VMEM_LIMIT),
        name="mla",
    )(qm, km, vm)


def _out_ffn_kernel(x_ref, a_ref, b_ref, c_ref, wo_ref, g_ref, wg_ref, wu_ref, wd_ref, o_ref):
    bf16 = jnp.bfloat16
    mixed = jnp.concatenate([a_ref[...], b_ref[...], c_ref[...]], axis=1)
    x1 = x_ref[...] + _dot(mixed, wo_ref[...])
    h = (_rms(x1, D_MODEL) * g_ref[...]).astype(bf16)
    acc = x1
    for start, size in FFN_CHUNKS:
        gate = _dot(h, wg_ref[:, start:start + size])
        up = _dot(h, wu_ref[:, start:start + size])
        act = (gate * jax.nn.sigmoid(gate) * up).astype(bf16)
        acc = acc + _dot(act, wd_ref[start:start + size, :])
    o_ref[...] = acc


def _resident(shape):
    nd = len(shape)
    return pl.BlockSpec(shape, lambda *_: (0,) * nd, pipeline_mode=pl.Buffered(1))


def _out_ffn(x2, oa, ob, oc, wo, g, wg, wu, wd):
    n = x2.shape[0]
    tm = min(FFN_TILE, n)
    row = lambda i: (i, 0)
    return pl.pallas_call(
        _out_ffn_kernel,
        grid=(n // tm,),
        in_specs=[pl.BlockSpec((tm, D_MODEL), row), pl.BlockSpec((tm, SWA_Q_DIM), row),
                  pl.BlockSpec((tm, POOL_DIM), row), pl.BlockSpec((tm, MLA_OUT_DIM), row),
                  _resident(wo.shape), _resident(g.shape), _resident(wg.shape),
                  _resident(wu.shape), _resident(wd.shape)],
        out_specs=pl.BlockSpec((tm, D_MODEL), row),
        out_shape=jax.ShapeDtypeStruct((n, D_MODEL), jnp.float32),
        compiler_params=pltpu.CompilerParams(
            dimension_semantics=("arbitrary",), vmem_limit_bytes=VMEM_LIMIT),
        name="out_ffn",
    )(x2, oa, ob, oc, wo, g, wg, wu, wd)


def _block_diag_ones(groups, width):
    g = np.zeros((width, width), np.float32)
    for lo, hi in groups:
        g[lo:hi, lo:hi] = 1.0
    return jnp.asarray(g, jnp.bfloat16)


def _lane_rows(rows, width):
    out = jnp.zeros((8, width), jnp.float32)
    for r, v in enumerate(rows):
        out = out.at[r].set(v.astype(jnp.float32))
    return out


def _head_lanes(nope, rope):
    pad = jnp.zeros((MLA_HEAD_PAD - QK_HEAD,), jnp.float32)
    return jnp.tile(jnp.concatenate([nope.astype(jnp.float32), rope.astype(jnp.float32), pad]),
                    MLA_HEADS)


def _layer_consts(l, attn_norm, w_in, swa_q_gain, swa_k_gain, pool_w, pool_scale, mla_q_a_gain,
                  mla_w_qb, mla_kv_a_gain, mla_w_kvb, mla_q_nope_gain, mla_q_rope_gain,
                  mla_k_nope_gain, mla_k_rope_gain):
    f32, bf16 = jnp.float32, jnp.bfloat16
    zeros = lambda r, c: jnp.zeros((r, c), f32)
    split = C_KR
    w1 = jnp.concatenate([w_in[l][:, :split], zeros(D_MODEL, QK_NOPE), w_in[l][:, split:],
                          zeros(D_MODEL, LANE - QK_HEAD)], axis=1).astype(bf16)
    wqb = mla_w_qb[l].reshape(Q_LORA, MLA_HEADS, QK_HEAD)
    wqb = jnp.pad(wqb, ((0, 0), (0, 0), (0, MLA_HEAD_PAD - QK_HEAD))).reshape(Q_LORA, MLA_PAD_DIM)
    wkvb = mla_w_kvb[l].reshape(KV_LORA, MLA_HEADS, QK_NOPE + V_HEAD)
    wk = jnp.pad(wkvb[:, :, :QK_NOPE], ((0, 0), (0, 0), (0, MLA_HEAD_PAD - QK_NOPE)))
    wv = wkvb[:, :, QK_NOPE:]
    wkvb_p = jnp.concatenate([wk.reshape(KV_LORA, MLA_PAD_DIM), wv.reshape(KV_LORA, MLA_OUT_DIM)],
                             axis=1)
    wp = jnp.zeros((POOL_DIM, POOL_DIM), f32)
    for g in range(len(POOL_WINDOWS)):
        sl = slice(g * POOL_GROUP_DIM, (g + 1) * POOL_GROUP_DIM)
        wp = wp.at[sl, sl].set(pool_w[l, g])
    ones_n, ones_r = jnp.ones((QK_NOPE,), f32), jnp.ones((QK_ROPE,), f32)
    half = jnp.concatenate([jnp.zeros((ROPE_HALF,), f32), jnp.ones((ROPE_HALF,), f32)])
    v768 = _lane_rows([
        _head_lanes(mla_q_nope_gain[l], mla_q_rope_gain[l]) * (QK_HEAD ** -0.5 * LOG2E),
        _head_lanes(ones_n / QK_NOPE, ones_r / QK_ROPE),
        _head_lanes(mla_k_nope_gain[l], 0.0 * ones_r),
        _head_lanes(0.0 * ones_n, half),
        _head_lanes(0.0 * ones_n, half - 1.0),
    ], MLA_PAD_DIM)
    v384 = (jnp.tile(swa_q_gain[l], SWA_HEADS) * (HEAD_DIM ** -0.5 * LOG2E)).reshape(1, SWA_Q_DIM)
    windows = jnp.repeat(jnp.asarray(POOL_WINDOWS, f32), POOL_GROUP_DIM)
    v256 = _lane_rows([mla_q_a_gain[l], pool_scale[l], windows], POOL_DIM)
    inv_freq = ROPE_THETA ** (-jnp.arange(0, QK_ROPE, 2, dtype=f32) / QK_ROPE)
    one_head = lambda rope: _head_lanes(jnp.zeros((QK_NOPE,), f32), rope)[:LANE]
    v128 = _lane_rows([jnp.tile(swa_k_gain[l], SWA_KV_HEADS), mla_kv_a_gain[l],
                       one_head(mla_k_rope_gain[l])], LANE)
    g384 = _block_diag_ones([(h * HEAD_DIM, (h + 1) * HEAD_DIM) for h in range(SWA_HEADS)],
                            SWA_Q_DIM)
    gsum = np.zeros((MLA_PAD_DIM, LANE), np.float32)
    for h in range(MLA_HEADS):
        base = h * MLA_HEAD_PAD
        gsum[base:base + QK_NOPE, 2 * h] = 1.0
        gsum[base + QK_NOPE:base + QK_HEAD, 2 * h + 1] = 1.0
    return [attn_norm[l].reshape(1, D_MODEL), w1, g384, jnp.asarray(gsum, bf16),
            jnp.asarray(gsum.T, bf16), v768, v384, v256, v128, inv_freq.reshape(ROPE_HALF, 1),
            wp.astype(bf16), wqb.astype(bf16), wkvb_p.astype(bf16)]


def kernel(x, positions, rel_bias, attn_norm, w_in, swa_q_gain, swa_k_gain, swa_sinks, pool_w,
           pool_scale, mla_q_a_gain, mla_w_qb, mla_kv_a_gain, mla_w_kvb, mla_q_nope_gain,
           mla_q_rope_gain, mla_k_nope_gain, mla_k_rope_gain, w_out, ffn_norm, w_gate, w_up,
           w_down):
    batch, seq, _ = x.shape
    n = batch * seq
    depth = w_in.shape[0]
    bf16 = jnp.bfloat16
    x2 = x.reshape(n, D_MODEL)
    pos2 = positions.astype(jnp.float32).reshape(1, n)
    bias = _bias_table(rel_bias)
    for l in range(depth):
        consts = _layer_consts(l, attn_norm, w_in, swa_q_gain, swa_k_gain, pool_w, pool_scale,
                               mla_q_a_gain, mla_w_qb, mla_kv_a_gain, mla_w_kvb, mla_q_nope_gain,
                               mla_q_rope_gain, mla_k_nope_gain, mla_k_rope_gain)
        qs, ks, vs, pool_o, qm, km, vm = _in_proj(x2, pos2, consts, batch, seq)
        swa_o = _swa(swa_sinks[l], qs, ks, vs, bias, batch, seq)
        mla_o = _mla(qm, km, vm, batch, seq)
        x2 = _out_ffn(x2, swa_o, pool_o, mla_o, w_out[l].astype(bf16),
                      ffn_norm[l].reshape(1, D_MODEL), w_gate[l].astype(bf16),
                      w_up[l].astype(bf16), w_down[l].astype(bf16))
    return x2.reshape(batch, seq, D_MODEL)
```

```python
import functools
import math

import numpy as np
import jax
import jax.numpy as jnp
from jax import lax
from jax.experimental import pallas as pl
from jax.experimental.pallas import tpu as pltpu

D_MODEL = 1024
HEAD_DIM = 64
SWA_HEADS = 6
SWA_KV_HEADS = 2
SWA_GROUP = SWA_HEADS // SWA_KV_HEADS
WINDOW = 128
POOL_WINDOWS = (2, 4, 8, 16)
POOL_GROUP_DIM = 64
POOL_DIM = POOL_GROUP_DIM * len(POOL_WINDOWS)
MLA_HEADS = 6
Q_LORA = 256
KV_LORA = 128
QK_NOPE = 64
QK_ROPE = 32
V_HEAD = 64
QK_HEAD = QK_NOPE + QK_ROPE
ROPE_THETA = 10000.0
N_BUCKETS = 32
MAX_DISTANCE = 128
SWA_Q_DIM = SWA_HEADS * HEAD_DIM
SWA_KV_DIM = SWA_KV_HEADS * HEAD_DIM
MLA_OUT_DIM = MLA_HEADS * V_HEAD
D_FF = 2816
EPS = 1e-6

LANE = 128
MLA_HEAD_PAD = LANE
MLA_PAD_DIM = MLA_HEADS * MLA_HEAD_PAD
ROPE_HALF = QK_ROPE // 2
MLA_V_ROWS = 80
POOL_HALO = max(POOL_WINDOWS)
NEG = -1e30
LOG2E = math.log2(math.e)

C_QA = 0
C_KA = C_QA + SWA_Q_DIM
C_VA = C_KA + SWA_KV_DIM
C_UB = C_VA + SWA_KV_DIM
C_CQ = C_UB + POOL_DIM
C_CKV = C_CQ + Q_LORA
C_KR = C_CKV + KV_LORA
IN_PAD_DIM = C_KR + LANE

IN_TILE = 1024
IN_SPLIT = 2
SWA_TILE = 1024
MLA_TQ = 1024
MLA_BLK = 1024
MLA_SLAB = 256
FFN_TILE = 1024
FFN_CHUNKS = ((0, 1536), (1536, 1280))
VMEM_LIMIT = 56 * 1024 * 1024


def _dot(a, b):
    return jnp.dot(a, b, preferred_element_type=jnp.float32)


def _dot_nt(a, b):
    return lax.dot_general(a, b, (((1,), (1,)), ((), ())), preferred_element_type=jnp.float32)


def _rms(x, width):
    return x * lax.rsqrt(jnp.sum(x * x, axis=-1, keepdims=True) * (1.0 / width) + EPS)


def _bias_kernel(rel_ref, bucket_ref, out_ref):
    bucket = bucket_ref[...]
    for h in range(SWA_HEADS):
        acc = jnp.full(bucket.shape, NEG, jnp.float32)
        for b in range(N_BUCKETS):
            acc = jnp.where(bucket == b, rel_ref[b, h] * LOG2E, acc)
        out_ref[h] = acc


def _band_buckets():
    q_loc = np.arange(WINDOW)[:, None]
    k_loc = np.arange(2 * WINDOW)[None, :]
    dist = q_loc + WINDOW - k_loc
    band_ok = (dist >= 0) & (dist < WINDOW)
    n = np.maximum(dist, 0)
    max_exact = N_BUCKETS // 2
    nf = np.maximum(n, 1).astype(np.float32)
    large = max_exact + (np.log(nf / max_exact) / math.log(MAX_DISTANCE / max_exact)
                         * (N_BUCKETS - max_exact)).astype(np.int32)
    large = np.minimum(large, N_BUCKETS - 1)
    bucket = np.where(n < max_exact, n, large)
    return np.where(band_ok, bucket, -1).astype(np.int32)


def _bias_table(rel_bias):
    return pl.pallas_call(
        _bias_kernel,
        out_shape=jax.ShapeDtypeStruct((SWA_HEADS, WINDOW, 2 * WINDOW), jnp.float32),
        in_specs=[pl.BlockSpec(memory_space=pltpu.SMEM),
                  pl.BlockSpec(memory_space=pltpu.VMEM)],
        out_specs=pl.BlockSpec(memory_space=pltpu.VMEM),
        name="bias_table",
    )(rel_bias, jnp.asarray(_band_buckets()))


def _in_proj_kernel(x_ref, pos_ref, g_attn_ref, w1_ref, g384_ref, gsum_ref, gexp_ref, v768_ref,
                    v384_ref, v256_ref, v128_ref, invf_ref, wp_ref, wqb_ref, wkvb_ref,
                    qs_ref, ks_ref, vs_ref, pool_ref, qm_ref, km_ref, vm_ref, carry_ref):
    i = pl.program_id(1)
    tm = x_ref.shape[0]
    bf16 = jnp.bfloat16

    x = x_ref[...]
    h = (_rms(x, D_MODEL) * g_attn_ref[...]).astype(bf16)
    bounds = [(r * tm // IN_SPLIT, (r + 1) * tm // IN_SPLIT) for r in range(IN_SPLIT)]
    projs = [_dot(h[r0:r1], w1_ref[...]) for r0, r1 in bounds]

    m1 = v768_ref[3:4, :]
    m2 = v768_ref[4:5, :]

    def rope(xn, cos_t, sin_t, m1_t, m2_t):
        width = xn.shape[1]
        rot = pltpu.roll(xn, ROPE_HALF, 1) * m1_t + pltpu.roll(xn, width - ROPE_HALF, 1) * m2_t
        return xn * cos_t + rot * sin_t

    def group_sums(v):
        compact = _dot((v * v).astype(bf16), gsum_ref[...])
        return _dot(compact.astype(bf16), gexp_ref[...])

    def prepare(proj, r0, r1):
        rows = r1 - r0
        qa = proj[:, C_QA:C_KA]
        ss = _dot((qa * qa).astype(bf16), g384_ref[...])
        qs_ref[r0:r1, :] = (qa * lax.rsqrt(ss * (1.0 / HEAD_DIM) + EPS) * v384_ref[...]).astype(bf16)
        ka = proj[:, C_KA:C_VA]
        ss = _dot((ka * ka).astype(bf16), g384_ref[0:SWA_KV_DIM, 0:SWA_KV_DIM])
        ks_ref[r0:r1, :] = (ka * lax.rsqrt(ss * (1.0 / HEAD_DIM) + EPS)
                            * v128_ref[0:1, :]).astype(bf16)
        vs_ref[r0:r1, :] = proj[:, C_VA:C_UB].astype(bf16)

        ang_t = invf_ref[...] * pos_ref[:, r0:r1]
        cos_t, sin_t = jnp.cos(ang_t), jnp.sin(ang_t)
        one_t = jnp.ones((QK_NOPE, rows), jnp.float32)
        zero_t = jnp.zeros((QK_NOPE, rows), jnp.float32)
        tail = LANE - QK_HEAD
        cos1 = jnp.concatenate([one_t, cos_t, cos_t, one_t[0:tail]], axis=0).T
        sin1 = jnp.concatenate([zero_t, sin_t, sin_t, zero_t[0:tail]], axis=0).T

        cq = proj[:, C_CQ:C_CKV]
        cqn = (_rms(cq, Q_LORA) * v256_ref[0:1, :]).astype(bf16)
        q = _dot(cqn, wqb_ref[...])
        qn = q * lax.rsqrt(group_sums(q) * v768_ref[1:2, :] + EPS) * v768_ref[0:1, :]
        cos6 = jnp.concatenate([cos1] * MLA_HEADS, axis=1)
        sin6 = jnp.concatenate([sin1] * MLA_HEADS, axis=1)
        qm_ref[r0:r1, :] = rope(qn, cos6, sin6, m1, m2).astype(bf16)

        kr = proj[:, C_KR:IN_PAD_DIM]
        krn = _rms(kr, QK_ROPE) * v128_ref[2:3, :]
        krr = rope(krn, cos1, sin1, m1[:, 0:LANE], m2[:, 0:LANE])
        ckv = proj[:, C_CKV:C_KR]
        ckvn = (_rms(ckv, KV_LORA) * v128_ref[1:2, :]).astype(bf16)
        kv = _dot(ckvn, wkvb_ref[...])
        kn = kv[:, 0:MLA_PAD_DIM]
        kn = kn * lax.rsqrt(group_sums(kn) * (1.0 / QK_NOPE) + EPS) * v768_ref[2:3, :]
        km_ref[r0:r1, :] = (kn + jnp.concatenate([krr] * MLA_HEADS, axis=1)).astype(bf16)
        vt = kv[:, MLA_PAD_DIM:].T
        fill = (lax.broadcasted_iota(jnp.int32, (MLA_V_ROWS - V_HEAD, rows), 0) == 0
                ).astype(jnp.float32)
        pieces = []
        for hd in range(MLA_HEADS):
            pieces += [vt[hd * V_HEAD:(hd + 1) * V_HEAD], fill]
        vm_ref[:, r0:r1] = jnp.concatenate(pieces, axis=0).astype(bf16)

    for (r0, r1), proj in zip(bounds, projs):
        prepare(proj, r0, r1)

    u = jnp.concatenate([proj[:, C_UB:C_CQ] for proj in projs], axis=0)

    @pl.when(i == 0)
    def _():
        carry_ref[...] = jnp.zeros_like(carry_ref)

    uext = jnp.concatenate([carry_ref[...], u], axis=0)
    a1 = uext + pltpu.roll(uext, 1, 0)
    a2 = a1 + pltpu.roll(a1, 2, 0)
    a3 = a2 + pltpu.roll(a2, 4, 0)
    a4 = a3 + pltpu.roll(a3, 8, 0)
    lane = lax.broadcasted_iota(jnp.int32, (1, POOL_DIM), 1)
    wsum = jnp.where(lane < POOL_GROUP_DIM, a1,
                     jnp.where(lane < 2 * POOL_GROUP_DIM, a2,
                               jnp.where(lane < 3 * POOL_GROUP_DIM, a3, a4)))[POOL_HALO:]
    t = (i * tm + lax.broadcasted_iota(jnp.int32, (tm, 1), 0) + 1).astype(jnp.float32)
    count = jnp.minimum(t, v256_ref[2:3, :])
    d = wsum / count - u
    pool_ref[...] = (_dot(d.astype(bf16), wp_ref[...]) * v256_ref[1:2, :]).astype(bf16)
    carry_ref[...] = u[tm - POOL_HALO:, :]


def _const_spec(shape):
    nd = len(shape)
    return pl.BlockSpec(shape, lambda *_: (0,) * nd)


def _in_proj(x2, pos2, consts, batch, seq):
    n = x2.shape[0]
    tm = min(IN_TILE, seq)
    nt = seq // tm
    row = lambda b, i: (b * nt + i, 0)
    bf16 = jnp.bfloat16
    col = lambda b, i: (0, b * nt + i)
    outs = [(SWA_Q_DIM, bf16), (SWA_KV_DIM, bf16), (SWA_KV_DIM, bf16), (POOL_DIM, bf16),
            (MLA_PAD_DIM, bf16), (MLA_PAD_DIM, bf16)]
    vt_rows = MLA_HEADS * MLA_V_ROWS
    return pl.pallas_call(
        _in_proj_kernel,
        grid=(batch, nt),
        in_specs=[pl.BlockSpec((tm, D_MODEL), row), pl.BlockSpec((1, tm), col)]
                 + [_const_spec(c.shape) for c in consts],
        out_specs=[pl.BlockSpec((tm, w), row) for w, _ in outs]
                  + [pl.BlockSpec((vt_rows, tm), col)],
        out_shape=[jax.ShapeDtypeStruct((n, w), dt) for w, dt in outs]
                  + [jax.ShapeDtypeStruct((vt_rows, n), bf16)],
        scratch_shapes=[pltpu.VMEM((POOL_HALO, POOL_DIM), jnp.float32)],
        compiler_params=pltpu.CompilerParams(
            dimension_semantics=("arbitrary", "arbitrary"), vmem_limit_bytes=VMEM_LIMIT),
        name="in_proj",
    )(x2, pos2, *consts)


def _swa_kernel(sink_ref, q_ref, kc_ref, kp_ref, vc_ref, vp_ref, bias_ref, o_ref):
    i = pl.program_id(1)
    ts = q_ref.shape[0]
    nwin = ts // WINDOW
    bf16 = jnp.bfloat16
    q = q_ref[...]
    kext = jnp.concatenate([kp_ref[...], kc_ref[...]], axis=0)
    vext = jnp.concatenate([vp_ref[...], vc_ref[...]], axis=0)
    col = lax.broadcasted_iota(jnp.int32, (1, 2 * WINDOW), 1)
    no_prev = jnp.logical_and(i == 0, col < WINDOW)
    ones = jnp.ones((2 * WINDOW, LANE), bf16)
    heads = range(SWA_HEADS)
    sinks = [sink_ref[h] * LOG2E for h in heads]

    def scores(c, h):
        g = h // SWA_GROUP
        qh = q[c * WINDOW:(c + 1) * WINDOW, h * HEAD_DIM:(h + 1) * HEAD_DIM]
        kb = kext[c * WINDOW:(c + 2) * WINDOW, g * HEAD_DIM:(g + 1) * HEAD_DIM]
        s = _dot_nt(qh, kb) + bias_ref[h]
        if c == 0:
            s = jnp.where(no_prev, NEG, s)
        return s

    pending = [scores(0, h) for h in heads]
    for c in range(nwin):
        s_all = pending
        ms = [jnp.maximum(jnp.max(s_all[h], axis=-1, keepdims=True), sinks[h]) for h in heads]
        pending = [scores(c + 1, h) for h in heads] if c + 1 < nwin else []
        ps = [jnp.exp2(s_all[h] - ms[h]).astype(bf16) for h in heads]
        vaug = jnp.concatenate([vext[c * WINDOW:(c + 2) * WINDOW], ones], axis=1)
        os_ = [_dot(ps[h], vaug) for h in heads]
        outs = []
        for h in heads:
            lo = (h // SWA_GROUP) * HEAD_DIM
            denom = os_[h][:, SWA_KV_DIM + lo:SWA_KV_DIM + lo + HEAD_DIM] + jnp.exp2(sinks[h] - ms[h])
            outs.append(os_[h][:, lo:lo + HEAD_DIM] / denom)
        o_ref[c * WINDOW:(c + 1) * WINDOW, :] = jnp.concatenate(outs, axis=1).astype(bf16)


def _swa(sinks, qs, ks, vs, bias, batch, seq):
    n = qs.shape[0]
    ts = min(SWA_TILE, seq)
    nt = seq // ts
    wpt = ts // WINDOW
    wps = seq // WINDOW
    row = lambda b, i: (b * nt + i, 0)
    prev = lambda b, i: (b * wps + jnp.maximum(i * wpt - 1, 0), 0)
    return pl.pallas_call(
        _swa_kernel,
        grid=(batch, nt),
        in_specs=[pl.BlockSpec(memory_space=pltpu.SMEM),
                  pl.BlockSpec((ts, SWA_Q_DIM), row),
                  pl.BlockSpec((ts, SWA_KV_DIM), row), pl.BlockSpec((WINDOW, SWA_KV_DIM), prev),
                  pl.BlockSpec((ts, SWA_KV_DIM), row), pl.BlockSpec((WINDOW, SWA_KV_DIM), prev),
                  _const_spec(bias.shape)],
        out_specs=pl.BlockSpec((ts, SWA_Q_DIM), row),
        out_shape=jax.ShapeDtypeStruct((n, SWA_Q_DIM), jnp.bfloat16),
        compiler_params=pltpu.CompilerParams(
            dimension_semantics=("arbitrary", "arbitrary"), vmem_limit_bytes=VMEM_LIMIT),
        name="swa",
    )(sinks, qs, ks, ks, vs, vs, bias)


def _mla_kernel(q_ref, k_ref, vt_ref, o_ref, m_ref, acc_ref, sa_ref, sb_ref, ma_ref, mb_ref):
    i = pl.program_id(2)
    tq = q_ref.shape[0]
    blk = max(MLA_BLK, tq)
    ratio = blk // tq
    bf16 = jnp.bfloat16
    chains = [(a, c) for a in range(2) for c in range(tq // MLA_SLAB)]
    m_ref[...] = jnp.full(m_ref.shape, NEG, jnp.float32)
    acc_ref[...] = jnp.zeros(acc_ref.shape, jnp.float32)

    def score(a, c, start, nk):
        lanes = slice(a * LANE, (a + 1) * LANE)
        return _dot_nt(k_ref[pl.ds(start, nk), lanes],
                       q_ref[c * MLA_SLAB:(c + 1) * MLA_SLAB, lanes])

    def full_score(ch, j, s_ref, mx_ref):
        a, c = chains[ch]
        s = score(a, c, pl.multiple_of(j * blk, blk), blk)
        s_ref[ch] = s
        mx_ref[ch] = jnp.max(s, axis=0, keepdims=True)

    def tail_score(ch, start, nfree):
        a, c = chains[ch]
        nk = nfree + (c + 1) * MLA_SLAB
        s = score(a, c, start, nk)
        row = lax.broadcasted_iota(jnp.int32, (nk, MLA_SLAB), 0)
        col = lax.broadcasted_iota(jnp.int32, (nk, MLA_SLAB), 1) + (c * MLA_SLAB + nfree)
        s = jnp.where(row <= col, s, NEG)
        return s, jnp.max(s, axis=0, keepdims=True)

    def soft_pv(ch, s, mx, start):
        a, c = chains[ch]
        cols = slice(c * MLA_SLAB, (c + 1) * MLA_SLAB)
        m_prev = m_ref[a, :, cols]
        m_new = jnp.maximum(m_prev, mx)
        alpha = jnp.exp2(m_prev - m_new)
        p = jnp.exp2(s - m_new).astype(bf16)
        vt = vt_ref[a * MLA_V_ROWS:(a + 1) * MLA_V_ROWS, pl.ds(start, s.shape[0])]
        acc_ref[a, :, cols] = alpha * acc_ref[a, :, cols] + _dot(vt, p)
        m_ref[a, :, cols] = m_new

    bufs = ((sa_ref, ma_ref), (sb_ref, mb_ref))
    nchain = len(chains)

    def pipe_step(j, cur, nxt):
        s_cur, mx_cur = bufs[cur]
        full_score(0, j + 1, *bufs[nxt])
        for ch in range(nchain):
            if ch + 1 < nchain:
                full_score(ch + 1, j + 1, *bufs[nxt])
            soft_pv(ch, s_cur[ch], mx_cur[ch], pl.multiple_of(j * blk, blk))

    nfull = i // ratio

    @pl.when(nfull > 0)
    def _():
        for ch in range(nchain):
            full_score(ch, 0, *bufs[0])

    def pair_body(t, carry):
        pipe_step(2 * t, 0, 1)
        pipe_step(2 * t + 1, 1, 0)
        return carry

    lax.fori_loop(0, jnp.maximum(nfull - 1, 0) // 2, pair_body, 0)

    def finish(has_full, last, nfree):
        if has_full and last == 1:
            pipe_step(nfull - 2, 0, 1)
        tail_start = pl.multiple_of(nfull * blk, blk)
        tail = [tail_score(0, tail_start, nfree)]
        for ch in range(nchain):
            if ch + 1 < nchain:
                tail.append(tail_score(ch + 1, tail_start, nfree))
            if has_full:
                s_last, mx_last = bufs[last]
                soft_pv(ch, s_last[ch], mx_last[ch], pl.multiple_of((nfull - 1) * blk, blk))
        for ch in range(nchain):
            soft_pv(ch, *tail[ch], tail_start)

    for r in range(ratio):
        nfree = r * tq
        is_odd = (i % ratio) == r
        pl.when(jnp.logical_and(nfull == 0, is_odd))(functools.partial(finish, False, 0, nfree))
        for last in (0, 1):
            cond = jnp.logical_and(jnp.logical_and(nfull > 0, (nfull - 1) % 2 == last), is_odd)
            pl.when(cond)(functools.partial(finish, True, last, nfree))

    outs = []
    for a in range(2):
        acc = acc_ref[a]
        outs.append(acc[0:V_HEAD] / acc[V_HEAD:V_HEAD + 1])
    o_ref[...] = jnp.concatenate(outs, axis=0).T.astype(bf16)


def _mla(qm, km, vm, batch, seq):
    n = qm.shape[0]
    tq = min(MLA_TQ, seq)
    nq = seq // tq
    pairs = MLA_HEADS // 2
    nchain = 2 * (tq // MLA_SLAB)
    return pl.pallas_call(
        _mla_kernel,
        grid=(batch, pairs, nq),
        in_specs=[pl.BlockSpec((tq, 2 * LANE), lambda b, p, i: (b * nq + i, p)),
                  pl.BlockSpec((seq, 2 * LANE), lambda b, p, i: (b, p)),
                  pl.BlockSpec((2 * MLA_V_ROWS, seq), lambda b, p, i: (p, b))],
        out_specs=pl.BlockSpec((tq, LANE), lambda b, p, i: (b * nq + i, p)),
        out_shape=jax.ShapeDtypeStruct((n, MLA_OUT_DIM), jnp.bfloat16),
        scratch_shapes=[pltpu.VMEM((2, 1, tq), jnp.float32),
                        pltpu.VMEM((2, MLA_V_ROWS, tq), jnp.float32)]
                       + [pltpu.VMEM((nchain, max(MLA_BLK, tq), MLA_SLAB), jnp.float32)] * 2
                       + [pltpu.VMEM((nchain, 1, MLA_SLAB), jnp.float32)] * 2,
        compiler_params=pltpu.CompilerParams(
            dimension_semantics=("arbitrary", "arbitrary", "arbitrary"),
            vmem_limit_bytes=VMEM_LIMIT),
        name="mla",
    )(qm, km, vm)


def _out_ffn_kernel(x_ref, a_ref, b_ref, c_ref, wo_ref, g_ref, wg_ref, wu_ref, wd_ref, o_ref):
    bf16 = jnp.bfloat16
    mixed = jnp.concatenate([a_ref[...], b_ref[...], c_ref[...]], axis=1)
    x1 = x_ref[...] + _dot(mixed, wo_ref[...])
    h = (_rms(x1, D_MODEL) * g_ref[...]).astype(bf16)
    acc = x1
    for start, size in FFN_CHUNKS:
        gate = _dot(h, wg_ref[:, start:start + size])
        up = _dot(h, wu_ref[:, start:start + size])
        act = (gate * jax.nn.sigmoid(gate) * up).astype(bf16)
        acc = acc + _dot(act, wd_ref[start:start + size, :])
    o_ref[...] = acc


def _resident(shape):
    nd = len(shape)
    return pl.BlockSpec(shape, lambda *_: (0,) * nd, pipeline_mode=pl.Buffered(1))


def _out_ffn(x2, oa, ob, oc, wo, g, wg, wu, wd):
    n = x2.shape[0]
    tm = min(FFN_TILE, n)
    row = lambda i: (i, 0)
    return pl.pallas_call(
        _out_ffn_kernel,
        grid=(n // tm,),
        in_specs=[pl.BlockSpec((tm, D_MODEL), row), pl.BlockSpec((tm, SWA_Q_DIM), row),
                  pl.BlockSpec((tm, POOL_DIM), row), pl.BlockSpec((tm, MLA_OUT_DIM), row),
                  _resident(wo.shape), _resident(g.shape), _resident(wg.shape),
                  _resident(wu.shape), _resident(wd.shape)],
        out_specs=pl.BlockSpec((tm, D_MODEL), row),
        out_shape=jax.ShapeDtypeStruct((n, D_MODEL), jnp.float32),
        compiler_params=pltpu.CompilerParams(
            dimension_semantics=("arbitrary",), vmem_limit_bytes=VMEM_LIMIT),
        name="out_ffn",
    )(x2, oa, ob, oc, wo, g, wg, wu, wd)


def _block_diag_ones(groups, width):
    g = np.zeros((width, width), np.float32)
    for lo, hi in groups:
        g[lo:hi, lo:hi] = 1.0
    return jnp.asarray(g, jnp.bfloat16)


def _lane_rows(rows, width):
    out = jnp.zeros((8, width), jnp.float32)
    for r, v in enumerate(rows):
        out = out.at[r].set(v.astype(jnp.float32))
    return out


def _head_lanes(nope, rope):
    pad = jnp.zeros((MLA_HEAD_PAD - QK_HEAD,), jnp.float32)
    return jnp.tile(jnp.concatenate([nope.astype(jnp.float32), rope.astype(jnp.float32), pad]),
                    MLA_HEADS)


def _layer_consts(l, attn_norm, w_in, swa_q_gain, swa_k_gain, pool_w, pool_scale, mla_q_a_gain,
                  mla_w_qb, mla_kv_a_gain, mla_w_kvb, mla_q_nope_gain, mla_q_rope_gain,
                  mla_k_nope_gain, mla_k_rope_gain):
    f32, bf16 = jnp.float32, jnp.bfloat16
    zeros = lambda r, c: jnp.zeros((r, c), f32)
    split = C_KR
    w1 = jnp.concatenate([w_in[l][:, :split], zeros(D_MODEL, QK_NOPE), w_in[l][:, split:],
                          zeros(D_MODEL, LANE - QK_HEAD)], axis=1).astype(bf16)
    wqb = mla_w_qb[l].reshape(Q_LORA, MLA_HEADS, QK_HEAD)
    wqb = jnp.pad(wqb, ((0, 0), (0, 0), (0, MLA_HEAD_PAD - QK_HEAD))).reshape(Q_LORA, MLA_PAD_DIM)
    wkvb = mla_w_kvb[l].reshape(KV_LORA, MLA_HEADS, QK_NOPE + V_HEAD)
    wk = jnp.pad(wkvb[:, :, :QK_NOPE], ((0, 0), (0, 0), (0, MLA_HEAD_PAD - QK_NOPE)))
    wv = wkvb[:, :, QK_NOPE:]
    wkvb_p = jnp.concatenate([wk.reshape(KV_LORA, MLA_PAD_DIM), wv.reshape(KV_LORA, MLA_OUT_DIM)],
                             axis=1)
    wp = jnp.zeros((POOL_DIM, POOL_DIM), f32)
    for g in range(len(POOL_WINDOWS)):
        sl = slice(g * POOL_GROUP_DIM, (g + 1) * POOL_GROUP_DIM)
        wp = wp.at[sl, sl].set(pool_w[l, g])
    ones_n, ones_r = jnp.ones((QK_NOPE,), f32), jnp.ones((QK_ROPE,), f32)
    half = jnp.concatenate([jnp.zeros((ROPE_HALF,), f32), jnp.ones((ROPE_HALF,), f32)])
    v768 = _lane_rows([
        _head_lanes(mla_q_nope_gain[l], mla_q_rope_gain[l]) * (QK_HEAD ** -0.5 * LOG2E),
        _head_lanes(ones_n / QK_NOPE, ones_r / QK_ROPE),
        _head_lanes(mla_k_nope_gain[l], 0.0 * ones_r),
        _head_lanes(0.0 * ones_n, half),
        _head_lanes(0.0 * ones_n, half - 1.0),
    ], MLA_PAD_DIM)
    v384 = (jnp.tile(swa_q_gain[l], SWA_HEADS) * (HEAD_DIM ** -0.5 * LOG2E)).reshape(1, SWA_Q_DIM)
    windows = jnp.repeat(jnp.asarray(POOL_WINDOWS, f32), POOL_GROUP_DIM)
    v256 = _lane_rows([mla_q_a_gain[l], pool_scale[l], windows], POOL_DIM)
    inv_freq = ROPE_THETA ** (-jnp.arange(0, QK_ROPE, 2, dtype=f32) / QK_ROPE)
    one_head = lambda rope: _head_lanes(jnp.zeros((QK_NOPE,), f32), rope)[:LANE]
    v128 = _lane_rows([jnp.tile(swa_k_gain[l], SWA_KV_HEADS), mla_kv_a_gain[l],
                       one_head(mla_k_rope_gain[l])], LANE)
    g384 = _block_diag_ones([(h * HEAD_DIM, (h + 1) * HEAD_DIM) for h in range(SWA_HEADS)],
                            SWA_Q_DIM)
    gsum = np.zeros((MLA_PAD_DIM, LANE), np.float32)
    for h in range(MLA_HEADS):
        base = h * MLA_HEAD_PAD
        gsum[base:base + QK_NOPE, 2 * h] = 1.0
        gsum[base + QK_NOPE:base + QK_HEAD, 2 * h + 1] = 1.0
    return [attn_norm[l].reshape(1, D_MODEL), w1, g384, jnp.asarray(gsum, bf16),
            jnp.asarray(gsum.T, bf16), v768, v384, v256, v128, inv_freq.reshape(ROPE_HALF, 1),
            wp.astype(bf16), wqb.astype(bf16), wkvb_p.astype(bf16)]


def kernel(x, positions, rel_bias, attn_norm, w_in, swa_q_gain, swa_k_gain, swa_sinks, pool_w,
           pool_scale, mla_q_a_gain, mla_w_qb, mla_kv_a_gain, mla_w_kvb, mla_q_nope_gain,
           mla_q_rope_gain, mla_k_nope_gain, mla_k_rope_gain, w_out, ffn_norm, w_gate, w_up,
           w_down):
    batch, seq, _ = x.shape
    n = batch * seq
    depth = w_in.shape[0]
    bf16 = jnp.bfloat16
    x2 = x.reshape(n, D_MODEL)
    pos2 = positions.astype(jnp.float32).reshape(1, n)
    bias = _bias_table(rel_bias)
    for l in range(depth):
        consts = _layer_consts(l, attn_norm, w_in, swa_q_gain, swa_k_gain, pool_w, pool_scale,
                               mla_q_a_gain, mla_w_qb, mla_kv_a_gain, mla_w_kvb, mla_q_nope_gain,
                               mla_q_rope_gain, mla_k_nope_gain, mla_k_rope_gain)
        qs, ks, vs, pool_o, qm, km, vm = _in_proj(x2, pos2, consts, batch, seq)
        swa_o = _swa(swa_sinks[l], qs, ks, vs, bias, batch, seq)
        mla_o = _mla(qm, km, vm, batch, seq)
        x2 = _out_ffn(x2, swa_o, pool_o, mla_o, w_out[l].astype(bf16),
                      ffn_norm[l].reshape(1, D_MODEL), w_gate[l].astype(bf16),
                      w_up[l].astype(bf16), w_down[l].astype(bf16))
    return x2.reshape(batch, seq, D_MODEL)
```

```python
import functools
import math

import numpy as np
import jax
import jax.numpy as jnp
from jax import lax
from jax.experimental import pallas as pl
from jax.experimental.pallas import tpu as pltpu

D_MODEL = 1024
HEAD_DIM = 64
SWA_HEADS = 6
SWA_KV_HEADS = 2
SWA_GROUP = SWA_HEADS // SWA_KV_HEADS
WINDOW = 128
POOL_WINDOWS = (2, 4, 8, 16)
POOL_GROUP_DIM = 64
POOL_DIM = POOL_GROUP_DIM * len(POOL_WINDOWS)
MLA_HEADS = 6
Q_LORA = 256
KV_LORA = 128
QK_NOPE = 64
QK_ROPE = 32
V_HEAD = 64
QK_HEAD = QK_NOPE + QK_ROPE
ROPE_THETA = 10000.0
N_BUCKETS = 32
MAX_DISTANCE = 128
SWA_Q_DIM = SWA_HEADS * HEAD_DIM
SWA_KV_DIM = SWA_KV_HEADS * HEAD_DIM
MLA_OUT_DIM = MLA_HEADS * V_HEAD
D_FF = 2816
EPS = 1e-6

LANE = 128
MLA_HEAD_PAD = LANE
MLA_PAD_DIM = MLA_HEADS * MLA_HEAD_PAD
ROPE_HALF = QK_ROPE // 2
MLA_V_ROWS = 80
POOL_HALO = max(POOL_WINDOWS)
NEG = -1e30
LOG2E = math.log2(math.e)

C_QA = 0
C_KA = C_QA + SWA_Q_DIM
C_VA = C_KA + SWA_KV_DIM
C_UB = C_VA + SWA_KV_DIM
C_CQ = C_UB + POOL_DIM
C_CKV = C_CQ + Q_LORA
C_KR = C_CKV + KV_LORA
IN_PAD_DIM = C_KR + LANE

IN_TILE = 1024
IN_SPLIT = 2
SWA_TILE = 1024
MLA_TQ = 1024
MLA_BLK = 1024
MLA_SLAB = 256
MLA_UNROLL = 4
MLA_MAX_BOUND = 64.0
MLA_BOUND_MARGIN = 1.02
FFN_TILE = 1024
FFN_CHUNKS = ((0, 1536), (1536, 1280))
VMEM_LIMIT = 56 * 1024 * 1024


def _dot(a, b):
    return jnp.dot(a, b, preferred_element_type=jnp.float32)


def _dot_nt(a, b):
    return lax.dot_general(a, b, (((1,), (1,)), ((), ())), preferred_element_type=jnp.float32)


def _rms(x, width):
    return x * lax.rsqrt(jnp.sum(x * x, axis=-1, keepdims=True) * (1.0 / width) + EPS)


def _bias_kernel(rel_ref, bucket_ref, out_ref):
    bucket = bucket_ref[...]
    for h in range(SWA_HEADS):
        acc = jnp.full(bucket.shape, NEG, jnp.float32)
        for b in range(N_BUCKETS):
            acc = jnp.where(bucket == b, rel_ref[b, h] * LOG2E, acc)
        out_ref[h] = acc


def _band_buckets():
    q_loc = np.arange(WINDOW)[:, None]
    k_loc = np.arange(2 * WINDOW)[None, :]
    dist = q_loc + WINDOW - k_loc
    band_ok = (dist >= 0) & (dist < WINDOW)
    n = np.maximum(dist, 0)
    max_exact = N_BUCKETS // 2
    nf = np.maximum(n, 1).astype(np.float32)
    large = max_exact + (np.log(nf / max_exact) / math.log(MAX_DISTANCE / max_exact)
                         * (N_BUCKETS - max_exact)).astype(np.int32)
    large = np.minimum(large, N_BUCKETS - 1)
    bucket = np.where(n < max_exact, n, large)
    return np.where(band_ok, bucket, -1).astype(np.int32)


def _bias_table(rel_bias):
    return pl.pallas_call(
        _bias_kernel,
        out_shape=jax.ShapeDtypeStruct((SWA_HEADS, WINDOW, 2 * WINDOW), jnp.float32),
        in_specs=[pl.BlockSpec(memory_space=pltpu.SMEM),
                  pl.BlockSpec(memory_space=pltpu.VMEM)],
        out_specs=pl.BlockSpec(memory_space=pltpu.VMEM),
        name="bias_table",
    )(rel_bias, jnp.asarray(_band_buckets()))


def _in_proj_kernel(x_ref, pos_ref, g_attn_ref, w1_ref, g384_ref, gsum_ref, gexp_ref, v768_ref,
                    v384_ref, v256_ref, v128_ref, invf_ref, wp_ref, wqb_ref, wkvb_ref,
                    qs_ref, ks_ref, vs_ref, pool_ref, qm_ref, km_ref, vm_ref, carry_ref):
    i = pl.program_id(1)
    tm = x_ref.shape[0]
    bf16 = jnp.bfloat16

    x = x_ref[...]
    h = (_rms(x, D_MODEL) * g_attn_ref[...]).astype(bf16)
    bounds = [(r * tm // IN_SPLIT, (r + 1) * tm // IN_SPLIT) for r in range(IN_SPLIT)]
    projs = [_dot(h[r0:r1], w1_ref[...]) for r0, r1 in bounds]

    m1 = v768_ref[3:4, :]
    m2 = v768_ref[4:5, :]

    def rope(xn, cos_t, sin_t, m1_t, m2_t):
        width = xn.shape[1]
        rot = pltpu.roll(xn, ROPE_HALF, 1) * m1_t + pltpu.roll(xn, width - ROPE_HALF, 1) * m2_t
        return xn * cos_t + rot * sin_t

    def group_sums(v):
        compact = _dot((v * v).astype(bf16), gsum_ref[...])
        return _dot(compact.astype(bf16), gexp_ref[...])

    def prepare(proj, r0, r1):
        rows = r1 - r0
        qa = proj[:, C_QA:C_KA]
        ss = _dot((qa * qa).astype(bf16), g384_ref[...])
        qs_ref[r0:r1, :] = (qa * lax.rsqrt(ss * (1.0 / HEAD_DIM) + EPS) * v384_ref[...]).astype(bf16)
        ka = proj[:, C_KA:C_VA]
        ss = _dot((ka * ka).astype(bf16), g384_ref[0:SWA_KV_DIM, 0:SWA_KV_DIM])
        ks_ref[r0:r1, :] = (ka * lax.rsqrt(ss * (1.0 / HEAD_DIM) + EPS)
                            * v128_ref[0:1, :]).astype(bf16)
        vs_ref[r0:r1, :] = proj[:, C_VA:C_UB].astype(bf16)

        ang_t = invf_ref[...] * pos_ref[:, r0:r1]
        cos_t, sin_t = jnp.cos(ang_t), jnp.sin(ang_t)
        one_t = jnp.ones((QK_NOPE, rows), jnp.float32)
        zero_t = jnp.zeros((QK_NOPE, rows), jnp.float32)
        tail = LANE - QK_HEAD
        cos1 = jnp.concatenate([one_t, cos_t, cos_t, one_t[0:tail]], axis=0).T
        sin1 = jnp.concatenate([zero_t, sin_t, sin_t, zero_t[0:tail]], axis=0).T

        cq = proj[:, C_CQ:C_CKV]
        cqn = (_rms(cq, Q_LORA) * v256_ref[0:1, :]).astype(bf16)
        q = _dot(cqn, wqb_ref[...])
        qn = q * lax.rsqrt(group_sums(q) * v768_ref[1:2, :] + EPS) * v768_ref[0:1, :]
        cos6 = jnp.concatenate([cos1] * MLA_HEADS, axis=1)
        sin6 = jnp.concatenate([sin1] * MLA_HEADS, axis=1)
        qm_ref[r0:r1, :] = rope(qn, cos6, sin6, m1, m2).astype(bf16)

        kr = proj[:, C_KR:IN_PAD_DIM]
        krn = _rms(kr, QK_ROPE) * v128_ref[2:3, :]
        krr = rope(krn, cos1, sin1, m1[:, 0:LANE], m2[:, 0:LANE])
        ckv = proj[:, C_CKV:C_KR]
        ckvn = (_rms(ckv, KV_LORA) * v128_ref[1:2, :]).astype(bf16)
        kv = _dot(ckvn, wkvb_ref[...])
        kn = kv[:, 0:MLA_PAD_DIM]
        kn = kn * lax.rsqrt(group_sums(kn) * (1.0 / QK_NOPE) + EPS) * v768_ref[2:3, :]
        km_ref[r0:r1, :] = (kn + jnp.concatenate([krr] * MLA_HEADS, axis=1)).astype(bf16)
        vt = kv[:, MLA_PAD_DIM:].T
        fill = (lax.broadcasted_iota(jnp.int32, (MLA_V_ROWS - V_HEAD, rows), 0) == 0
                ).astype(jnp.float32)
        pieces = []
        for hd in range(MLA_HEADS):
            pieces += [vt[hd * V_HEAD:(hd + 1) * V_HEAD], fill]
        vm_ref[:, r0:r1] = jnp.concatenate(pieces, axis=0).astype(bf16)

    for (r0, r1), proj in zip(bounds, projs):
        prepare(proj, r0, r1)

    u = jnp.concatenate([proj[:, C_UB:C_CQ] for proj in projs], axis=0)

    @pl.when(i == 0)
    def _():
        carry_ref[...] = jnp.zeros_like(carry_ref)

    uext = jnp.concatenate([carry_ref[...], u], axis=0)
    a1 = uext + pltpu.roll(uext, 1, 0)
    a2 = a1 + pltpu.roll(a1, 2, 0)
    a3 = a2 + pltpu.roll(a2, 4, 0)
    a4 = a3 + pltpu.roll(a3, 8, 0)
    lane = lax.broadcasted_iota(jnp.int32, (1, POOL_DIM), 1)
    wsum = jnp.where(lane < POOL_GROUP_DIM, a1,
                     jnp.where(lane < 2 * POOL_GROUP_DIM, a2,
                               jnp.where(lane < 3 * POOL_GROUP_DIM, a3, a4)))[POOL_HALO:]
    t = (i * tm + lax.broadcasted_iota(jnp.int32, (tm, 1), 0) + 1).astype(jnp.float32)
    count = jnp.minimum(t, v256_ref[2:3, :])
    d = wsum / count - u
    pool_ref[...] = (_dot(d.astype(bf16), wp_ref[...]) * v256_ref[1:2, :]).astype(bf16)
    carry_ref[...] = u[tm - POOL_HALO:, :]


def _const_spec(shape):
    nd = len(shape)
    return pl.BlockSpec(shape, lambda *_: (0,) * nd)


def _layer_spec(shape, l, single_buffer=False):
    nd = len(shape)
    mode = dict(pipeline_mode=pl.Buffered(1)) if single_buffer else {}
    return pl.BlockSpec((None,) + tuple(shape[1:]), lambda *_: (l,) + (0,) * (nd - 1), **mode)


def _in_proj(x2, pos2, l, stacked, shared, batch, seq):
    g_attn, w1, v768, v384, v256, v128, wp, wqb, wkvb = stacked
    g384, gsum, gexp, invf = shared
    consts = [g_attn, w1, g384, gsum, gexp, v768, v384, v256, v128, invf, wp, wqb, wkvb]
    is_stacked = [True, True, False, False, False, True, True, True, True, False, True, True, True]
    const_specs = [_layer_spec(c.shape, l) if st else _const_spec(c.shape)
                   for c, st in zip(consts, is_stacked)]
    n = x2.shape[0]
    tm = min(IN_TILE, seq)
    nt = seq // tm
    row = lambda b, i: (b * nt + i, 0)
    bf16 = jnp.bfloat16
    col = lambda b, i: (0, b * nt + i)
    outs = [(SWA_Q_DIM, bf16), (SWA_KV_DIM, bf16), (SWA_KV_DIM, bf16), (POOL_DIM, bf16),
            (MLA_PAD_DIM, bf16), (MLA_PAD_DIM, bf16)]
    vt_rows = MLA_HEADS * MLA_V_ROWS
    return pl.pallas_call(
        _in_proj_kernel,
        grid=(batch, nt),
        in_specs=[pl.BlockSpec((tm, D_MODEL), row), pl.BlockSpec((1, tm), col)]
                 + const_specs,
        out_specs=[pl.BlockSpec((tm, w), row) for w, _ in outs]
                  + [pl.BlockSpec((vt_rows, tm), col)],
        out_shape=[jax.ShapeDtypeStruct((n, w), dt) for w, dt in outs]
                  + [jax.ShapeDtypeStruct((vt_rows, n), bf16)],
        scratch_shapes=[pltpu.VMEM((POOL_HALO, POOL_DIM), jnp.float32)],
        compiler_params=pltpu.CompilerParams(
            dimension_semantics=("arbitrary", "arbitrary"), vmem_limit_bytes=VMEM_LIMIT),
        name="in_proj",
    )(x2, pos2, *consts)


def _swa_kernel(sink_ref, q_ref, kc_ref, kp_ref, vc_ref, vp_ref, bias_ref, o_ref):
    i = pl.program_id(1)
    ts = q_ref.shape[0]
    nwin = ts // WINDOW
    bf16 = jnp.bfloat16
    q = q_ref[...]
    kext = jnp.concatenate([kp_ref[...], kc_ref[...]], axis=0)
    vext = jnp.concatenate([vp_ref[...], vc_ref[...]], axis=0)
    col = lax.broadcasted_iota(jnp.int32, (1, 2 * WINDOW), 1)
    no_prev = jnp.logical_and(i == 0, col < WINDOW)
    ones = jnp.ones((2 * WINDOW, LANE), bf16)
    heads = range(SWA_HEADS)
    sinks = [sink_ref[h] * LOG2E for h in heads]

    def scores(c, h):
        g = h // SWA_GROUP
        qh = q[c * WINDOW:(c + 1) * WINDOW, h * HEAD_DIM:(h + 1) * HEAD_DIM]
        kb = kext[c * WINDOW:(c + 2) * WINDOW, g * HEAD_DIM:(g + 1) * HEAD_DIM]
        s = _dot_nt(qh, kb) + bias_ref[h]
        if c == 0:
            s = jnp.where(no_prev, NEG, s)
        return s

    pending = [scores(0, h) for h in heads]
    for c in range(nwin):
        s_all = pending
        ms = [jnp.maximum(jnp.max(s_all[h], axis=-1, keepdims=True), sinks[h]) for h in heads]
        pending = [scores(c + 1, h) for h in heads] if c + 1 < nwin else []
        ps = [jnp.exp2(s_all[h] - ms[h]).astype(bf16) for h in heads]
        vaug = jnp.concatenate([vext[c * WINDOW:(c + 2) * WINDOW], ones], axis=1)
        os_ = [_dot(ps[h], vaug) for h in heads]
        outs = []
        for h in heads:
            lo = (h // SWA_GROUP) * HEAD_DIM
            denom = os_[h][:, SWA_KV_DIM + lo:SWA_KV_DIM + lo + HEAD_DIM] + jnp.exp2(sinks[h] - ms[h])
            outs.append(os_[h][:, lo:lo + HEAD_DIM] / denom)
        o_ref[c * WINDOW:(c + 1) * WINDOW, :] = jnp.concatenate(outs, axis=1).astype(bf16)


def _swa(sinks, qs, ks, vs, bias, batch, seq):
    n = qs.shape[0]
    ts = min(SWA_TILE, seq)
    nt = seq // ts
    wpt = ts // WINDOW
    wps = seq // WINDOW
    row = lambda b, i: (b * nt + i, 0)
    prev = lambda b, i: (b * wps + jnp.maximum(i * wpt - 1, 0), 0)
    return pl.pallas_call(
        _swa_kernel,
        grid=(batch, nt),
        in_specs=[pl.BlockSpec(memory_space=pltpu.SMEM),
                  pl.BlockSpec((ts, SWA_Q_DIM), row),
                  pl.BlockSpec((ts, SWA_KV_DIM), row), pl.BlockSpec((WINDOW, SWA_KV_DIM), prev),
                  pl.BlockSpec((ts, SWA_KV_DIM), row), pl.BlockSpec((WINDOW, SWA_KV_DIM), prev),
                  _const_spec(bias.shape)],
        out_specs=pl.BlockSpec((ts, SWA_Q_DIM), row),
        out_shape=jax.ShapeDtypeStruct((n, SWA_Q_DIM), jnp.bfloat16),
        compiler_params=pltpu.CompilerParams(
            dimension_semantics=("arbitrary", "arbitrary"), vmem_limit_bytes=VMEM_LIMIT),
        name="swa",
    )(sinks, qs, ks, ks, vs, vs, bias)


def _mla_kernel(q_ref, k_ref, vt_ref, o_ref, m_ref, acc_ref, sa_ref, sb_ref, ma_ref, mb_ref):
    i = pl.program_id(2)
    tq = q_ref.shape[0]
    blk = max(MLA_BLK, tq)
    ratio = blk // tq
    bf16 = jnp.bfloat16
    chains = [(a, c) for a in range(2) for c in range(tq // MLA_SLAB)]
    m_ref[...] = jnp.full(m_ref.shape, NEG, jnp.float32)
    acc_ref[...] = jnp.zeros(acc_ref.shape, jnp.float32)

    def score(a, c, start, nk):
        lanes = slice(a * LANE, (a + 1) * LANE)
        return _dot_nt(k_ref[pl.ds(start, nk), lanes],
                       q_ref[c * MLA_SLAB:(c + 1) * MLA_SLAB, lanes])

    def full_score(ch, j, s_ref, mx_ref):
        a, c = chains[ch]
        s = score(a, c, pl.multiple_of(j * blk, blk), blk)
        s_ref[ch] = s
        mx_ref[ch] = jnp.max(s, axis=0, keepdims=True)

    def tail_score(ch, start, nfree):
        a, c = chains[ch]
        nk = nfree + (c + 1) * MLA_SLAB
        s = score(a, c, start, nk)
        row = lax.broadcasted_iota(jnp.int32, (nk, MLA_SLAB), 0)
        col = lax.broadcasted_iota(jnp.int32, (nk, MLA_SLAB), 1) + (c * MLA_SLAB + nfree)
        s = jnp.where(row <= col, s, NEG)
        return s, jnp.max(s, axis=0, keepdims=True)

    def soft_pv(ch, s, mx, start):
        a, c = chains[ch]
        cols = slice(c * MLA_SLAB, (c + 1) * MLA_SLAB)
        m_prev = m_ref[a, :, cols]
        m_new = jnp.maximum(m_prev, mx)
        alpha = jnp.exp2(m_prev - m_new)
        p = jnp.exp2(s - m_new).astype(bf16)
        vt = vt_ref[a * MLA_V_ROWS:(a + 1) * MLA_V_ROWS, pl.ds(start, s.shape[0])]
        acc_ref[a, :, cols] = alpha * acc_ref[a, :, cols] + _dot(vt, p)
        m_ref[a, :, cols] = m_new

    bufs = ((sa_ref, ma_ref), (sb_ref, mb_ref))
    nchain = len(chains)

    def pipe_step(j, cur, nxt):
        s_cur, mx_cur = bufs[cur]
        full_score(0, j + 1, *bufs[nxt])
        for ch in range(nchain):
            if ch + 1 < nchain:
                full_score(ch + 1, j + 1, *bufs[nxt])
            soft_pv(ch, s_cur[ch], mx_cur[ch], pl.multiple_of(j * blk, blk))

    nfull = i // ratio

    @pl.when(nfull > 0)
    def _():
        for ch in range(nchain):
            full_score(ch, 0, *bufs[0])

    def pair_body(t, carry):
        pipe_step(2 * t, 0, 1)
        pipe_step(2 * t + 1, 1, 0)
        return carry

    lax.fori_loop(0, jnp.maximum(nfull - 1, 0) // 2, pair_body, 0)

    def finish(has_full, last, nfree):
        if has_full and last == 1:
            pipe_step(nfull - 2, 0, 1)
        tail_start = pl.multiple_of(nfull * blk, blk)
        tail = [tail_score(0, tail_start, nfree)]
        for ch in range(nchain):
            if ch + 1 < nchain:
                tail.append(tail_score(ch + 1, tail_start, nfree))
            if has_full:
                s_last, mx_last = bufs[last]
                soft_pv(ch, s_last[ch], mx_last[ch], pl.multiple_of((nfull - 1) * blk, blk))
        for ch in range(nchain):
            soft_pv(ch, *tail[ch], tail_start)

    for r in range(ratio):
        nfree = r * tq
        is_odd = (i % ratio) == r
        pl.when(jnp.logical_and(nfull == 0, is_odd))(functools.partial(finish, False, 0, nfree))
        for last in (0, 1):
            cond = jnp.logical_and(jnp.logical_and(nfull > 0, (nfull - 1) % 2 == last), is_odd)
            pl.when(cond)(functools.partial(finish, True, last, nfree))

    outs = []
    for a in range(2):
        acc = acc_ref[a]
        outs.append(acc[0:V_HEAD] / acc[V_HEAD:V_HEAD + 1])
    o_ref[...] = jnp.concatenate(outs, axis=0).T.astype(bf16)


def _mla(qm, km, vm, batch, seq):
    n = qm.shape[0]
    tq = min(MLA_TQ, seq)
    nq = seq // tq
    pairs = MLA_HEADS // 2
    nchain = 2 * (tq // MLA_SLAB)
    return pl.pallas_call(
        _mla_kernel,
        grid=(batch, pairs, nq),
        in_specs=[pl.BlockSpec((tq, 2 * LANE), lambda b, p, i: (b * nq + i, p)),
                  pl.BlockSpec((seq, 2 * LANE), lambda b, p, i: (b, p)),
                  pl.BlockSpec((2 * MLA_V_ROWS, seq), lambda b, p, i: (p, b))],
        out_specs=pl.BlockSpec((tq, LANE), lambda b, p, i: (b * nq + i, p)),
        out_shape=jax.ShapeDtypeStruct((n, MLA_OUT_DIM), jnp.bfloat16),
        scratch_shapes=[pltpu.VMEM((2, 1, tq), jnp.float32),
                        pltpu.VMEM((2, MLA_V_ROWS, tq), jnp.float32)]
                       + [pltpu.VMEM((nchain, max(MLA_BLK, tq), MLA_SLAB), jnp.float32)] * 2
                       + [pltpu.VMEM((nchain, 1, MLA_SLAB), jnp.float32)] * 2,
        compiler_params=pltpu.CompilerParams(
            dimension_semantics=("arbitrary", "arbitrary", "arbitrary"),
            vmem_limit_bytes=VMEM_LIMIT),
        name="mla",
    )(qm, km, vm)


def _mla_bounded_kernel(q_ref, k_ref, vt_ref, o_ref, acc_ref):
    i = pl.program_id(2)
    tq = q_ref.shape[0]
    blk = tq
    bf16 = jnp.bfloat16
    chains = [(a, c) for a in range(2) for c in range(tq // MLA_SLAB)]
    nchain = len(chains)
    acc_ref[...] = jnp.zeros(acc_ref.shape, jnp.float32)

    def score(ch, start, nk):
        a, c = chains[ch]
        lanes = slice(a * LANE, (a + 1) * LANE)
        return _dot_nt(k_ref[pl.ds(start, nk), lanes],
                       q_ref[c * MLA_SLAB:(c + 1) * MLA_SLAB, lanes])

    def weigh(ch, s, start):
        a, c = chains[ch]
        cols = slice(c * MLA_SLAB, (c + 1) * MLA_SLAB)
        p = jnp.exp2(s).astype(bf16)
        vt = vt_ref[a * MLA_V_ROWS:(a + 1) * MLA_V_ROWS, pl.ds(start, s.shape[0])]
        acc_ref[a, :, cols] += _dot(vt, p)

    def blocks(starts, masked_last):
        work = [(ch, st, masked_last and b == len(starts) - 1)
                for b, st in enumerate(starts) for ch in range(nchain)]

        def chain_scores(ch, start, masked):
            if not masked:
                return score(ch, start, blk)
            c = chains[ch][1]
            nk = (c + 1) * MLA_SLAB
            s = score(ch, start, nk)
            row = lax.broadcasted_iota(jnp.int32, (nk, MLA_SLAB), 0)
            col = lax.broadcasted_iota(jnp.int32, (nk, MLA_SLAB), 1) + c * MLA_SLAB
            return jnp.where(row <= col, s, NEG)

        s_cur = chain_scores(*work[0])
        for w in range(len(work)):
            s_nxt = chain_scores(*work[w + 1]) if w + 1 < len(work) else None
            weigh(work[w][0], s_cur, work[w][1])
            s_cur = s_nxt

    nb = MLA_UNROLL

    def body(t, carry):
        st = pl.multiple_of(nb * t * blk, blk)
        blocks([st + k * blk for k in range(nb)], False)
        return carry

    lax.fori_loop(0, i // nb, body, 0)
    for r in range(nb):
        def tail(r=r):
            first = pl.multiple_of((i - r) * blk, blk)
            blocks([first + k * blk for k in range(r + 1)], True)
        pl.when(i % nb == r)(tail)

    outs = []
    for a in range(2):
        acc = acc_ref[a]
        outs.append(acc[0:V_HEAD] / acc[V_HEAD:V_HEAD + 1])
    o_ref[...] = jnp.concatenate(outs, axis=0).T.astype(bf16)


def _mla_bounded(qm, km, vm, batch, seq):
    n = qm.shape[0]
    tq = min(MLA_TQ, seq)
    nq = seq // tq
    pairs = MLA_HEADS // 2
    return pl.pallas_call(
        _mla_bounded_kernel,
        grid=(batch, pairs, nq),
        in_specs=[pl.BlockSpec((tq, 2 * LANE), lambda b, p, i: (b * nq + i, p)),
                  pl.BlockSpec((seq, 2 * LANE), lambda b, p, i: (b, p)),
                  pl.BlockSpec((2 * MLA_V_ROWS, seq), lambda b, p, i: (p, b))],
        out_specs=pl.BlockSpec((tq, LANE), lambda b, p, i: (b * nq + i, p)),
        out_shape=jax.ShapeDtypeStruct((n, MLA_OUT_DIM), jnp.bfloat16),
        scratch_shapes=[pltpu.VMEM((2, MLA_V_ROWS, tq), jnp.float32)],
        compiler_params=pltpu.CompilerParams(
            dimension_semantics=("arbitrary", "arbitrary", "arbitrary"),
            vmem_limit_bytes=VMEM_LIMIT),
        name="mla_bounded",
    )(qm, km, vm)


def _mla_score_bound(q_nope_gain, q_rope_gain, k_nope_gain, k_rope_gain):
    gmax = lambda g: jnp.max(jnp.abs(g.astype(jnp.float32)), axis=-1)
    dots = (QK_NOPE * gmax(q_nope_gain) * gmax(k_nope_gain)
            + QK_ROPE * gmax(q_rope_gain) * gmax(k_rope_gain))
    return dots * (QK_HEAD ** -0.5 * LOG2E * MLA_BOUND_MARGIN)


def _out_ffn_kernel(x_ref, a_ref, b_ref, c_ref, wo_ref, g_ref, wg_ref, wu_ref, wd_ref, o_ref):
    bf16 = jnp.bfloat16
    mixed = jnp.concatenate([a_ref[...], b_ref[...], c_ref[...]], axis=1)
    x1 = x_ref[...] + _dot(mixed, wo_ref[...])
    h = (_rms(x1, D_MODEL) * g_ref[...]).astype(bf16)
    acc = x1
    for start, size in FFN_CHUNKS:
        gate = _dot(h, wg_ref[:, start:start + size])
        up = _dot(h, wu_ref[:, start:start + size])
        act = (gate * jax.nn.sigmoid(gate) * up).astype(bf16)
        acc = acc + _dot(act, wd_ref[start:start + size, :])
    o_ref[...] = acc


def _out_ffn(x2, oa, ob, oc, l, wo, g, wg, wu, wd):
    n = x2.shape[0]
    tm = min(FFN_TILE, n)
    row = lambda i: (i, 0)
    return pl.pallas_call(
        _out_ffn_kernel,
        grid=(n // tm,),
        in_specs=[pl.BlockSpec((tm, D_MODEL), row), pl.BlockSpec((tm, SWA_Q_DIM), row),
                  pl.BlockSpec((tm, POOL_DIM), row), pl.BlockSpec((tm, MLA_OUT_DIM), row)]
                 + [_layer_spec(w.shape, l, single_buffer=True) for w in (wo, g, wg, wu, wd)],
        out_specs=pl.BlockSpec((tm, D_MODEL), row),
        out_shape=jax.ShapeDtypeStruct((n, D_MODEL), jnp.float32),
        compiler_params=pltpu.CompilerParams(
            dimension_semantics=("arbitrary",), vmem_limit_bytes=VMEM_LIMIT),
        name="out_ffn",
    )(x2, oa, ob, oc, wo, g, wg, wu, wd)


def _block_diag_ones(groups, width):
    g = np.zeros((width, width), np.float32)
    for lo, hi in groups:
        g[lo:hi, lo:hi] = 1.0
    return jnp.asarray(g, jnp.bfloat16)


def _lane_rows(rows):
    rows = [r.astype(jnp.float32) for r in rows]
    rows += [jnp.zeros_like(rows[0])] * (8 - len(rows))
    return jnp.stack(rows, axis=1)


def _head_lanes(nope, rope):
    pad = jnp.zeros(nope.shape[:-1] + (MLA_HEAD_PAD - QK_HEAD,), jnp.float32)
    one = jnp.concatenate([nope.astype(jnp.float32), rope.astype(jnp.float32), pad], axis=-1)
    return jnp.tile(one, (1,) * (one.ndim - 1) + (MLA_HEADS,))


def _pack_params(attn_norm, w_in, swa_q_gain, swa_k_gain, pool_w, pool_scale, mla_q_a_gain,
                 mla_w_qb, mla_kv_a_gain, mla_w_kvb, mla_q_nope_gain, mla_q_rope_gain,
                 mla_k_nope_gain, mla_k_rope_gain):
    f32, bf16 = jnp.float32, jnp.bfloat16
    depth = w_in.shape[0]
    zeros = lambda c: jnp.zeros((depth, D_MODEL, c), f32)
    w1 = jnp.concatenate([w_in[:, :, :C_KR], zeros(QK_NOPE), w_in[:, :, C_KR:],
                          zeros(LANE - QK_HEAD)], axis=2).astype(bf16)
    wqb = mla_w_qb.reshape(depth, Q_LORA, MLA_HEADS, QK_HEAD)
    wqb = jnp.pad(wqb, ((0, 0), (0, 0), (0, 0), (0, MLA_HEAD_PAD - QK_HEAD)))
    wqb = wqb.reshape(depth, Q_LORA, MLA_PAD_DIM).astype(bf16)
    wkvb = mla_w_kvb.reshape(depth, KV_LORA, MLA_HEADS, QK_NOPE + V_HEAD)
    wk = jnp.pad(wkvb[..., :QK_NOPE], ((0, 0), (0, 0), (0, 0), (0, MLA_HEAD_PAD - QK_NOPE)))
    wkvb_p = jnp.concatenate([wk.reshape(depth, KV_LORA, MLA_PAD_DIM),
                              wkvb[..., QK_NOPE:].reshape(depth, KV_LORA, MLA_OUT_DIM)],
                             axis=2).astype(bf16)
    ngroup = len(POOL_WINDOWS)
    eye = jnp.eye(ngroup, dtype=f32)[None, :, None, :, None]
    wp = (eye * pool_w[:, :, :, None, :]).reshape(depth, POOL_DIM, POOL_DIM).astype(bf16)
    ones_n, ones_r = jnp.ones((depth, QK_NOPE), f32), jnp.ones((depth, QK_ROPE), f32)
    half = jnp.concatenate([jnp.zeros((depth, ROPE_HALF), f32), jnp.ones((depth, ROPE_HALF), f32)],
                           axis=1)
    v768 = _lane_rows([
        _head_lanes(mla_q_nope_gain, mla_q_rope_gain) * (QK_HEAD ** -0.5 * LOG2E),
        _head_lanes(ones_n / QK_NOPE, ones_r / QK_ROPE),
        _head_lanes(mla_k_nope_gain, 0.0 * ones_r),
        _head_lanes(0.0 * ones_n, half),
        _head_lanes(0.0 * ones_n, half - 1.0),
    ])
    v384 = (jnp.tile(swa_q_gain, (1, SWA_HEADS)) * (HEAD_DIM ** -0.5 * LOG2E))[:, None, :]
    windows = jnp.broadcast_to(jnp.repeat(jnp.asarray(POOL_WINDOWS, f32), POOL_GROUP_DIM),
                               (depth, POOL_DIM))
    v256 = _lane_rows([mla_q_a_gain, pool_scale, windows])
    v128 = _lane_rows([jnp.tile(swa_k_gain, (1, SWA_KV_HEADS)), mla_kv_a_gain,
                       _head_lanes(0.0 * ones_n, mla_k_rope_gain)[:, :LANE]])
    stacked = (attn_norm[:, None, :], w1, v768, v384, v256, v128, wp, wqb, wkvb_p)

    inv_freq = ROPE_THETA ** (-jnp.arange(0, QK_ROPE, 2, dtype=f32) / QK_ROPE)
    g384 = _block_diag_ones([(h * HEAD_DIM, (h + 1) * HEAD_DIM) for h in range(SWA_HEADS)],
                            SWA_Q_DIM)
    gsum = np.zeros((MLA_PAD_DIM, LANE), np.float32)
    for h in range(MLA_HEADS):
        base = h * MLA_HEAD_PAD
        gsum[base:base + QK_NOPE, 2 * h] = 1.0
        gsum[base + QK_NOPE:base + QK_HEAD, 2 * h + 1] = 1.0
    shared = (g384, jnp.asarray(gsum, bf16), jnp.asarray(gsum.T, bf16),
              inv_freq.reshape(ROPE_HALF, 1))
    return stacked, shared


def kernel(x, positions, rel_bias, attn_norm, w_in, swa_q_gain, swa_k_gain, swa_sinks, pool_w,
           pool_scale, mla_q_a_gain, mla_w_qb, mla_kv_a_gain, mla_w_kvb, mla_q_nope_gain,
           mla_q_rope_gain, mla_k_nope_gain, mla_k_rope_gain, w_out, ffn_norm, w_gate, w_up,
           w_down):
    batch, seq, _ = x.shape
    n = batch * seq
    depth = w_in.shape[0]
    bf16 = jnp.bfloat16
    x2 = x.reshape(n, D_MODEL)
    pos2 = positions.astype(jnp.float32).reshape(1, n)
    bias = _bias_table(rel_bias)
    stacked, shared = _pack_params(attn_norm, w_in, swa_q_gain, swa_k_gain, pool_w, pool_scale,
                                   mla_q_a_gain, mla_w_qb, mla_kv_a_gain, mla_w_kvb,
                                   mla_q_nope_gain, mla_q_rope_gain, mla_k_nope_gain,
                                   mla_k_rope_gain)
    ffn = (w_out.astype(bf16), ffn_norm[:, None, :], w_gate.astype(bf16), w_up.astype(bf16),
           w_down.astype(bf16))
    score_bound = _mla_score_bound(mla_q_nope_gain, mla_q_rope_gain, mla_k_nope_gain,
                                   mla_k_rope_gain)
    for l in range(depth):
        qs, ks, vs, pool_o, qm, km, vm = _in_proj(x2, pos2, l, stacked, shared, batch, seq)
        swa_o = _swa(swa_sinks[l], qs, ks, vs, bias, batch, seq)
        mla_o = lax.cond(score_bound[l] <= MLA_MAX_BOUND,
                         functools.partial(_mla_bounded, batch=batch, seq=seq),
                         functools.partial(_mla, batch=batch, seq=seq), qm, km, vm)
        x2 = _out_ffn(x2, swa_o, pool_o, mla_o, l, *ffn)
    return x2.reshape(batch, seq, D_MODEL)
```

```python
import functools
import math

import numpy as np
import jax
import jax.numpy as jnp
from jax import lax
from jax.experimental import pallas as pl
from jax.experimental.pallas import tpu as pltpu

D_MODEL = 1024
HEAD_DIM = 64
SWA_HEADS = 6
SWA_KV_HEADS = 2
SWA_GROUP = SWA_HEADS // SWA_KV_HEADS
WINDOW = 128
POOL_WINDOWS = (2, 4, 8, 16)
POOL_GROUP_DIM = 64
POOL_DIM = POOL_GROUP_DIM * len(POOL_WINDOWS)
MLA_HEADS = 6
Q_LORA = 256
KV_LORA = 128
QK_NOPE = 64
QK_ROPE = 32
V_HEAD = 64
QK_HEAD = QK_NOPE + QK_ROPE
ROPE_THETA = 10000.0
N_BUCKETS = 32
MAX_DISTANCE = 128
SWA_Q_DIM = SWA_HEADS * HEAD_DIM
SWA_KV_DIM = SWA_KV_HEADS * HEAD_DIM
MLA_OUT_DIM = MLA_HEADS * V_HEAD
D_FF = 2816
EPS = 1e-6

LANE = 128
MLA_HEAD_PAD = LANE
MLA_PAD_DIM = MLA_HEADS * MLA_HEAD_PAD
ROPE_HALF = QK_ROPE // 2
MLA_V_ROWS = 80
POOL_HALO = max(POOL_WINDOWS)
NEG = -1e30
LOG2E = math.log2(math.e)

C_QA = 0
C_KA = C_QA + SWA_Q_DIM
C_VA = C_KA + SWA_KV_DIM
C_UB = C_VA + SWA_KV_DIM
C_CQ = C_UB + POOL_DIM
C_CKV = C_CQ + Q_LORA
C_KR = C_CKV + KV_LORA
IN_PAD_DIM = C_KR + LANE

IN_TILE = 1024
IN_SPLIT = 2
SWA_TILE = 1024
MLA_TQ = 1024
MLA_BLK = 1024
MLA_SLAB = 256
MLA_UNROLL = 4
MLA_AHEAD = 2
MLA_MAX_BOUND = 64.0
MLA_BOUND_MARGIN = 1.02
FFN_TILE = 1024
FFN_CHUNKS = ((0, 1536), (1536, 1280))
VMEM_LIMIT = 56 * 1024 * 1024


def _dot(a, b):
    return jnp.dot(a, b, preferred_element_type=jnp.float32)


def _dot_nt(a, b):
    return lax.dot_general(a, b, (((1,), (1,)), ((), ())), preferred_element_type=jnp.float32)


def _rms(x, width):
    return x * lax.rsqrt(jnp.sum(x * x, axis=-1, keepdims=True) * (1.0 / width) + EPS)


def _bias_kernel(rel_ref, bucket_ref, out_ref):
    bucket = bucket_ref[...]
    for h in range(SWA_HEADS):
        acc = jnp.full(bucket.shape, NEG, jnp.float32)
        for b in range(N_BUCKETS):
            acc = jnp.where(bucket == b, rel_ref[b, h] * LOG2E, acc)
        out_ref[h] = acc


def _band_buckets():
    q_loc = np.arange(WINDOW)[:, None]
    k_loc = np.arange(2 * WINDOW)[None, :]
    dist = q_loc + WINDOW - k_loc
    band_ok = (dist >= 0) & (dist < WINDOW)
    n = np.maximum(dist, 0)
    max_exact = N_BUCKETS // 2
    nf = np.maximum(n, 1).astype(np.float32)
    large = max_exact + (np.log(nf / max_exact) / math.log(MAX_DISTANCE / max_exact)
                         * (N_BUCKETS - max_exact)).astype(np.int32)
    large = np.minimum(large, N_BUCKETS - 1)
    bucket = np.where(n < max_exact, n, large)
    return np.where(band_ok, bucket, -1).astype(np.int32)


def _bias_table(rel_bias):
    return pl.pallas_call(
        _bias_kernel,
        out_shape=jax.ShapeDtypeStruct((SWA_HEADS, WINDOW, 2 * WINDOW), jnp.float32),
        in_specs=[pl.BlockSpec(memory_space=pltpu.SMEM),
                  pl.BlockSpec(memory_space=pltpu.VMEM)],
        out_specs=pl.BlockSpec(memory_space=pltpu.VMEM),
        name="bias_table",
    )(rel_bias, jnp.asarray(_band_buckets()))


def _in_proj_kernel(x_ref, pos_ref, g_attn_ref, w1_ref, g384_ref, gsum_ref, gexp_ref, v768_ref,
                    v384_ref, v256_ref, v128_ref, invf_ref, wp_ref, wqb_ref, wkvb_ref,
                    qs_ref, ks_ref, vs_ref, pool_ref, qm_ref, km_ref, vm_ref, carry_ref):
    i = pl.program_id(1)
    tm = x_ref.shape[0]
    bf16 = jnp.bfloat16

    x = x_ref[...]
    h = (_rms(x, D_MODEL) * g_attn_ref[...]).astype(bf16)
    bounds = [(r * tm // IN_SPLIT, (r + 1) * tm // IN_SPLIT) for r in range(IN_SPLIT)]
    projs = [_dot(h[r0:r1], w1_ref[...]) for r0, r1 in bounds]

    m1 = v768_ref[3:4, :]
    m2 = v768_ref[4:5, :]

    def rope(xn, cos_t, sin_t, m1_t, m2_t):
        width = xn.shape[1]
        rot = pltpu.roll(xn, ROPE_HALF, 1) * m1_t + pltpu.roll(xn, width - ROPE_HALF, 1) * m2_t
        return xn * cos_t + rot * sin_t

    def group_sums(v):
        compact = _dot((v * v).astype(bf16), gsum_ref[...])
        return _dot(compact.astype(bf16), gexp_ref[...])

    def prepare(proj, r0, r1):
        rows = r1 - r0
        qa = proj[:, C_QA:C_KA]
        ss = _dot((qa * qa).astype(bf16), g384_ref[...])
        qs_ref[r0:r1, :] = (qa * lax.rsqrt(ss * (1.0 / HEAD_DIM) + EPS) * v384_ref[...]).astype(bf16)
        ka = proj[:, C_KA:C_VA]
        ss = _dot((ka * ka).astype(bf16), g384_ref[0:SWA_KV_DIM, 0:SWA_KV_DIM])
        ks_ref[r0:r1, :] = (ka * lax.rsqrt(ss * (1.0 / HEAD_DIM) + EPS)
                            * v128_ref[0:1, :]).astype(bf16)
        vs_ref[r0:r1, :] = proj[:, C_VA:C_UB].astype(bf16)

        ang_t = invf_ref[...] * pos_ref[:, r0:r1]
        cos_t, sin_t = jnp.cos(ang_t), jnp.sin(ang_t)
        one_t = jnp.ones((QK_NOPE, rows), jnp.float32)
        zero_t = jnp.zeros((QK_NOPE, rows), jnp.float32)
        tail = LANE - QK_HEAD
        cos1 = jnp.concatenate([one_t, cos_t, cos_t, one_t[0:tail]], axis=0).T
        sin1 = jnp.concatenate([zero_t, sin_t, sin_t, zero_t[0:tail]], axis=0).T

        cq = proj[:, C_CQ:C_CKV]
        cqn = (_rms(cq, Q_LORA) * v256_ref[0:1, :]).astype(bf16)
        q = _dot(cqn, wqb_ref[...])
        qn = q * lax.rsqrt(group_sums(q) * v768_ref[1:2, :] + EPS) * v768_ref[0:1, :]
        cos6 = jnp.concatenate([cos1] * MLA_HEADS, axis=1)
        sin6 = jnp.concatenate([sin1] * MLA_HEADS, axis=1)
        qm_ref[r0:r1, :] = rope(qn, cos6, sin6, m1, m2).astype(bf16)

        kr = proj[:, C_KR:IN_PAD_DIM]
        krn = _rms(kr, QK_ROPE) * v128_ref[2:3, :]
        krr = rope(krn, cos1, sin1, m1[:, 0:LANE], m2[:, 0:LANE])
        ckv = proj[:, C_CKV:C_KR]
        ckvn = (_rms(ckv, KV_LORA) * v128_ref[1:2, :]).astype(bf16)
        kv = _dot(ckvn, wkvb_ref[...])
        kn = kv[:, 0:MLA_PAD_DIM]
        kn = kn * lax.rsqrt(group_sums(kn) * (1.0 / QK_NOPE) + EPS) * v768_ref[2:3, :]
        km_ref[r0:r1, :] = (kn + jnp.concatenate([krr] * MLA_HEADS, axis=1)).astype(bf16)
        vt = kv[:, MLA_PAD_DIM:].T
        fill = (lax.broadcasted_iota(jnp.int32, (MLA_V_ROWS - V_HEAD, rows), 0) == 0
                ).astype(jnp.float32)
        pieces = []
        for hd in range(MLA_HEADS):
            pieces += [vt[hd * V_HEAD:(hd + 1) * V_HEAD], fill]
        vm_ref[:, r0:r1] = jnp.concatenate(pieces, axis=0).astype(bf16)

    for (r0, r1), proj in zip(bounds, projs):
        prepare(proj, r0, r1)

    u = jnp.concatenate([proj[:, C_UB:C_CQ] for proj in projs], axis=0)

    @pl.when(i == 0)
    def _():
        carry_ref[...] = jnp.zeros_like(carry_ref)

    uext = jnp.concatenate([carry_ref[...], u], axis=0)
    a1 = uext + pltpu.roll(uext, 1, 0)
    a2 = a1 + pltpu.roll(a1, 2, 0)
    a3 = a2 + pltpu.roll(a2, 4, 0)
    a4 = a3 + pltpu.roll(a3, 8, 0)
    lane = lax.broadcasted_iota(jnp.int32, (1, POOL_DIM), 1)
    wsum = jnp.where(lane < POOL_GROUP_DIM, a1,
                     jnp.where(lane < 2 * POOL_GROUP_DIM, a2,
                               jnp.where(lane < 3 * POOL_GROUP_DIM, a3, a4)))[POOL_HALO:]
    t = (i * tm + lax.broadcasted_iota(jnp.int32, (tm, 1), 0) + 1).astype(jnp.float32)
    count = jnp.minimum(t, v256_ref[2:3, :])
    d = wsum / count - u
    pool_ref[...] = (_dot(d.astype(bf16), wp_ref[...]) * v256_ref[1:2, :]).astype(bf16)
    carry_ref[...] = u[tm - POOL_HALO:, :]


def _const_spec(shape):
    nd = len(shape)
    return pl.BlockSpec(shape, lambda *_: (0,) * nd)


def _layer_spec(shape, l, single_buffer=False):
    nd = len(shape)
    mode = dict(pipeline_mode=pl.Buffered(1)) if single_buffer else {}
    return pl.BlockSpec((None,) + tuple(shape[1:]), lambda *_: (l,) + (0,) * (nd - 1), **mode)


def _in_proj(x2, pos2, l, stacked, shared, batch, seq):
    g_attn, w1, v768, v384, v256, v128, wp, wqb, wkvb = stacked
    g384, gsum, gexp, invf = shared
    consts = [g_attn, w1, g384, gsum, gexp, v768, v384, v256, v128, invf, wp, wqb, wkvb]
    is_stacked = [True, True, False, False, False, True, True, True, True, False, True, True, True]
    const_specs = [_layer_spec(c.shape, l) if st else _const_spec(c.shape)
                   for c, st in zip(consts, is_stacked)]
    n = x2.shape[0]
    tm = min(IN_TILE, seq)
    nt = seq // tm
    row = lambda b, i: (b * nt + i, 0)
    bf16 = jnp.bfloat16
    col = lambda b, i: (0, b * nt + i)
    outs = [(SWA_Q_DIM, bf16), (SWA_KV_DIM, bf16), (SWA_KV_DIM, bf16), (POOL_DIM, bf16),
            (MLA_PAD_DIM, bf16), (MLA_PAD_DIM, bf16)]
    vt_rows = MLA_HEADS * MLA_V_ROWS
    return pl.pallas_call(
        _in_proj_kernel,
        grid=(batch, nt),
        in_specs=[pl.BlockSpec((tm, D_MODEL), row), pl.BlockSpec((1, tm), col)]
                 + const_specs,
        out_specs=[pl.BlockSpec((tm, w), row) for w, _ in outs]
                  + [pl.BlockSpec((vt_rows, tm), col)],
        out_shape=[jax.ShapeDtypeStruct((n, w), dt) for w, dt in outs]
                  + [jax.ShapeDtypeStruct((vt_rows, n), bf16)],
        scratch_shapes=[pltpu.VMEM((POOL_HALO, POOL_DIM), jnp.float32)],
        compiler_params=pltpu.CompilerParams(
            dimension_semantics=("arbitrary", "arbitrary"), vmem_limit_bytes=VMEM_LIMIT),
        name="in_proj",
    )(x2, pos2, *consts)


def _swa_kernel(sink_ref, q_ref, kc_ref, kp_ref, vc_ref, vp_ref, bias_ref, o_ref):
    i = pl.program_id(1)
    ts = q_ref.shape[0]
    nwin = ts // WINDOW
    bf16 = jnp.bfloat16
    q = q_ref[...]
    kext = jnp.concatenate([kp_ref[...], kc_ref[...]], axis=0)
    vext = jnp.concatenate([vp_ref[...], vc_ref[...]], axis=0)
    col = lax.broadcasted_iota(jnp.int32, (1, 2 * WINDOW), 1)
    no_prev = jnp.logical_and(i == 0, col < WINDOW)
    ones = jnp.ones((2 * WINDOW, LANE), bf16)
    heads = range(SWA_HEADS)
    sinks = [sink_ref[h] * LOG2E for h in heads]

    def scores(c, h):
        g = h // SWA_GROUP
        qh = q[c * WINDOW:(c + 1) * WINDOW, h * HEAD_DIM:(h + 1) * HEAD_DIM]
        kb = kext[c * WINDOW:(c + 2) * WINDOW, g * HEAD_DIM:(g + 1) * HEAD_DIM]
        s = _dot_nt(qh, kb) + bias_ref[h]
        if c == 0:
            s = jnp.where(no_prev, NEG, s)
        return s

    pending = [scores(0, h) for h in heads]
    for c in range(nwin):
        s_all = pending
        ms = [jnp.maximum(jnp.max(s_all[h], axis=-1, keepdims=True), sinks[h]) for h in heads]
        pending = [scores(c + 1, h) for h in heads] if c + 1 < nwin else []
        ps = [jnp.exp2(s_all[h] - ms[h]).astype(bf16) for h in heads]
        vaug = jnp.concatenate([vext[c * WINDOW:(c + 2) * WINDOW], ones], axis=1)
        os_ = [_dot(ps[h], vaug) for h in heads]
        outs = []
        for h in heads:
            lo = (h // SWA_GROUP) * HEAD_DIM
            denom = os_[h][:, SWA_KV_DIM + lo:SWA_KV_DIM + lo + HEAD_DIM] + jnp.exp2(sinks[h] - ms[h])
            outs.append(os_[h][:, lo:lo + HEAD_DIM] / denom)
        o_ref[c * WINDOW:(c + 1) * WINDOW, :] = jnp.concatenate(outs, axis=1).astype(bf16)


def _swa(sinks, qs, ks, vs, bias, batch, seq):
    n = qs.shape[0]
    ts = min(SWA_TILE, seq)
    nt = seq // ts
    wpt = ts // WINDOW
    wps = seq // WINDOW
    row = lambda b, i: (b * nt + i, 0)
    prev = lambda b, i: (b * wps + jnp.maximum(i * wpt - 1, 0), 0)
    return pl.pallas_call(
        _swa_kernel,
        grid=(batch, nt),
        in_specs=[pl.BlockSpec(memory_space=pltpu.SMEM),
                  pl.BlockSpec((ts, SWA_Q_DIM), row),
                  pl.BlockSpec((ts, SWA_KV_DIM), row), pl.BlockSpec((WINDOW, SWA_KV_DIM), prev),
                  pl.BlockSpec((ts, SWA_KV_DIM), row), pl.BlockSpec((WINDOW, SWA_KV_DIM), prev),
                  _const_spec(bias.shape)],
        out_specs=pl.BlockSpec((ts, SWA_Q_DIM), row),
        out_shape=jax.ShapeDtypeStruct((n, SWA_Q_DIM), jnp.bfloat16),
        compiler_params=pltpu.CompilerParams(
            dimension_semantics=("arbitrary", "arbitrary"), vmem_limit_bytes=VMEM_LIMIT),
        name="swa",
    )(sinks, qs, ks, ks, vs, vs, bias)


def _mla_kernel(q_ref, k_ref, vt_ref, o_ref, m_ref, acc_ref, sa_ref, sb_ref, ma_ref, mb_ref):
    i = pl.program_id(2)
    tq = q_ref.shape[0]
    blk = max(MLA_BLK, tq)
    ratio = blk // tq
    bf16 = jnp.bfloat16
    chains = [(a, c) for a in range(2) for c in range(tq // MLA_SLAB)]
    m_ref[...] = jnp.full(m_ref.shape, NEG, jnp.float32)
    acc_ref[...] = jnp.zeros(acc_ref.shape, jnp.float32)

    def score(a, c, start, nk):
        lanes = slice(a * LANE, (a + 1) * LANE)
        return _dot_nt(k_ref[pl.ds(start, nk), lanes],
                       q_ref[c * MLA_SLAB:(c + 1) * MLA_SLAB, lanes])

    def full_score(ch, j, s_ref, mx_ref):
        a, c = chains[ch]
        s = score(a, c, pl.multiple_of(j * blk, blk), blk)
        s_ref[ch] = s
        mx_ref[ch] = jnp.max(s, axis=0, keepdims=True)

    def tail_score(ch, start, nfree):
        a, c = chains[ch]
        nk = nfree + (c + 1) * MLA_SLAB
        s = score(a, c, start, nk)
        row = lax.broadcasted_iota(jnp.int32, (nk, MLA_SLAB), 0)
        col = lax.broadcasted_iota(jnp.int32, (nk, MLA_SLAB), 1) + (c * MLA_SLAB + nfree)
        s = jnp.where(row <= col, s, NEG)
        return s, jnp.max(s, axis=0, keepdims=True)

    def soft_pv(ch, s, mx, start):
        a, c = chains[ch]
        cols = slice(c * MLA_SLAB, (c + 1) * MLA_SLAB)
        m_prev = m_ref[a, :, cols]
        m_new = jnp.maximum(m_prev, mx)
        alpha = jnp.exp2(m_prev - m_new)
        p = jnp.exp2(s - m_new).astype(bf16)
        vt = vt_ref[a * MLA_V_ROWS:(a + 1) * MLA_V_ROWS, pl.ds(start, s.shape[0])]
        acc_ref[a, :, cols] = alpha * acc_ref[a, :, cols] + _dot(vt, p)
        m_ref[a, :, cols] = m_new

    bufs = ((sa_ref, ma_ref), (sb_ref, mb_ref))
    nchain = len(chains)

    def pipe_step(j, cur, nxt):
        s_cur, mx_cur = bufs[cur]
        full_score(0, j + 1, *bufs[nxt])
        for ch in range(nchain):
            if ch + 1 < nchain:
                full_score(ch + 1, j + 1, *bufs[nxt])
            soft_pv(ch, s_cur[ch], mx_cur[ch], pl.multiple_of(j * blk, blk))

    nfull = i // ratio

    @pl.when(nfull > 0)
    def _():
        for ch in range(nchain):
            full_score(ch, 0, *bufs[0])

    def pair_body(t, carry):
        pipe_step(2 * t, 0, 1)
        pipe_step(2 * t + 1, 1, 0)
        return carry

    lax.fori_loop(0, jnp.maximum(nfull - 1, 0) // 2, pair_body, 0)

    def finish(has_full, last, nfree):
        if has_full and last == 1:
            pipe_step(nfull - 2, 0, 1)
        tail_start = pl.multiple_of(nfull * blk, blk)
        tail = [tail_score(0, tail_start, nfree)]
        for ch in range(nchain):
            if ch + 1 < nchain:
                tail.append(tail_score(ch + 1, tail_start, nfree))
            if has_full:
                s_last, mx_last = bufs[last]
                soft_pv(ch, s_last[ch], mx_last[ch], pl.multiple_of((nfull - 1) * blk, blk))
        for ch in range(nchain):
            soft_pv(ch, *tail[ch], tail_start)

    for r in range(ratio):
        nfree = r * tq
        is_odd = (i % ratio) == r
        pl.when(jnp.logical_and(nfull == 0, is_odd))(functools.partial(finish, False, 0, nfree))
        for last in (0, 1):
            cond = jnp.logical_and(jnp.logical_and(nfull > 0, (nfull - 1) % 2 == last), is_odd)
            pl.when(cond)(functools.partial(finish, True, last, nfree))

    outs = []
    for a in range(2):
        acc = acc_ref[a]
        outs.append(acc[0:V_HEAD] / acc[V_HEAD:V_HEAD + 1])
    o_ref[...] = jnp.concatenate(outs, axis=0).T.astype(bf16)


def _mla(qm, km, vm, batch, seq):
    n = qm.shape[0]
    tq = min(MLA_TQ, seq)
    nq = seq // tq
    pairs = MLA_HEADS // 2
    nchain = 2 * (tq // MLA_SLAB)
    return pl.pallas_call(
        _mla_kernel,
        grid=(batch, pairs, nq),
        in_specs=[pl.BlockSpec((tq, 2 * LANE), lambda b, p, i: (b * nq + i, p)),
                  pl.BlockSpec((seq, 2 * LANE), lambda b, p, i: (b, p)),
                  pl.BlockSpec((2 * MLA_V_ROWS, seq), lambda b, p, i: (p, b))],
        out_specs=pl.BlockSpec((tq, LANE), lambda b, p, i: (b * nq + i, p)),
        out_shape=jax.ShapeDtypeStruct((n, MLA_OUT_DIM), jnp.bfloat16),
        scratch_shapes=[pltpu.VMEM((2, 1, tq), jnp.float32),
                        pltpu.VMEM((2, MLA_V_ROWS, tq), jnp.float32)]
                       + [pltpu.VMEM((nchain, max(MLA_BLK, tq), MLA_SLAB), jnp.float32)] * 2
                       + [pltpu.VMEM((nchain, 1, MLA_SLAB), jnp.float32)] * 2,
        compiler_params=pltpu.CompilerParams(
            dimension_semantics=("arbitrary", "arbitrary", "arbitrary"),
            vmem_limit_bytes=VMEM_LIMIT),
        name="mla",
    )(qm, km, vm)


def _mla_bounded_kernel(q_ref, k_ref, vt_ref, o_ref, acc_ref):
    i = pl.program_id(2)
    tq = q_ref.shape[0]
    blk = tq
    bf16 = jnp.bfloat16
    chains = [(a, c) for a in range(2) for c in range(tq // MLA_SLAB)]
    nchain = len(chains)
    acc_ref[...] = jnp.zeros(acc_ref.shape, jnp.float32)

    def score(ch, start, nk):
        a, c = chains[ch]
        lanes = slice(a * LANE, (a + 1) * LANE)
        return _dot_nt(k_ref[pl.ds(start, nk), lanes],
                       q_ref[c * MLA_SLAB:(c + 1) * MLA_SLAB, lanes])

    def weigh(ch, s, start):
        a, c = chains[ch]
        cols = slice(c * MLA_SLAB, (c + 1) * MLA_SLAB)
        p = jnp.exp2(s).astype(bf16)
        vt = vt_ref[a * MLA_V_ROWS:(a + 1) * MLA_V_ROWS, pl.ds(start, s.shape[0])]
        acc_ref[a, :, cols] += _dot(vt, p)

    def blocks(starts, masked_last):
        work = [(ch, st, masked_last and b == len(starts) - 1)
                for b, st in enumerate(starts) for ch in range(nchain)]

        def chain_scores(ch, start, masked):
            if not masked:
                return score(ch, start, blk)
            c = chains[ch][1]
            nk = (c + 1) * MLA_SLAB
            s = score(ch, start, nk)
            row = lax.broadcasted_iota(jnp.int32, (nk, MLA_SLAB), 0)
            col = lax.broadcasted_iota(jnp.int32, (nk, MLA_SLAB), 1) + c * MLA_SLAB
            return jnp.where(row <= col, s, NEG)

        ahead = min(MLA_AHEAD, len(work))
        pend = [chain_scores(*work[w]) for w in range(ahead)]
        for w in range(len(work)):
            if w + ahead < len(work):
                pend.append(chain_scores(*work[w + ahead]))
            weigh(work[w][0], pend.pop(0), work[w][1])

    nb = MLA_UNROLL

    def body(t, carry):
        st = pl.multiple_of(nb * t * blk, blk)
        blocks([st + k * blk for k in range(nb)], False)
        return carry

    lax.fori_loop(0, i // nb, body, 0)
    for r in range(nb):
        def tail(r=r):
            first = pl.multiple_of((i - r) * blk, blk)
            blocks([first + k * blk for k in range(r + 1)], True)
        pl.when(i % nb == r)(tail)

    outs = []
    for a in range(2):
        acc = acc_ref[a]
        outs.append(acc[0:V_HEAD] / acc[V_HEAD:V_HEAD + 1])
    o_ref[...] = jnp.concatenate(outs, axis=0).T.astype(bf16)


def _mla_bounded(qm, km, vm, batch, seq):
    n = qm.shape[0]
    tq = min(MLA_TQ, seq)
    nq = seq // tq
    pairs = MLA_HEADS // 2
    return pl.pallas_call(
        _mla_bounded_kernel,
        grid=(batch, pairs, nq),
        in_specs=[pl.BlockSpec((tq, 2 * LANE), lambda b, p, i: (b * nq + i, p)),
                  pl.BlockSpec((seq, 2 * LANE), lambda b, p, i: (b, p)),
                  pl.BlockSpec((2 * MLA_V_ROWS, seq), lambda b, p, i: (p, b))],
        out_specs=pl.BlockSpec((tq, LANE), lambda b, p, i: (b * nq + i, p)),
        out_shape=jax.ShapeDtypeStruct((n, MLA_OUT_DIM), jnp.bfloat16),
        scratch_shapes=[pltpu.VMEM((2, MLA_V_ROWS, tq), jnp.float32)],
        compiler_params=pltpu.CompilerParams(
            dimension_semantics=("arbitrary", "arbitrary", "arbitrary"),
            vmem_limit_bytes=VMEM_LIMIT),
        name="mla_bounded",
    )(qm, km, vm)


def _mla_score_bound(q_nope_gain, q_rope_gain, k_nope_gain, k_rope_gain):
    gmax = lambda g: jnp.max(jnp.abs(g.astype(jnp.float32)), axis=-1)
    dots = (QK_NOPE * gmax(q_nope_gain) * gmax(k_nope_gain)
            + QK_ROPE * gmax(q_rope_gain) * gmax(k_rope_gain))
    return dots * (QK_HEAD ** -0.5 * LOG2E * MLA_BOUND_MARGIN)


def _out_ffn_kernel(x_ref, a_ref, b_ref, c_ref, wo_ref, g_ref, wg_ref, wu_ref, wd_ref, o_ref):
    bf16 = jnp.bfloat16
    mixed = jnp.concatenate([a_ref[...], b_ref[...], c_ref[...]], axis=1)
    x1 = x_ref[...] + _dot(mixed, wo_ref[...])
    h = (_rms(x1, D_MODEL) * g_ref[...]).astype(bf16)
    acc = x1
    for start, size in FFN_CHUNKS:
        gate = _dot(h, wg_ref[:, start:start + size])
        up = _dot(h, wu_ref[:, start:start + size])
        act = (gate * jax.nn.sigmoid(gate) * up).astype(bf16)
        acc = acc + _dot(act, wd_ref[start:start + size, :])
    o_ref[...] = acc


def _out_ffn(x2, oa, ob, oc, l, wo, g, wg, wu, wd):
    n = x2.shape[0]
    tm = min(FFN_TILE, n)
    row = lambda i: (i, 0)
    return pl.pallas_call(
        _out_ffn_kernel,
        grid=(n // tm,),
        in_specs=[pl.BlockSpec((tm, D_MODEL), row), pl.BlockSpec((tm, SWA_Q_DIM), row),
                  pl.BlockSpec((tm, POOL_DIM), row), pl.BlockSpec((tm, MLA_OUT_DIM), row)]
                 + [_layer_spec(w.shape, l, single_buffer=True) for w in (wo, g, wg, wu, wd)],
        out_specs=pl.BlockSpec((tm, D_MODEL), row),
        out_shape=jax.ShapeDtypeStruct((n, D_MODEL), jnp.float32),
        compiler_params=pltpu.CompilerParams(
            dimension_semantics=("arbitrary",), vmem_limit_bytes=VMEM_LIMIT),
        name="out_ffn",
    )(x2, oa, ob, oc, wo, g, wg, wu, wd)


def _block_diag_ones(groups, width):
    g = np.zeros((width, width), np.float32)
    for lo, hi in groups:
        g[lo:hi, lo:hi] = 1.0
    return jnp.asarray(g, jnp.bfloat16)


def _lane_rows(rows):
    rows = [r.astype(jnp.float32) for r in rows]
    rows += [jnp.zeros_like(rows[0])] * (8 - len(rows))
    return jnp.stack(rows, axis=1)


def _head_lanes(nope, rope):
    pad = jnp.zeros(nope.shape[:-1] + (MLA_HEAD_PAD - QK_HEAD,), jnp.float32)
    one = jnp.concatenate([nope.astype(jnp.float32), rope.astype(jnp.float32), pad], axis=-1)
    return jnp.tile(one, (1,) * (one.ndim - 1) + (MLA_HEADS,))


def _pack_params(attn_norm, w_in, swa_q_gain, swa_k_gain, pool_w, pool_scale, mla_q_a_gain,
                 mla_w_qb, mla_kv_a_gain, mla_w_kvb, mla_q_nope_gain, mla_q_rope_gain,
                 mla_k_nope_gain, mla_k_rope_gain):
    f32, bf16 = jnp.float32, jnp.bfloat16
    depth = w_in.shape[0]
    zeros = lambda c: jnp.zeros((depth, D_MODEL, c), f32)
    w1 = jnp.concatenate([w_in[:, :, :C_KR], zeros(QK_NOPE), w_in[:, :, C_KR:],
                          zeros(LANE - QK_HEAD)], axis=2).astype(bf16)
    wqb = mla_w_qb.reshape(depth, Q_LORA, MLA_HEADS, QK_HEAD)
    wqb = jnp.pad(wqb, ((0, 0), (0, 0), (0, 0), (0, MLA_HEAD_PAD - QK_HEAD)))
    wqb = wqb.reshape(depth, Q_LORA, MLA_PAD_DIM).astype(bf16)
    wkvb = mla_w_kvb.reshape(depth, KV_LORA, MLA_HEADS, QK_NOPE + V_HEAD)
    wk = jnp.pad(wkvb[..., :QK_NOPE], ((0, 0), (0, 0), (0, 0), (0, MLA_HEAD_PAD - QK_NOPE)))
    wkvb_p = jnp.concatenate([wk.reshape(depth, KV_LORA, MLA_PAD_DIM),
                              wkvb[..., QK_NOPE:].reshape(depth, KV_LORA, MLA_OUT_DIM)],
                             axis=2).astype(bf16)
    ngroup = len(POOL_WINDOWS)
    eye = jnp.eye(ngroup, dtype=f32)[None, :, None, :, None]
    wp = (eye * pool_w[:, :, :, None, :]).reshape(depth, POOL_DIM, POOL_DIM).astype(bf16)
    ones_n, ones_r = jnp.ones((depth, QK_NOPE), f32), jnp.ones((depth, QK_ROPE), f32)
    half = jnp.concatenate([jnp.zeros((depth, ROPE_HALF), f32), jnp.ones((depth, ROPE_HALF), f32)],
                           axis=1)
    v768 = _lane_rows([
        _head_lanes(mla_q_nope_gain, mla_q_rope_gain) * (QK_HEAD ** -0.5 * LOG2E),
        _head_lanes(ones_n / QK_NOPE, ones_r / QK_ROPE),
        _head_lanes(mla_k_nope_gain, 0.0 * ones_r),
        _head_lanes(0.0 * ones_n, half),
        _head_lanes(0.0 * ones_n, half - 1.0),
    ])
    v384 = (jnp.tile(swa_q_gain, (1, SWA_HEADS)) * (HEAD_DIM ** -0.5 * LOG2E))[:, None, :]
    windows = jnp.broadcast_to(jnp.repeat(jnp.asarray(POOL_WINDOWS, f32), POOL_GROUP_DIM),
                               (depth, POOL_DIM))
    v256 = _lane_rows([mla_q_a_gain, pool_scale, windows])
    v128 = _lane_rows([jnp.tile(swa_k_gain, (1, SWA_KV_HEADS)), mla_kv_a_gain,
                       _head_lanes(0.0 * ones_n, mla_k_rope_gain)[:, :LANE]])
    stacked = (attn_norm[:, None, :], w1, v768, v384, v256, v128, wp, wqb, wkvb_p)

    inv_freq = ROPE_THETA ** (-jnp.arange(0, QK_ROPE, 2, dtype=f32) / QK_ROPE)
    g384 = _block_diag_ones([(h * HEAD_DIM, (h + 1) * HEAD_DIM) for h in range(SWA_HEADS)],
                            SWA_Q_DIM)
    gsum = np.zeros((MLA_PAD_DIM, LANE), np.float32)
    for h in range(MLA_HEADS):
        base = h * MLA_HEAD_PAD
        gsum[base:base + QK_NOPE, 2 * h] = 1.0
        gsum[base + QK_NOPE:base + QK_HEAD, 2 * h + 1] = 1.0
    shared = (g384, jnp.asarray(gsum, bf16), jnp.asarray(gsum.T, bf16),
              inv_freq.reshape(ROPE_HALF, 1))
    return stacked, shared


def kernel(x, positions, rel_bias, attn_norm, w_in, swa_q_gain, swa_k_gain, swa_sinks, pool_w,
           pool_scale, mla_q_a_gain, mla_w_qb, mla_kv_a_gain, mla_w_kvb, mla_q_nope_gain,
           mla_q_rope_gain, mla_k_nope_gain, mla_k_rope_gain, w_out, ffn_norm, w_gate, w_up,
           w_down):
    batch, seq, _ = x.shape
    n = batch * seq
    depth = w_in.shape[0]
    bf16 = jnp.bfloat16
    x2 = x.reshape(n, D_MODEL)
    pos2 = positions.astype(jnp.float32).reshape(1, n)
    bias = _bias_table(rel_bias)
    stacked, shared = _pack_params(attn_norm, w_in, swa_q_gain, swa_k_gain, pool_w, pool_scale,
                                   mla_q_a_gain, mla_w_qb, mla_kv_a_gain, mla_w_kvb,
                                   mla_q_nope_gain, mla_q_rope_gain, mla_k_nope_gain,
                                   mla_k_rope_gain)
    ffn = (w_out.astype(bf16), ffn_norm[:, None, :], w_gate.astype(bf16), w_up.astype(bf16),
           w_down.astype(bf16))
    score_bound = _mla_score_bound(mla_q_nope_gain, mla_q_rope_gain, mla_k_nope_gain,
                                   mla_k_rope_gain)
    for l in range(depth):
        qs, ks, vs, pool_o, qm, km, vm = _in_proj(x2, pos2, l, stacked, shared, batch, seq)
        swa_o = _swa(swa_sinks[l], qs, ks, vs, bias, batch, seq)
        mla_o = lax.cond(score_bound[l] <= MLA_MAX_BOUND,
                         functools.partial(_mla_bounded, batch=batch, seq=seq),
                         functools.partial(_mla, batch=batch, seq=seq), qm, km, vm)
        x2 = _out_ffn(x2, swa_o, pool_o, mla_o, l, *ffn)
    return x2.reshape(batch, seq, D_MODEL)
```

```python
import functools
import math

import numpy as np
import jax
import jax.numpy as jnp
from jax import lax
from jax.experimental import pallas as pl
from jax.experimental.pallas import tpu as pltpu

D_MODEL = 1024
HEAD_DIM = 64
SWA_HEADS = 6
SWA_KV_HEADS = 2
SWA_GROUP = SWA_HEADS // SWA_KV_HEADS
WINDOW = 128
POOL_WINDOWS = (2, 4, 8, 16)
POOL_GROUP_DIM = 64
POOL_DIM = POOL_GROUP_DIM * len(POOL_WINDOWS)
MLA_HEADS = 6
Q_LORA = 256
KV_LORA = 128
QK_NOPE = 64
QK_ROPE = 32
V_HEAD = 64
QK_HEAD = QK_NOPE + QK_ROPE
ROPE_THETA = 10000.0
N_BUCKETS = 32
MAX_DISTANCE = 128
SWA_Q_DIM = SWA_HEADS * HEAD_DIM
SWA_KV_DIM = SWA_KV_HEADS * HEAD_DIM
MLA_OUT_DIM = MLA_HEADS * V_HEAD
D_FF = 2816
EPS = 1e-6

LANE = 128
MLA_HEAD_PAD = LANE
MLA_PAD_DIM = MLA_HEADS * MLA_HEAD_PAD
ROPE_HALF = QK_ROPE // 2
MLA_V_ROWS = 80
POOL_HALO = max(POOL_WINDOWS)
NEG = -1e30
LOG2E = math.log2(math.e)

C_QA = 0
C_KA = C_QA + SWA_Q_DIM
C_VA = C_KA + SWA_KV_DIM
C_UB = C_VA + SWA_KV_DIM
C_CQ = C_UB + POOL_DIM
C_CKV = C_CQ + Q_LORA
C_KR = C_CKV + KV_LORA
IN_PAD_DIM = C_KR + LANE

IN_TILE = 1024
IN_SPLIT = 2
SWA_TILE = 1024
MLA_TQ = 1024
MLA_BLK = 1024
MLA_SLAB = 256
MLA_UNROLL = 4
MLA_AHEAD = 2
MLA_MAX_BOUND = 48.0
MLA_BOUND_MARGIN = 1.02
FFN_TILE = 1024
FFN_CHUNKS = ((0, 1536), (1536, 1280))
VMEM_LIMIT = 56 * 1024 * 1024


def _dot(a, b):
    return jnp.dot(a, b, preferred_element_type=jnp.float32)


def _dot_nt(a, b):
    return lax.dot_general(a, b, (((1,), (1,)), ((), ())), preferred_element_type=jnp.float32)


def _rms(x, width):
    return x * lax.rsqrt(jnp.sum(x * x, axis=-1, keepdims=True) * (1.0 / width) + EPS)


def _bias_kernel(rel_ref, bucket_ref, out_ref):
    bucket = bucket_ref[...]
    for h in range(SWA_HEADS):
        acc = jnp.full(bucket.shape, NEG, jnp.float32)
        for b in range(N_BUCKETS):
            acc = jnp.where(bucket == b, rel_ref[b, h] * LOG2E, acc)
        out_ref[h] = acc


def _band_buckets():
    q_loc = np.arange(WINDOW)[:, None]
    k_loc = np.arange(2 * WINDOW)[None, :]
    dist = q_loc + WINDOW - k_loc
    band_ok = (dist >= 0) & (dist < WINDOW)
    n = np.maximum(dist, 0)
    max_exact = N_BUCKETS // 2
    nf = np.maximum(n, 1).astype(np.float32)
    large = max_exact + (np.log(nf / max_exact) / math.log(MAX_DISTANCE / max_exact)
                         * (N_BUCKETS - max_exact)).astype(np.int32)
    large = np.minimum(large, N_BUCKETS - 1)
    bucket = np.where(n < max_exact, n, large)
    return np.where(band_ok, bucket, -1).astype(np.int32)


def _bias_table(rel_bias):
    return pl.pallas_call(
        _bias_kernel,
        out_shape=jax.ShapeDtypeStruct((SWA_HEADS, WINDOW, 2 * WINDOW), jnp.float32),
        in_specs=[pl.BlockSpec(memory_space=pltpu.SMEM),
                  pl.BlockSpec(memory_space=pltpu.VMEM)],
        out_specs=pl.BlockSpec(memory_space=pltpu.VMEM),
        name="bias_table",
    )(rel_bias, jnp.asarray(_band_buckets()))


def _in_proj_kernel(x_ref, pos_ref, g_attn_ref, w1_ref, g384_ref, gsum_ref, gexp_ref, v768_ref,
                    v384_ref, v256_ref, v128_ref, invf_ref, wp_ref, wqb_ref, wkvb_ref,
                    qs_ref, ks_ref, vs_ref, pool_ref, qm_ref, km_ref, vm_ref, carry_ref):
    i = pl.program_id(1)
    tm = x_ref.shape[0]
    bf16 = jnp.bfloat16

    x = x_ref[...]
    h = (_rms(x, D_MODEL) * g_attn_ref[...]).astype(bf16)
    bounds = [(r * tm // IN_SPLIT, (r + 1) * tm // IN_SPLIT) for r in range(IN_SPLIT)]
    projs = [_dot(h[r0:r1], w1_ref[...]) for r0, r1 in bounds]

    m1 = v768_ref[3:4, :]
    m2 = v768_ref[4:5, :]

    def rope(xn, cos_t, sin_t, m1_t, m2_t):
        width = xn.shape[1]
        rot = pltpu.roll(xn, ROPE_HALF, 1) * m1_t + pltpu.roll(xn, width - ROPE_HALF, 1) * m2_t
        return xn * cos_t + rot * sin_t

    def group_sums(v):
        compact = _dot((v * v).astype(bf16), gsum_ref[...])
        return _dot(compact.astype(bf16), gexp_ref[...])

    def prepare(proj, r0, r1):
        rows = r1 - r0
        qa = proj[:, C_QA:C_KA]
        ss = _dot((qa * qa).astype(bf16), g384_ref[...])
        qs_ref[r0:r1, :] = (qa * lax.rsqrt(ss * (1.0 / HEAD_DIM) + EPS) * v384_ref[...]).astype(bf16)
        ka = proj[:, C_KA:C_VA]
        ss = _dot((ka * ka).astype(bf16), g384_ref[0:SWA_KV_DIM, 0:SWA_KV_DIM])
        ks_ref[r0:r1, :] = (ka * lax.rsqrt(ss * (1.0 / HEAD_DIM) + EPS)
                            * v128_ref[0:1, :]).astype(bf16)
        vs_ref[r0:r1, :] = proj[:, C_VA:C_UB].astype(bf16)

        ang_t = invf_ref[...] * pos_ref[:, r0:r1]
        cos_t, sin_t = jnp.cos(ang_t), jnp.sin(ang_t)
        one_t = jnp.ones((QK_NOPE, rows), jnp.float32)
        zero_t = jnp.zeros((QK_NOPE, rows), jnp.float32)
        tail = LANE - QK_HEAD
        cos1 = jnp.concatenate([one_t, cos_t, cos_t, one_t[0:tail]], axis=0).T
        sin1 = jnp.concatenate([zero_t, sin_t, sin_t, zero_t[0:tail]], axis=0).T

        cq = proj[:, C_CQ:C_CKV]
        cqn = (_rms(cq, Q_LORA) * v256_ref[0:1, :]).astype(bf16)
        q = _dot(cqn, wqb_ref[...])
        qn = q * lax.rsqrt(group_sums(q) * v768_ref[1:2, :] + EPS) * v768_ref[0:1, :]
        cos6 = jnp.concatenate([cos1] * MLA_HEADS, axis=1)
        sin6 = jnp.concatenate([sin1] * MLA_HEADS, axis=1)
        qm_ref[r0:r1, :] = (rope(qn, cos6, sin6, m1, m2) + v768_ref[5:6, :]).astype(bf16)

        kr = proj[:, C_KR:IN_PAD_DIM]
        krn = _rms(kr, QK_ROPE) * v128_ref[2:3, :]
        krr = rope(krn, cos1, sin1, m1[:, 0:LANE], m2[:, 0:LANE])
        ckv = proj[:, C_CKV:C_KR]
        ckvn = (_rms(ckv, KV_LORA) * v128_ref[1:2, :]).astype(bf16)
        kv = _dot(ckvn, wkvb_ref[...])
        kn = kv[:, 0:MLA_PAD_DIM]
        kn = kn * lax.rsqrt(group_sums(kn) * (1.0 / QK_NOPE) + EPS) * v768_ref[2:3, :]
        km_ref[r0:r1, :] = (kn + jnp.concatenate([krr] * MLA_HEADS, axis=1)
                            + v768_ref[6:7, :]).astype(bf16)
        vt = kv[:, MLA_PAD_DIM:].T
        fill = (lax.broadcasted_iota(jnp.int32, (MLA_V_ROWS - V_HEAD, rows), 0) == 0
                ).astype(jnp.float32)
        pieces = []
        for hd in range(MLA_HEADS):
            pieces += [vt[hd * V_HEAD:(hd + 1) * V_HEAD], fill]
        vm_ref[:, r0:r1] = jnp.concatenate(pieces, axis=0).astype(bf16)

    for (r0, r1), proj in zip(bounds, projs):
        prepare(proj, r0, r1)

    u = jnp.concatenate([proj[:, C_UB:C_CQ] for proj in projs], axis=0)

    @pl.when(i == 0)
    def _():
        carry_ref[...] = jnp.zeros_like(carry_ref)

    uext = jnp.concatenate([carry_ref[...], u], axis=0)
    a1 = uext + pltpu.roll(uext, 1, 0)
    a2 = a1 + pltpu.roll(a1, 2, 0)
    a3 = a2 + pltpu.roll(a2, 4, 0)
    a4 = a3 + pltpu.roll(a3, 8, 0)
    lane = lax.broadcasted_iota(jnp.int32, (1, POOL_DIM), 1)
    wsum = jnp.where(lane < POOL_GROUP_DIM, a1,
                     jnp.where(lane < 2 * POOL_GROUP_DIM, a2,
                               jnp.where(lane < 3 * POOL_GROUP_DIM, a3, a4)))[POOL_HALO:]
    t = (i * tm + lax.broadcasted_iota(jnp.int32, (tm, 1), 0) + 1).astype(jnp.float32)
    count = jnp.minimum(t, v256_ref[2:3, :])
    d = wsum / count - u
    pool_ref[...] = (_dot(d.astype(bf16), wp_ref[...]) * v256_ref[1:2, :]).astype(bf16)
    carry_ref[...] = u[tm - POOL_HALO:, :]


def _const_spec(shape):
    nd = len(shape)
    return pl.BlockSpec(shape, lambda *_: (0,) * nd)


def _layer_spec(shape, l, single_buffer=False):
    nd = len(shape)
    mode = dict(pipeline_mode=pl.Buffered(1)) if single_buffer else {}
    return pl.BlockSpec((None,) + tuple(shape[1:]), lambda *_: (l,) + (0,) * (nd - 1), **mode)


def _in_proj(x2, pos2, l, stacked, shared, batch, seq):
    g_attn, w1, v768, v384, v256, v128, wp, wqb, wkvb = stacked
    g384, gsum, gexp, invf = shared
    consts = [g_attn, w1, g384, gsum, gexp, v768, v384, v256, v128, invf, wp, wqb, wkvb]
    is_stacked = [True, True, False, False, False, True, True, True, True, False, True, True, True]
    const_specs = [_layer_spec(c.shape, l) if st else _const_spec(c.shape)
                   for c, st in zip(consts, is_stacked)]
    n = x2.shape[0]
    tm = min(IN_TILE, seq)
    nt = seq // tm
    row = lambda b, i: (b * nt + i, 0)
    bf16 = jnp.bfloat16
    col = lambda b, i: (0, b * nt + i)
    outs = [(SWA_Q_DIM, bf16), (SWA_KV_DIM, bf16), (SWA_KV_DIM, bf16), (POOL_DIM, bf16),
            (MLA_PAD_DIM, bf16), (MLA_PAD_DIM, bf16)]
    vt_rows = MLA_HEADS * MLA_V_ROWS
    return pl.pallas_call(
        _in_proj_kernel,
        grid=(batch, nt),
        in_specs=[pl.BlockSpec((tm, D_MODEL), row), pl.BlockSpec((1, tm), col)]
                 + const_specs,
        out_specs=[pl.BlockSpec((tm, w), row) for w, _ in outs]
                  + [pl.BlockSpec((vt_rows, tm), col)],
        out_shape=[jax.ShapeDtypeStruct((n, w), dt) for w, dt in outs]
                  + [jax.ShapeDtypeStruct((vt_rows, n), bf16)],
        scratch_shapes=[pltpu.VMEM((POOL_HALO, POOL_DIM), jnp.float32)],
        compiler_params=pltpu.CompilerParams(
            dimension_semantics=("arbitrary", "arbitrary"), vmem_limit_bytes=VMEM_LIMIT),
        name="in_proj",
    )(x2, pos2, *consts)


def _swa_kernel(sink_ref, q_ref, kc_ref, kp_ref, vc_ref, vp_ref, bias_ref, o_ref):
    i = pl.program_id(1)
    ts = q_ref.shape[0]
    nwin = ts // WINDOW
    bf16 = jnp.bfloat16
    q = q_ref[...]
    kext = jnp.concatenate([kp_ref[...], kc_ref[...]], axis=0)
    vext = jnp.concatenate([vp_ref[...], vc_ref[...]], axis=0)
    col = lax.broadcasted_iota(jnp.int32, (1, 2 * WINDOW), 1)
    no_prev = jnp.logical_and(i == 0, col < WINDOW)
    ones = jnp.ones((2 * WINDOW, LANE), bf16)
    heads = range(SWA_HEADS)
    sinks = [sink_ref[h] * LOG2E for h in heads]

    def scores(c, h):
        g = h // SWA_GROUP
        qh = q[c * WINDOW:(c + 1) * WINDOW, h * HEAD_DIM:(h + 1) * HEAD_DIM]
        kb = kext[c * WINDOW:(c + 2) * WINDOW, g * HEAD_DIM:(g + 1) * HEAD_DIM]
        s = _dot_nt(qh, kb) + bias_ref[h]
        if c == 0:
            s = jnp.where(no_prev, NEG, s)
        return s

    pending = [scores(0, h) for h in heads]
    for c in range(nwin):
        s_all = pending
        ms = [jnp.maximum(jnp.max(s_all[h], axis=-1, keepdims=True), sinks[h]) for h in heads]
        pending = [scores(c + 1, h) for h in heads] if c + 1 < nwin else []
        ps = [jnp.exp2(s_all[h] - ms[h]).astype(bf16) for h in heads]
        vaug = jnp.concatenate([vext[c * WINDOW:(c + 2) * WINDOW], ones], axis=1)
        os_ = [_dot(ps[h], vaug) for h in heads]
        outs = []
        for h in heads:
            lo = (h // SWA_GROUP) * HEAD_DIM
            denom = os_[h][:, SWA_KV_DIM + lo:SWA_KV_DIM + lo + HEAD_DIM] + jnp.exp2(sinks[h] - ms[h])
            outs.append(os_[h][:, lo:lo + HEAD_DIM] / denom)
        o_ref[c * WINDOW:(c + 1) * WINDOW, :] = jnp.concatenate(outs, axis=1).astype(bf16)


def _swa(sinks, qs, ks, vs, bias, batch, seq):
    n = qs.shape[0]
    ts = min(SWA_TILE, seq)
    nt = seq // ts
    wpt = ts // WINDOW
    wps = seq // WINDOW
    row = lambda b, i: (b * nt + i, 0)
    prev = lambda b, i: (b * wps + jnp.maximum(i * wpt - 1, 0), 0)
    return pl.pallas_call(
        _swa_kernel,
        grid=(batch, nt),
        in_specs=[pl.BlockSpec(memory_space=pltpu.SMEM),
                  pl.BlockSpec((ts, SWA_Q_DIM), row),
                  pl.BlockSpec((ts, SWA_KV_DIM), row), pl.BlockSpec((WINDOW, SWA_KV_DIM), prev),
                  pl.BlockSpec((ts, SWA_KV_DIM), row), pl.BlockSpec((WINDOW, SWA_KV_DIM), prev),
                  _const_spec(bias.shape)],
        out_specs=pl.BlockSpec((ts, SWA_Q_DIM), row),
        out_shape=jax.ShapeDtypeStruct((n, SWA_Q_DIM), jnp.bfloat16),
        compiler_params=pltpu.CompilerParams(
            dimension_semantics=("arbitrary", "arbitrary"), vmem_limit_bytes=VMEM_LIMIT),
        name="swa",
    )(sinks, qs, ks, ks, vs, vs, bias)


def _mla_kernel(q_ref, k_ref, vt_ref, o_ref, m_ref, acc_ref, sa_ref, sb_ref, ma_ref, mb_ref):
    i = pl.program_id(2)
    tq = q_ref.shape[0]
    blk = max(MLA_BLK, tq)
    ratio = blk // tq
    bf16 = jnp.bfloat16
    chains = [(a, c) for a in range(2) for c in range(tq // MLA_SLAB)]
    m_ref[...] = jnp.full(m_ref.shape, NEG, jnp.float32)
    acc_ref[...] = jnp.zeros(acc_ref.shape, jnp.float32)

    def score(a, c, start, nk):
        lanes = slice(a * LANE, (a + 1) * LANE)
        return _dot_nt(k_ref[pl.ds(start, nk), lanes],
                       q_ref[c * MLA_SLAB:(c + 1) * MLA_SLAB, lanes])

    def full_score(ch, j, s_ref, mx_ref):
        a, c = chains[ch]
        s = score(a, c, pl.multiple_of(j * blk, blk), blk)
        s_ref[ch] = s
        mx_ref[ch] = jnp.max(s, axis=0, keepdims=True)

    def tail_score(ch, start, nfree):
        a, c = chains[ch]
        nk = nfree + (c + 1) * MLA_SLAB
        s = score(a, c, start, nk)
        row = lax.broadcasted_iota(jnp.int32, (nk, MLA_SLAB), 0)
        col = lax.broadcasted_iota(jnp.int32, (nk, MLA_SLAB), 1) + (c * MLA_SLAB + nfree)
        s = jnp.where(row <= col, s, NEG)
        return s, jnp.max(s, axis=0, keepdims=True)

    def soft_pv(ch, s, mx, start):
        a, c = chains[ch]
        cols = slice(c * MLA_SLAB, (c + 1) * MLA_SLAB)
        m_prev = m_ref[a, :, cols]
        m_new = jnp.maximum(m_prev, mx)
        alpha = jnp.exp2(m_prev - m_new)
        p = jnp.exp2(s - m_new).astype(bf16)
        vt = vt_ref[a * MLA_V_ROWS:(a + 1) * MLA_V_ROWS, pl.ds(start, s.shape[0])]
        acc_ref[a, :, cols] = alpha * acc_ref[a, :, cols] + _dot(vt, p)
        m_ref[a, :, cols] = m_new

    bufs = ((sa_ref, ma_ref), (sb_ref, mb_ref))
    nchain = len(chains)

    def pipe_step(j, cur, nxt):
        s_cur, mx_cur = bufs[cur]
        full_score(0, j + 1, *bufs[nxt])
        for ch in range(nchain):
            if ch + 1 < nchain:
                full_score(ch + 1, j + 1, *bufs[nxt])
            soft_pv(ch, s_cur[ch], mx_cur[ch], pl.multiple_of(j * blk, blk))

    nfull = i // ratio

    @pl.when(nfull > 0)
    def _():
        for ch in range(nchain):
            full_score(ch, 0, *bufs[0])

    def pair_body(t, carry):
        pipe_step(2 * t, 0, 1)
        pipe_step(2 * t + 1, 1, 0)
        return carry

    lax.fori_loop(0, jnp.maximum(nfull - 1, 0) // 2, pair_body, 0)

    def finish(has_full, last, nfree):
        if has_full and last == 1:
            pipe_step(nfull - 2, 0, 1)
        tail_start = pl.multiple_of(nfull * blk, blk)
        tail = [tail_score(0, tail_start, nfree)]
        for ch in range(nchain):
            if ch + 1 < nchain:
                tail.append(tail_score(ch + 1, tail_start, nfree))
            if has_full:
                s_last, mx_last = bufs[last]
                soft_pv(ch, s_last[ch], mx_last[ch], pl.multiple_of((nfull - 1) * blk, blk))
        for ch in range(nchain):
            soft_pv(ch, *tail[ch], tail_start)

    for r in range(ratio):
        nfree = r * tq
        is_odd = (i % ratio) == r
        pl.when(jnp.logical_and(nfull == 0, is_odd))(functools.partial(finish, False, 0, nfree))
        for last in (0, 1):
            cond = jnp.logical_and(jnp.logical_and(nfull > 0, (nfull - 1) % 2 == last), is_odd)
            pl.when(cond)(functools.partial(finish, True, last, nfree))

    outs = []
    for a in range(2):
        acc = acc_ref[a]
        outs.append(acc[0:V_HEAD] / acc[V_HEAD:V_HEAD + 1])
    o_ref[...] = jnp.concatenate(outs, axis=0).T.astype(bf16)


def _mla(qm, km, vm, batch, seq):
    n = qm.shape[0]
    tq = min(MLA_TQ, seq)
    nq = seq // tq
    pairs = MLA_HEADS // 2
    nchain = 2 * (tq // MLA_SLAB)
    return pl.pallas_call(
        _mla_kernel,
        grid=(batch, pairs, nq),
        in_specs=[pl.BlockSpec((tq, 2 * LANE), lambda b, p, i: (b * nq + i, p)),
                  pl.BlockSpec((seq, 2 * LANE), lambda b, p, i: (b, p)),
                  pl.BlockSpec((2 * MLA_V_ROWS, seq), lambda b, p, i: (p, b))],
        out_specs=pl.BlockSpec((tq, LANE), lambda b, p, i: (b * nq + i, p)),
        out_shape=jax.ShapeDtypeStruct((n, MLA_OUT_DIM), jnp.bfloat16),
        scratch_shapes=[pltpu.VMEM((2, 1, tq), jnp.float32),
                        pltpu.VMEM((2, MLA_V_ROWS, tq), jnp.float32)]
                       + [pltpu.VMEM((nchain, max(MLA_BLK, tq), MLA_SLAB), jnp.float32)] * 2
                       + [pltpu.VMEM((nchain, 1, MLA_SLAB), jnp.float32)] * 2,
        compiler_params=pltpu.CompilerParams(
            dimension_semantics=("arbitrary", "arbitrary", "arbitrary"),
            vmem_limit_bytes=VMEM_LIMIT),
        name="mla",
    )(qm, km, vm)


def _mla_bounded_kernel(q_ref, k_ref, vt_ref, o_ref, acc_ref):
    i = pl.program_id(2)
    tq = q_ref.shape[0]
    blk = tq
    bf16 = jnp.bfloat16
    chains = [(a, c) for a in range(2) for c in range(tq // MLA_SLAB)]
    nchain = len(chains)
    acc_ref[...] = jnp.zeros(acc_ref.shape, jnp.float32)

    def score(ch, start, nk):
        a, c = chains[ch]
        lanes = slice(a * LANE, (a + 1) * LANE)
        return _dot_nt(k_ref[pl.ds(start, nk), lanes],
                       q_ref[c * MLA_SLAB:(c + 1) * MLA_SLAB, lanes])

    def weigh(ch, s, start):
        a, c = chains[ch]
        cols = slice(c * MLA_SLAB, (c + 1) * MLA_SLAB)
        p = jnp.exp2(s).astype(bf16)
        vt = vt_ref[a * MLA_V_ROWS:(a + 1) * MLA_V_ROWS, pl.ds(start, s.shape[0])]
        acc_ref[a, :, cols] += _dot(vt, p)

    def blocks(starts, masked_last):
        work = [(ch, st, masked_last and b == len(starts) - 1)
                for b, st in enumerate(starts) for ch in range(nchain)]

        def chain_scores(ch, start, masked):
            if not masked:
                return score(ch, start, blk)
            c = chains[ch][1]
            nk = (c + 1) * MLA_SLAB
            s = score(ch, start, nk)
            row = lax.broadcasted_iota(jnp.int32, (nk, MLA_SLAB), 0)
            col = lax.broadcasted_iota(jnp.int32, (nk, MLA_SLAB), 1) + c * MLA_SLAB
            return jnp.where(row <= col, s, NEG)

        ahead = min(MLA_AHEAD, len(work))
        pend = [chain_scores(*work[w]) for w in range(ahead)]
        for w in range(len(work)):
            if w + ahead < len(work):
                pend.append(chain_scores(*work[w + ahead]))
            weigh(work[w][0], pend.pop(0), work[w][1])

    nb = MLA_UNROLL

    def body(t, carry):
        st = pl.multiple_of(nb * t * blk, blk)
        blocks([st + k * blk for k in range(nb)], False)
        return carry

    lax.fori_loop(0, i // nb, body, 0)
    for r in range(nb):
        def tail(r=r):
            first = pl.multiple_of((i - r) * blk, blk)
            blocks([first + k * blk for k in range(r + 1)], True)
        pl.when(i % nb == r)(tail)

    outs = []
    for a in range(2):
        acc = acc_ref[a]
        outs.append(acc[0:V_HEAD] / acc[V_HEAD:V_HEAD + 1])
    o_ref[...] = jnp.concatenate(outs, axis=0).T.astype(bf16)


def _mla_bounded(qm, km, vm, batch, seq):
    n = qm.shape[0]
    tq = min(MLA_TQ, seq)
    nq = seq // tq
    pairs = MLA_HEADS // 2
    return pl.pallas_call(
        _mla_bounded_kernel,
        grid=(batch, pairs, nq),
        in_specs=[pl.BlockSpec((tq, 2 * LANE), lambda b, p, i: (b * nq + i, p)),
                  pl.BlockSpec((seq, 2 * LANE), lambda b, p, i: (b, p)),
                  pl.BlockSpec((2 * MLA_V_ROWS, seq), lambda b, p, i: (p, b))],
        out_specs=pl.BlockSpec((tq, LANE), lambda b, p, i: (b * nq + i, p)),
        out_shape=jax.ShapeDtypeStruct((n, MLA_OUT_DIM), jnp.bfloat16),
        scratch_shapes=[pltpu.VMEM((2, MLA_V_ROWS, tq), jnp.float32)],
        compiler_params=pltpu.CompilerParams(
            dimension_semantics=("arbitrary", "arbitrary", "arbitrary"),
            vmem_limit_bytes=VMEM_LIMIT),
        name="mla_bounded",
    )(qm, km, vm)


def _mla_score_bound(q_nope_gain, q_rope_gain, k_nope_gain, k_rope_gain):
    gmax = lambda g: jnp.max(jnp.abs(g.astype(jnp.float32)), axis=-1)
    dots = (QK_NOPE * gmax(q_nope_gain) * gmax(k_nope_gain)
            + QK_ROPE * gmax(q_rope_gain) * gmax(k_rope_gain))
    return dots * (QK_HEAD ** -0.5 * LOG2E * MLA_BOUND_MARGIN)


def _out_ffn_kernel(x_ref, a_ref, b_ref, c_ref, wo_ref, g_ref, wg_ref, wu_ref, wd_ref, o_ref):
    bf16 = jnp.bfloat16
    mixed = jnp.concatenate([a_ref[...], b_ref[...], c_ref[...]], axis=1)
    x1 = x_ref[...] + _dot(mixed, wo_ref[...])
    h = (_rms(x1, D_MODEL) * g_ref[...]).astype(bf16)
    acc = x1
    for start, size in FFN_CHUNKS:
        gate = _dot(h, wg_ref[:, start:start + size])
        up = _dot(h, wu_ref[:, start:start + size])
        act = (gate * jax.nn.sigmoid(gate) * up).astype(bf16)
        acc = acc + _dot(act, wd_ref[start:start + size, :])
    o_ref[...] = acc


def _out_ffn(x2, oa, ob, oc, l, wo, g, wg, wu, wd):
    n = x2.shape[0]
    tm = min(FFN_TILE, n)
    row = lambda i: (i, 0)
    return pl.pallas_call(
        _out_ffn_kernel,
        grid=(n // tm,),
        in_specs=[pl.BlockSpec((tm, D_MODEL), row), pl.BlockSpec((tm, SWA_Q_DIM), row),
                  pl.BlockSpec((tm, POOL_DIM), row), pl.BlockSpec((tm, MLA_OUT_DIM), row)]
                 + [_layer_spec(w.shape, l, single_buffer=True) for w in (wo, g, wg, wu, wd)],
        out_specs=pl.BlockSpec((tm, D_MODEL), row),
        out_shape=jax.ShapeDtypeStruct((n, D_MODEL), jnp.float32),
        compiler_params=pltpu.CompilerParams(
            dimension_semantics=("arbitrary",), vmem_limit_bytes=VMEM_LIMIT),
        name="out_ffn",
    )(x2, oa, ob, oc, wo, g, wg, wu, wd)


def _block_diag_ones(groups, width):
    g = np.zeros((width, width), np.float32)
    for lo, hi in groups:
        g[lo:hi, lo:hi] = 1.0
    return jnp.asarray(g, jnp.bfloat16)


def _lane_rows(rows):
    rows = [r.astype(jnp.float32) for r in rows]
    rows += [jnp.zeros_like(rows[0])] * (8 - len(rows))
    return jnp.stack(rows, axis=1)


def _head_lanes(nope, rope):
    pad = jnp.zeros(nope.shape[:-1] + (MLA_HEAD_PAD - QK_HEAD,), jnp.float32)
    one = jnp.concatenate([nope.astype(jnp.float32), rope.astype(jnp.float32), pad], axis=-1)
    return jnp.tile(one, (1,) * (one.ndim - 1) + (MLA_HEADS,))


def _pack_params(attn_norm, w_in, swa_q_gain, swa_k_gain, pool_w, pool_scale, mla_q_a_gain,
                 mla_w_qb, mla_kv_a_gain, mla_w_kvb, mla_q_nope_gain, mla_q_rope_gain,
                 mla_k_nope_gain, mla_k_rope_gain, score_shift):
    f32, bf16 = jnp.float32, jnp.bfloat16
    depth = w_in.shape[0]
    zeros = lambda c: jnp.zeros((depth, D_MODEL, c), f32)
    w1 = jnp.concatenate([w_in[:, :, :C_KR], zeros(QK_NOPE), w_in[:, :, C_KR:],
                          zeros(LANE - QK_HEAD)], axis=2).astype(bf16)
    wqb = mla_w_qb.reshape(depth, Q_LORA, MLA_HEADS, QK_HEAD)
    wqb = jnp.pad(wqb, ((0, 0), (0, 0), (0, 0), (0, MLA_HEAD_PAD - QK_HEAD)))
    wqb = wqb.reshape(depth, Q_LORA, MLA_PAD_DIM).astype(bf16)
    wkvb = mla_w_kvb.reshape(depth, KV_LORA, MLA_HEADS, QK_NOPE + V_HEAD)
    wk = jnp.pad(wkvb[..., :QK_NOPE], ((0, 0), (0, 0), (0, 0), (0, MLA_HEAD_PAD - QK_NOPE)))
    wkvb_p = jnp.concatenate([wk.reshape(depth, KV_LORA, MLA_PAD_DIM),
                              wkvb[..., QK_NOPE:].reshape(depth, KV_LORA, MLA_OUT_DIM)],
                             axis=2).astype(bf16)
    ngroup = len(POOL_WINDOWS)
    eye = jnp.eye(ngroup, dtype=f32)[None, :, None, :, None]
    wp = (eye * pool_w[:, :, :, None, :]).reshape(depth, POOL_DIM, POOL_DIM).astype(bf16)
    ones_n, ones_r = jnp.ones((depth, QK_NOPE), f32), jnp.ones((depth, QK_ROPE), f32)
    half = jnp.concatenate([jnp.zeros((depth, ROPE_HALF), f32), jnp.ones((depth, ROPE_HALF), f32)],
                           axis=1)
    pad_lane = lambda v: jnp.tile(
        jnp.pad(v, ((0, 0), (QK_HEAD, MLA_HEAD_PAD - QK_HEAD - 1))), (1, MLA_HEADS))
    v768 = _lane_rows([
        _head_lanes(mla_q_nope_gain, mla_q_rope_gain) * (QK_HEAD ** -0.5 * LOG2E),
        _head_lanes(ones_n / QK_NOPE, ones_r / QK_ROPE),
        _head_lanes(mla_k_nope_gain, 0.0 * ones_r),
        _head_lanes(0.0 * ones_n, half),
        _head_lanes(0.0 * ones_n, half - 1.0),
        pad_lane(jnp.ones((depth, 1), f32)),
        pad_lane(-score_shift[:, None]),
    ])
    v384 = (jnp.tile(swa_q_gain, (1, SWA_HEADS)) * (HEAD_DIM ** -0.5 * LOG2E))[:, None, :]
    windows = jnp.broadcast_to(jnp.repeat(jnp.asarray(POOL_WINDOWS, f32), POOL_GROUP_DIM),
                               (depth, POOL_DIM))
    v256 = _lane_rows([mla_q_a_gain, pool_scale, windows])
    v128 = _lane_rows([jnp.tile(swa_k_gain, (1, SWA_KV_HEADS)), mla_kv_a_gain,
                       _head_lanes(0.0 * ones_n, mla_k_rope_gain)[:, :LANE]])
    stacked = (attn_norm[:, None, :], w1, v768, v384, v256, v128, wp, wqb, wkvb_p)

    inv_freq = ROPE_THETA ** (-jnp.arange(0, QK_ROPE, 2, dtype=f32) / QK_ROPE)
    g384 = _block_diag_ones([(h * HEAD_DIM, (h + 1) * HEAD_DIM) for h in range(SWA_HEADS)],
                            SWA_Q_DIM)
    gsum = np.zeros((MLA_PAD_DIM, LANE), np.float32)
    for h in range(MLA_HEADS):
        base = h * MLA_HEAD_PAD
        gsum[base:base + QK_NOPE, 2 * h] = 1.0
        gsum[base + QK_NOPE:base + QK_HEAD, 2 * h + 1] = 1.0
    shared = (g384, jnp.asarray(gsum, bf16), jnp.asarray(gsum.T, bf16),
              inv_freq.reshape(ROPE_HALF, 1))
    return stacked, shared


def kernel(x, positions, rel_bias, attn_norm, w_in, swa_q_gain, swa_k_gain, swa_sinks, pool_w,
           pool_scale, mla_q_a_gain, mla_w_qb, mla_kv_a_gain, mla_w_kvb, mla_q_nope_gain,
           mla_q_rope_gain, mla_k_nope_gain, mla_k_rope_gain, w_out, ffn_norm, w_gate, w_up,
           w_down):
    batch, seq, _ = x.shape
    n = batch * seq
    depth = w_in.shape[0]
    bf16 = jnp.bfloat16
    x2 = x.reshape(n, D_MODEL)
    pos2 = positions.astype(jnp.float32).reshape(1, n)
    bias = _bias_table(rel_bias)
    score_bound = _mla_score_bound(mla_q_nope_gain, mla_q_rope_gain, mla_k_nope_gain,
                                   mla_k_rope_gain)
    bounded = score_bound <= MLA_MAX_BOUND
    stacked, shared = _pack_params(attn_norm, w_in, swa_q_gain, swa_k_gain, pool_w, pool_scale,
                                   mla_q_a_gain, mla_w_qb, mla_kv_a_gain, mla_w_kvb,
                                   mla_q_nope_gain, mla_q_rope_gain, mla_k_nope_gain,
                                   mla_k_rope_gain, jnp.where(bounded, score_bound, 0.0))
    ffn = (w_out.astype(bf16), ffn_norm[:, None, :], w_gate.astype(bf16), w_up.astype(bf16),
           w_down.astype(bf16))
    for l in range(depth):
        qs, ks, vs, pool_o, qm, km, vm = _in_proj(x2, pos2, l, stacked, shared, batch, seq)
        swa_o = _swa(swa_sinks[l], qs, ks, vs, bias, batch, seq)
        mla_o = lax.cond(bounded[l],
                         functools.partial(_mla_bounded, batch=batch, seq=seq),
                         functools.partial(_mla, batch=batch, seq=seq), qm, km, vm)
        x2 = _out_ffn(x2, swa_o, pool_o, mla_o, l, *ffn)
    return x2.reshape(batch, seq, D_MODEL)
```

```python
import functools
import math

import numpy as np
import jax
import jax.numpy as jnp
from jax import lax
from jax.experimental import pallas as pl
from jax.experimental.pallas import tpu as pltpu

D_MODEL = 1024
HEAD_DIM = 64
SWA_HEADS = 6
SWA_KV_HEADS = 2
SWA_GROUP = SWA_HEADS // SWA_KV_HEADS
WINDOW = 128
POOL_WINDOWS = (2, 4, 8, 16)
POOL_GROUP_DIM = 64
POOL_DIM = POOL_GROUP_DIM * len(POOL_WINDOWS)
MLA_HEADS = 6
Q_LORA = 256
KV_LORA = 128
QK_NOPE = 64
QK_ROPE = 32
V_HEAD = 64
QK_HEAD = QK_NOPE + QK_ROPE
ROPE_THETA = 10000.0
N_BUCKETS = 32
MAX_DISTANCE = 128
SWA_Q_DIM = SWA_HEADS * HEAD_DIM
SWA_KV_DIM = SWA_KV_HEADS * HEAD_DIM
MLA_OUT_DIM = MLA_HEADS * V_HEAD
D_FF = 2816
EPS = 1e-6

LANE = 128
MLA_HEAD_PAD = LANE
MLA_PAD_DIM = MLA_HEADS * MLA_HEAD_PAD
ROPE_HALF = QK_ROPE // 2
MLA_V_ROWS = 80
POOL_HALO = max(POOL_WINDOWS)
NEG = -1e30
LOG2E = math.log2(math.e)

C_QA = 0
C_KA = C_QA + SWA_Q_DIM
C_VA = C_KA + SWA_KV_DIM
C_UB = C_VA + SWA_KV_DIM
C_CQ = C_UB + POOL_DIM
C_CKV = C_CQ + Q_LORA
C_KR = C_CKV + KV_LORA
IN_PAD_DIM = C_KR + LANE

IN_TILE = 1024
IN_SPLIT = 2
SWA_TILE = 1024
MLA_TQ = 1024
MLA_BLK = 1024
MLA_SLAB = 256
MLA_UNROLL = 4
MLA_AHEAD = 2
MLA_MAX_BOUND = 48.0
MLA_BOUND_MARGIN = 1.02
FFN_TILE = 1024
FFN_CHUNKS = ((0, 1536), (1536, 1280))
VMEM_LIMIT = 56 * 1024 * 1024


def _dot(a, b):
    return jnp.dot(a, b, preferred_element_type=jnp.float32)


def _dot_nt(a, b):
    return lax.dot_general(a, b, (((1,), (1,)), ((), ())), preferred_element_type=jnp.float32)


def _rms(x, width):
    return x * lax.rsqrt(jnp.sum(x * x, axis=-1, keepdims=True) * (1.0 / width) + EPS)


def _bias_kernel(rel_ref, shift_ref, bucket_ref, out_ref):
    bucket = bucket_ref[...]
    for h in range(SWA_HEADS):
        acc = jnp.full(bucket.shape, NEG, jnp.float32)
        for b in range(N_BUCKETS):
            acc = jnp.where(bucket == b, rel_ref[b, h] * LOG2E - shift_ref[0], acc)
        out_ref[h] = acc


def _band_buckets():
    q_loc = np.arange(WINDOW)[:, None]
    k_loc = np.arange(2 * WINDOW)[None, :]
    dist = q_loc + WINDOW - k_loc
    band_ok = (dist >= 0) & (dist < WINDOW)
    n = np.maximum(dist, 0)
    max_exact = N_BUCKETS // 2
    nf = np.maximum(n, 1).astype(np.float32)
    large = max_exact + (np.log(nf / max_exact) / math.log(MAX_DISTANCE / max_exact)
                         * (N_BUCKETS - max_exact)).astype(np.int32)
    large = np.minimum(large, N_BUCKETS - 1)
    bucket = np.where(n < max_exact, n, large)
    return np.where(band_ok, bucket, -1).astype(np.int32)


def _bias_table(rel_bias, shift):
    return pl.pallas_call(
        _bias_kernel,
        out_shape=jax.ShapeDtypeStruct((SWA_HEADS, WINDOW, 2 * WINDOW), jnp.float32),
        in_specs=[pl.BlockSpec(memory_space=pltpu.SMEM), pl.BlockSpec(memory_space=pltpu.SMEM),
                  pl.BlockSpec(memory_space=pltpu.VMEM)],
        out_specs=pl.BlockSpec(memory_space=pltpu.VMEM),
        name="bias_table",
    )(rel_bias, shift, jnp.asarray(_band_buckets()))


def _in_proj_kernel(x_ref, pos_ref, g_attn_ref, w1_ref, g384_ref, gsum_ref, gexp_ref, v768_ref,
                    v384_ref, v256_ref, v128_ref, invf_ref, wp_ref, wqb_ref, wkvb_ref,
                    qs_ref, ks_ref, vs_ref, pool_ref, qm_ref, km_ref, vm_ref, carry_ref):
    i = pl.program_id(1)
    tm = x_ref.shape[0]
    bf16 = jnp.bfloat16

    x = x_ref[...]
    h = (_rms(x, D_MODEL) * g_attn_ref[...]).astype(bf16)
    bounds = [(r * tm // IN_SPLIT, (r + 1) * tm // IN_SPLIT) for r in range(IN_SPLIT)]
    projs = [_dot(h[r0:r1], w1_ref[...]) for r0, r1 in bounds]

    m1 = v768_ref[3:4, :]
    m2 = v768_ref[4:5, :]

    def rope(xn, cos_t, sin_t, m1_t, m2_t):
        width = xn.shape[1]
        rot = pltpu.roll(xn, ROPE_HALF, 1) * m1_t + pltpu.roll(xn, width - ROPE_HALF, 1) * m2_t
        return xn * cos_t + rot * sin_t

    def group_sums(v):
        compact = _dot((v * v).astype(bf16), gsum_ref[...])
        return _dot(compact.astype(bf16), gexp_ref[...])

    def prepare(proj, r0, r1):
        rows = r1 - r0
        qa = proj[:, C_QA:C_KA]
        ss = _dot((qa * qa).astype(bf16), g384_ref[...])
        qs_ref[r0:r1, :] = (qa * lax.rsqrt(ss * (1.0 / HEAD_DIM) + EPS) * v384_ref[...]).astype(bf16)
        ka = proj[:, C_KA:C_VA]
        ss = _dot((ka * ka).astype(bf16), g384_ref[0:SWA_KV_DIM, 0:SWA_KV_DIM])
        ks_ref[r0:r1, :] = (ka * lax.rsqrt(ss * (1.0 / HEAD_DIM) + EPS)
                            * v128_ref[0:1, :]).astype(bf16)
        vs_ref[r0:r1, :] = proj[:, C_VA:C_UB].astype(bf16)

        ang_t = invf_ref[...] * pos_ref[:, r0:r1]
        cos_t, sin_t = jnp.cos(ang_t), jnp.sin(ang_t)
        one_t = jnp.ones((QK_NOPE, rows), jnp.float32)
        zero_t = jnp.zeros((QK_NOPE, rows), jnp.float32)
        tail = LANE - QK_HEAD
        cos1 = jnp.concatenate([one_t, cos_t, cos_t, one_t[0:tail]], axis=0).T
        sin1 = jnp.concatenate([zero_t, sin_t, sin_t, zero_t[0:tail]], axis=0).T

        cq = proj[:, C_CQ:C_CKV]
        cqn = (_rms(cq, Q_LORA) * v256_ref[0:1, :]).astype(bf16)
        q = _dot(cqn, wqb_ref[...])
        qn = q * lax.rsqrt(group_sums(q) * v768_ref[1:2, :] + EPS) * v768_ref[0:1, :]
        cos6 = jnp.concatenate([cos1] * MLA_HEADS, axis=1)
        sin6 = jnp.concatenate([sin1] * MLA_HEADS, axis=1)
        qm_ref[r0:r1, :] = (rope(qn, cos6, sin6, m1, m2) + v768_ref[5:6, :]).astype(bf16)

        kr = proj[:, C_KR:IN_PAD_DIM]
        krn = _rms(kr, QK_ROPE) * v128_ref[2:3, :]
        krr = rope(krn, cos1, sin1, m1[:, 0:LANE], m2[:, 0:LANE])
        ckv = proj[:, C_CKV:C_KR]
        ckvn = (_rms(ckv, KV_LORA) * v128_ref[1:2, :]).astype(bf16)
        kv = _dot(ckvn, wkvb_ref[...])
        kn = kv[:, 0:MLA_PAD_DIM]
        kn = kn * lax.rsqrt(group_sums(kn) * (1.0 / QK_NOPE) + EPS) * v768_ref[2:3, :]
        km_ref[r0:r1, :] = (kn + jnp.concatenate([krr] * MLA_HEADS, axis=1)
                            + v768_ref[6:7, :]).astype(bf16)
        vt = kv[:, MLA_PAD_DIM:].T
        fill = (lax.broadcasted_iota(jnp.int32, (MLA_V_ROWS - V_HEAD, rows), 0) == 0
                ).astype(jnp.float32)
        pieces = []
        for hd in range(MLA_HEADS):
            pieces += [vt[hd * V_HEAD:(hd + 1) * V_HEAD], fill]
        vm_ref[:, r0:r1] = jnp.concatenate(pieces, axis=0).astype(bf16)

    for (r0, r1), proj in zip(bounds, projs):
        prepare(proj, r0, r1)

    u = jnp.concatenate([proj[:, C_UB:C_CQ] for proj in projs], axis=0)

    @pl.when(i == 0)
    def _():
        carry_ref[...] = jnp.zeros_like(carry_ref)

    uext = jnp.concatenate([carry_ref[...], u], axis=0)
    a1 = uext + pltpu.roll(uext, 1, 0)
    a2 = a1 + pltpu.roll(a1, 2, 0)
    a3 = a2 + pltpu.roll(a2, 4, 0)
    a4 = a3 + pltpu.roll(a3, 8, 0)
    lane = lax.broadcasted_iota(jnp.int32, (1, POOL_DIM), 1)
    wsum = jnp.where(lane < POOL_GROUP_DIM, a1,
                     jnp.where(lane < 2 * POOL_GROUP_DIM, a2,
                               jnp.where(lane < 3 * POOL_GROUP_DIM, a3, a4)))[POOL_HALO:]
    t = (i * tm + lax.broadcasted_iota(jnp.int32, (tm, 1), 0) + 1).astype(jnp.float32)
    count = jnp.minimum(t, v256_ref[2:3, :])
    d = wsum / count - u
    pool_ref[...] = (_dot(d.astype(bf16), wp_ref[...]) * v256_ref[1:2, :]).astype(bf16)
    carry_ref[...] = u[tm - POOL_HALO:, :]


def _const_spec(shape):
    nd = len(shape)
    return pl.BlockSpec(shape, lambda *_: (0,) * nd)


def _layer_spec(shape, l, single_buffer=False):
    nd = len(shape)
    mode = dict(pipeline_mode=pl.Buffered(1)) if single_buffer else {}
    return pl.BlockSpec((None,) + tuple(shape[1:]), lambda *_: (l,) + (0,) * (nd - 1), **mode)


def _in_proj(x2, pos2, l, stacked, shared, batch, seq):
    g_attn, w1, v768, v384, v256, v128, wp, wqb, wkvb = stacked
    g384, gsum, gexp, invf = shared
    consts = [g_attn, w1, g384, gsum, gexp, v768, v384, v256, v128, invf, wp, wqb, wkvb]
    is_stacked = [True, True, False, False, False, True, True, True, True, False, True, True, True]
    const_specs = [_layer_spec(c.shape, l) if st else _const_spec(c.shape)
                   for c, st in zip(consts, is_stacked)]
    n = x2.shape[0]
    tm = min(IN_TILE, seq)
    nt = seq // tm
    row = lambda b, i: (b * nt + i, 0)
    bf16 = jnp.bfloat16
    col = lambda b, i: (0, b * nt + i)
    outs = [(SWA_Q_DIM, bf16), (SWA_KV_DIM, bf16), (SWA_KV_DIM, bf16), (POOL_DIM, bf16),
            (MLA_PAD_DIM, bf16), (MLA_PAD_DIM, bf16)]
    vt_rows = MLA_HEADS * MLA_V_ROWS
    return pl.pallas_call(
        _in_proj_kernel,
        grid=(batch, nt),
        in_specs=[pl.BlockSpec((tm, D_MODEL), row), pl.BlockSpec((1, tm), col)]
                 + const_specs,
        out_specs=[pl.BlockSpec((tm, w), row) for w, _ in outs]
                  + [pl.BlockSpec((vt_rows, tm), col)],
        out_shape=[jax.ShapeDtypeStruct((n, w), dt) for w, dt in outs]
                  + [jax.ShapeDtypeStruct((vt_rows, n), bf16)],
        scratch_shapes=[pltpu.VMEM((POOL_HALO, POOL_DIM), jnp.float32)],
        compiler_params=pltpu.CompilerParams(
            dimension_semantics=("arbitrary", "arbitrary"), vmem_limit_bytes=VMEM_LIMIT),
        name="in_proj",
    )(x2, pos2, *consts)


def _swa_kernel(sink_ref, q_ref, kc_ref, kp_ref, vc_ref, vp_ref, bias_ref, o_ref, *, bounded):
    i = pl.program_id(1)
    ts = q_ref.shape[0]
    nwin = ts // WINDOW
    bf16 = jnp.bfloat16
    q = q_ref[...]
    kext = jnp.concatenate([kp_ref[...], kc_ref[...]], axis=0)
    vext = jnp.concatenate([vp_ref[...], vc_ref[...]], axis=0)
    col = lax.broadcasted_iota(jnp.int32, (1, 2 * WINDOW), 1)
    no_prev = jnp.logical_and(i == 0, col < WINDOW)
    ones = jnp.ones((2 * WINDOW, LANE), bf16)
    heads = range(SWA_HEADS)
    sinks = [sink_ref[h] * LOG2E - sink_ref[SWA_HEADS] for h in heads]

    def scores(c, h):
        g = h // SWA_GROUP
        qh = q[c * WINDOW:(c + 1) * WINDOW, h * HEAD_DIM:(h + 1) * HEAD_DIM]
        kb = kext[c * WINDOW:(c + 2) * WINDOW, g * HEAD_DIM:(g + 1) * HEAD_DIM]
        s = _dot_nt(qh, kb) + bias_ref[h]
        if c == 0:
            s = jnp.where(no_prev, NEG, s)
        return s

    pending = [scores(0, h) for h in heads]
    for c in range(nwin):
        s_all = pending
        if bounded:
            ms = [0.0 for h in heads]
        else:
            ms = [jnp.maximum(jnp.max(s_all[h], axis=-1, keepdims=True), sinks[h]) for h in heads]
        pending = [scores(c + 1, h) for h in heads] if c + 1 < nwin else []
        ps = [jnp.exp2(s_all[h] - ms[h]).astype(bf16) for h in heads]
        vaug = jnp.concatenate([vext[c * WINDOW:(c + 2) * WINDOW], ones], axis=1)
        os_ = [_dot(ps[h], vaug) for h in heads]
        outs = []
        for h in heads:
            lo = (h // SWA_GROUP) * HEAD_DIM
            denom = os_[h][:, SWA_KV_DIM + lo:SWA_KV_DIM + lo + HEAD_DIM] + jnp.exp2(sinks[h] - ms[h])
            outs.append(os_[h][:, lo:lo + HEAD_DIM] / denom)
        o_ref[c * WINDOW:(c + 1) * WINDOW, :] = jnp.concatenate(outs, axis=1).astype(bf16)


def _swa(sinks, qs, ks, vs, bias, batch, seq, bounded):
    n = qs.shape[0]
    ts = min(SWA_TILE, seq)
    nt = seq // ts
    wpt = ts // WINDOW
    wps = seq // WINDOW
    row = lambda b, i: (b * nt + i, 0)
    prev = lambda b, i: (b * wps + jnp.maximum(i * wpt - 1, 0), 0)
    return pl.pallas_call(
        functools.partial(_swa_kernel, bounded=bounded),
        grid=(batch, nt),
        in_specs=[pl.BlockSpec(memory_space=pltpu.SMEM),
                  pl.BlockSpec((ts, SWA_Q_DIM), row),
                  pl.BlockSpec((ts, SWA_KV_DIM), row), pl.BlockSpec((WINDOW, SWA_KV_DIM), prev),
                  pl.BlockSpec((ts, SWA_KV_DIM), row), pl.BlockSpec((WINDOW, SWA_KV_DIM), prev),
                  _const_spec(bias.shape)],
        out_specs=pl.BlockSpec((ts, SWA_Q_DIM), row),
        out_shape=jax.ShapeDtypeStruct((n, SWA_Q_DIM), jnp.bfloat16),
        compiler_params=pltpu.CompilerParams(
            dimension_semantics=("arbitrary", "arbitrary"), vmem_limit_bytes=VMEM_LIMIT),
        name="swa_bounded" if bounded else "swa",
    )(sinks, qs, ks, ks, vs, vs, bias)


def _swa_score_shift(swa_q_gain, swa_k_gain, rel_bias, swa_sinks):
    gmax = lambda g: jnp.max(jnp.abs(g.astype(jnp.float32)), axis=-1)
    score = (HEAD_DIM ** 0.5 * gmax(swa_q_gain) * gmax(swa_k_gain) * MLA_BOUND_MARGIN
             + jnp.max(jnp.abs(rel_bias.astype(jnp.float32)))) * LOG2E
    shift = jnp.maximum(score, gmax(swa_sinks) * LOG2E)
    bounded = shift <= MLA_MAX_BOUND
    return jnp.where(bounded, shift, 0.0), bounded


def _mla_kernel(q_ref, k_ref, vt_ref, o_ref, m_ref, acc_ref, sa_ref, sb_ref, ma_ref, mb_ref):
    i = pl.program_id(2)
    tq = q_ref.shape[0]
    blk = max(MLA_BLK, tq)
    ratio = blk // tq
    bf16 = jnp.bfloat16
    chains = [(a, c) for a in range(2) for c in range(tq // MLA_SLAB)]
    m_ref[...] = jnp.full(m_ref.shape, NEG, jnp.float32)
    acc_ref[...] = jnp.zeros(acc_ref.shape, jnp.float32)

    def score(a, c, start, nk):
        lanes = slice(a * LANE, (a + 1) * LANE)
        return _dot_nt(k_ref[pl.ds(start, nk), lanes],
                       q_ref[c * MLA_SLAB:(c + 1) * MLA_SLAB, lanes])

    def full_score(ch, j, s_ref, mx_ref):
        a, c = chains[ch]
        s = score(a, c, pl.multiple_of(j * blk, blk), blk)
        s_ref[ch] = s
        mx_ref[ch] = jnp.max(s, axis=0, keepdims=True)

    def tail_score(ch, start, nfree):
        a, c = chains[ch]
        nk = nfree + (c + 1) * MLA_SLAB
        s = score(a, c, start, nk)
        row = lax.broadcasted_iota(jnp.int32, (nk, MLA_SLAB), 0)
        col = lax.broadcasted_iota(jnp.int32, (nk, MLA_SLAB), 1) + (c * MLA_SLAB + nfree)
        s = jnp.where(row <= col, s, NEG)
        return s, jnp.max(s, axis=0, keepdims=True)

    def soft_pv(ch, s, mx, start):
        a, c = chains[ch]
        cols = slice(c * MLA_SLAB, (c + 1) * MLA_SLAB)
        m_prev = m_ref[a, :, cols]
        m_new = jnp.maximum(m_prev, mx)
        alpha = jnp.exp2(m_prev - m_new)
        p = jnp.exp2(s - m_new).astype(bf16)
        vt = vt_ref[a * MLA_V_ROWS:(a + 1) * MLA_V_ROWS, pl.ds(start, s.shape[0])]
        acc_ref[a, :, cols] = alpha * acc_ref[a, :, cols] + _dot(vt, p)
        m_ref[a, :, cols] = m_new

    bufs = ((sa_ref, ma_ref), (sb_ref, mb_ref))
    nchain = len(chains)

    def pipe_step(j, cur, nxt):
        s_cur, mx_cur = bufs[cur]
        full_score(0, j + 1, *bufs[nxt])
        for ch in range(nchain):
            if ch + 1 < nchain:
                full_score(ch + 1, j + 1, *bufs[nxt])
            soft_pv(ch, s_cur[ch], mx_cur[ch], pl.multiple_of(j * blk, blk))

    nfull = i // ratio

    @pl.when(nfull > 0)
    def _():
        for ch in range(nchain):
            full_score(ch, 0, *bufs[0])

    def pair_body(t, carry):
        pipe_step(2 * t, 0, 1)
        pipe_step(2 * t + 1, 1, 0)
        return carry

    lax.fori_loop(0, jnp.maximum(nfull - 1, 0) // 2, pair_body, 0)

    def finish(has_full, last, nfree):
        if has_full and last == 1:
            pipe_step(nfull - 2, 0, 1)
        tail_start = pl.multiple_of(nfull * blk, blk)
        tail = [tail_score(0, tail_start, nfree)]
        for ch in range(nchain):
            if ch + 1 < nchain:
                tail.append(tail_score(ch + 1, tail_start, nfree))
            if has_full:
                s_last, mx_last = bufs[last]
                soft_pv(ch, s_last[ch], mx_last[ch], pl.multiple_of((nfull - 1) * blk, blk))
        for ch in range(nchain):
            soft_pv(ch, *tail[ch], tail_start)

    for r in range(ratio):
        nfree = r * tq
        is_odd = (i % ratio) == r
        pl.when(jnp.logical_and(nfull == 0, is_odd))(functools.partial(finish, False, 0, nfree))
        for last in (0, 1):
            cond = jnp.logical_and(jnp.logical_and(nfull > 0, (nfull - 1) % 2 == last), is_odd)
            pl.when(cond)(functools.partial(finish, True, last, nfree))

    outs = []
    for a in range(2):
        acc = acc_ref[a]
        outs.append(acc[0:V_HEAD] / acc[V_HEAD:V_HEAD + 1])
    o_ref[...] = jnp.concatenate(outs, axis=0).T.astype(bf16)


def _mla(qm, km, vm, batch, seq):
    n = qm.shape[0]
    tq = min(MLA_TQ, seq)
    nq = seq // tq
    pairs = MLA_HEADS // 2
    nchain = 2 * (tq // MLA_SLAB)
    return pl.pallas_call(
        _mla_kernel,
        grid=(batch, pairs, nq),
        in_specs=[pl.BlockSpec((tq, 2 * LANE), lambda b, p, i: (b * nq + i, p)),
                  pl.BlockSpec((seq, 2 * LANE), lambda b, p, i: (b, p)),
                  pl.BlockSpec((2 * MLA_V_ROWS, seq), lambda b, p, i: (p, b))],
        out_specs=pl.BlockSpec((tq, LANE), lambda b, p, i: (b * nq + i, p)),
        out_shape=jax.ShapeDtypeStruct((n, MLA_OUT_DIM), jnp.bfloat16),
        scratch_shapes=[pltpu.VMEM((2, 1, tq), jnp.float32),
                        pltpu.VMEM((2, MLA_V_ROWS, tq), jnp.float32)]
                       + [pltpu.VMEM((nchain, max(MLA_BLK, tq), MLA_SLAB), jnp.float32)] * 2
                       + [pltpu.VMEM((nchain, 1, MLA_SLAB), jnp.float32)] * 2,
        compiler_params=pltpu.CompilerParams(
            dimension_semantics=("arbitrary", "arbitrary", "arbitrary"),
            vmem_limit_bytes=VMEM_LIMIT),
        name="mla",
    )(qm, km, vm)


def _mla_bounded_kernel(q_ref, k_ref, vt_ref, o_ref, acc_ref):
    i = pl.program_id(2)
    tq = q_ref.shape[0]
    blk = tq
    bf16 = jnp.bfloat16
    chains = [(a, c) for a in range(2) for c in range(tq // MLA_SLAB)]
    nchain = len(chains)
    acc_ref[...] = jnp.zeros(acc_ref.shape, jnp.float32)

    def score(ch, start, nk):
        a, c = chains[ch]
        lanes = slice(a * LANE, (a + 1) * LANE)
        return _dot_nt(k_ref[pl.ds(start, nk), lanes],
                       q_ref[c * MLA_SLAB:(c + 1) * MLA_SLAB, lanes])

    def weigh(ch, s, start):
        a, c = chains[ch]
        cols = slice(c * MLA_SLAB, (c + 1) * MLA_SLAB)
        p = jnp.exp2(s).astype(bf16)
        vt = vt_ref[a * MLA_V_ROWS:(a + 1) * MLA_V_ROWS, pl.ds(start, s.shape[0])]
        acc_ref[a, :, cols] += _dot(vt, p)

    def blocks(starts, masked_last):
        work = [(ch, st, masked_last and b == len(starts) - 1)
                for b, st in enumerate(starts) for ch in range(nchain)]

        def chain_scores(ch, start, masked):
            if not masked:
                return score(ch, start, blk)
            c = chains[ch][1]
            nk = (c + 1) * MLA_SLAB
            s = score(ch, start, nk)
            row = lax.broadcasted_iota(jnp.int32, (nk, MLA_SLAB), 0)
            col = lax.broadcasted_iota(jnp.int32, (nk, MLA_SLAB), 1) + c * MLA_SLAB
            return jnp.where(row <= col, s, NEG)

        ahead = min(MLA_AHEAD, len(work))
        pend = [chain_scores(*work[w]) for w in range(ahead)]
        for w in range(len(work)):
            if w + ahead < len(work):
                pend.append(chain_scores(*work[w + ahead]))
            weigh(work[w][0], pend.pop(0), work[w][1])

    nb = MLA_UNROLL

    def body(t, carry):
        st = pl.multiple_of(nb * t * blk, blk)
        blocks([st + k * blk for k in range(nb)], False)
        return carry

    lax.fori_loop(0, i // nb, body, 0)
    for r in range(nb):
        def tail(r=r):
            first = pl.multiple_of((i - r) * blk, blk)
            blocks([first + k * blk for k in range(r + 1)], True)
        pl.when(i % nb == r)(tail)

    outs = []
    for a in range(2):
        acc = acc_ref[a]
        outs.append(acc[0:V_HEAD] / acc[V_HEAD:V_HEAD + 1])
    o_ref[...] = jnp.concatenate(outs, axis=0).T.astype(bf16)


def _mla_bounded(qm, km, vm, batch, seq):
    n = qm.shape[0]
    tq = min(MLA_TQ, seq)
    nq = seq // tq
    pairs = MLA_HEADS // 2
    return pl.pallas_call(
        _mla_bounded_kernel,
        grid=(batch, pairs, nq),
        in_specs=[pl.BlockSpec((tq, 2 * LANE), lambda b, p, i: (b * nq + i, p)),
                  pl.BlockSpec((seq, 2 * LANE), lambda b, p, i: (b, p)),
                  pl.BlockSpec((2 * MLA_V_ROWS, seq), lambda b, p, i: (p, b))],
        out_specs=pl.BlockSpec((tq, LANE), lambda b, p, i: (b * nq + i, p)),
        out_shape=jax.ShapeDtypeStruct((n, MLA_OUT_DIM), jnp.bfloat16),
        scratch_shapes=[pltpu.VMEM((2, MLA_V_ROWS, tq), jnp.float32)],
        compiler_params=pltpu.CompilerParams(
            dimension_semantics=("arbitrary", "arbitrary", "arbitrary"),
            vmem_limit_bytes=VMEM_LIMIT),
        name="mla_bounded",
    )(qm, km, vm)


def _mla_score_bound(q_nope_gain, q_rope_gain, k_nope_gain, k_rope_gain):
    gmax = lambda g: jnp.max(jnp.abs(g.astype(jnp.float32)), axis=-1)
    dots = (QK_NOPE * gmax(q_nope_gain) * gmax(k_nope_gain)
            + QK_ROPE * gmax(q_rope_gain) * gmax(k_rope_gain))
    return dots * (QK_HEAD ** -0.5 * LOG2E * MLA_BOUND_MARGIN)


def _out_ffn_kernel(x_ref, a_ref, b_ref, c_ref, wo_ref, g_ref, wg_ref, wu_ref, wd_ref, o_ref):
    bf16 = jnp.bfloat16
    mixed = jnp.concatenate([a_ref[...], b_ref[...], c_ref[...]], axis=1)
    x1 = x_ref[...] + _dot(mixed, wo_ref[...])
    h = (_rms(x1, D_MODEL) * g_ref[...]).astype(bf16)
    acc = x1
    for start, size in FFN_CHUNKS:
        gate = _dot(h, wg_ref[:, start:start + size])
        up = _dot(h, wu_ref[:, start:start + size])
        act = (gate * jax.nn.sigmoid(gate) * up).astype(bf16)
        acc = acc + _dot(act, wd_ref[start:start + size, :])
    o_ref[...] = acc


def _out_ffn(x2, oa, ob, oc, l, wo, g, wg, wu, wd):
    n = x2.shape[0]
    tm = min(FFN_TILE, n)
    row = lambda i: (i, 0)
    return pl.pallas_call(
        _out_ffn_kernel,
        grid=(n // tm,),
        in_specs=[pl.BlockSpec((tm, D_MODEL), row), pl.BlockSpec((tm, SWA_Q_DIM), row),
                  pl.BlockSpec((tm, POOL_DIM), row), pl.BlockSpec((tm, MLA_OUT_DIM), row)]
                 + [_layer_spec(w.shape, l, single_buffer=True) for w in (wo, g, wg, wu, wd)],
        out_specs=pl.BlockSpec((tm, D_MODEL), row),
        out_shape=jax.ShapeDtypeStruct((n, D_MODEL), jnp.float32),
        compiler_params=pltpu.CompilerParams(
            dimension_semantics=("arbitrary",), vmem_limit_bytes=VMEM_LIMIT),
        name="out_ffn",
    )(x2, oa, ob, oc, wo, g, wg, wu, wd)


def _block_diag_ones(groups, width):
    g = np.zeros((width, width), np.float32)
    for lo, hi in groups:
        g[lo:hi, lo:hi] = 1.0
    return jnp.asarray(g, jnp.bfloat16)


def _lane_rows(rows):
    rows = [r.astype(jnp.float32) for r in rows]
    rows += [jnp.zeros_like(rows[0])] * (8 - len(rows))
    return jnp.stack(rows, axis=1)


def _head_lanes(nope, rope):
    pad = jnp.zeros(nope.shape[:-1] + (MLA_HEAD_PAD - QK_HEAD,), jnp.float32)
    one = jnp.concatenate([nope.astype(jnp.float32), rope.astype(jnp.float32), pad], axis=-1)
    return jnp.tile(one, (1,) * (one.ndim - 1) + (MLA_HEADS,))


def _pack_params(attn_norm, w_in, swa_q_gain, swa_k_gain, pool_w, pool_scale, mla_q_a_gain,
                 mla_w_qb, mla_kv_a_gain, mla_w_kvb, mla_q_nope_gain, mla_q_rope_gain,
                 mla_k_nope_gain, mla_k_rope_gain, score_shift):
    f32, bf16 = jnp.float32, jnp.bfloat16
    depth = w_in.shape[0]
    zeros = lambda c: jnp.zeros((depth, D_MODEL, c), f32)
    w1 = jnp.concatenate([w_in[:, :, :C_KR], zeros(QK_NOPE), w_in[:, :, C_KR:],
                          zeros(LANE - QK_HEAD)], axis=2).astype(bf16)
    wqb = mla_w_qb.reshape(depth, Q_LORA, MLA_HEADS, QK_HEAD)
    wqb = jnp.pad(wqb, ((0, 0), (0, 0), (0, 0), (0, MLA_HEAD_PAD - QK_HEAD)))
    wqb = wqb.reshape(depth, Q_LORA, MLA_PAD_DIM).astype(bf16)
    wkvb = mla_w_kvb.reshape(depth, KV_LORA, MLA_HEADS, QK_NOPE + V_HEAD)
    wk = jnp.pad(wkvb[..., :QK_NOPE], ((0, 0), (0, 0), (0, 0), (0, MLA_HEAD_PAD - QK_NOPE)))
    wkvb_p = jnp.concatenate([wk.reshape(depth, KV_LORA, MLA_PAD_DIM),
                              wkvb[..., QK_NOPE:].reshape(depth, KV_LORA, MLA_OUT_DIM)],
                             axis=2).astype(bf16)
    ngroup = len(POOL_WINDOWS)
    eye = jnp.eye(ngroup, dtype=f32)[None, :, None, :, None]
    wp = (eye * pool_w[:, :, :, None, :]).reshape(depth, POOL_DIM, POOL_DIM).astype(bf16)
    ones_n, ones_r = jnp.ones((depth, QK_NOPE), f32), jnp.ones((depth, QK_ROPE), f32)
    half = jnp.concatenate([jnp.zeros((depth, ROPE_HALF), f32), jnp.ones((depth, ROPE_HALF), f32)],
                           axis=1)
    pad_lane = lambda v: jnp.tile(
        jnp.pad(v, ((0, 0), (QK_HEAD, MLA_HEAD_PAD - QK_HEAD - 1))), (1, MLA_HEADS))
    v768 = _lane_rows([
        _head_lanes(mla_q_nope_gain, mla_q_rope_gain) * (QK_HEAD ** -0.5 * LOG2E),
        _head_lanes(ones_n / QK_NOPE, ones_r / QK_ROPE),
        _head_lanes(mla_k_nope_gain, 0.0 * ones_r),
        _head_lanes(0.0 * ones_n, half),
        _head_lanes(0.0 * ones_n, half - 1.0),
        pad_lane(jnp.ones((depth, 1), f32)),
        pad_lane(-score_shift[:, None]),
    ])
    v384 = (jnp.tile(swa_q_gain, (1, SWA_HEADS)) * (HEAD_DIM ** -0.5 * LOG2E))[:, None, :]
    windows = jnp.broadcast_to(jnp.repeat(jnp.asarray(POOL_WINDOWS, f32), POOL_GROUP_DIM),
                               (depth, POOL_DIM))
    v256 = _lane_rows([mla_q_a_gain, pool_scale, windows])
    v128 = _lane_rows([jnp.tile(swa_k_gain, (1, SWA_KV_HEADS)), mla_kv_a_gain,
                       _head_lanes(0.0 * ones_n, mla_k_rope_gain)[:, :LANE]])
    stacked = (attn_norm[:, None, :], w1, v768, v384, v256, v128, wp, wqb, wkvb_p)

    inv_freq = ROPE_THETA ** (-jnp.arange(0, QK_ROPE, 2, dtype=f32) / QK_ROPE)
    g384 = _block_diag_ones([(h * HEAD_DIM, (h + 1) * HEAD_DIM) for h in range(SWA_HEADS)],
                            SWA_Q_DIM)
    gsum = np.zeros((MLA_PAD_DIM, LANE), np.float32)
    for h in range(MLA_HEADS):
        base = h * MLA_HEAD_PAD
        gsum[base:base + QK_NOPE, 2 * h] = 1.0
        gsum[base + QK_NOPE:base + QK_HEAD, 2 * h + 1] = 1.0
    shared = (g384, jnp.asarray(gsum, bf16), jnp.asarray(gsum.T, bf16),
              inv_freq.reshape(ROPE_HALF, 1))
    return stacked, shared


def kernel(x, positions, rel_bias, attn_norm, w_in, swa_q_gain, swa_k_gain, swa_sinks, pool_w,
           pool_scale, mla_q_a_gain, mla_w_qb, mla_kv_a_gain, mla_w_kvb, mla_q_nope_gain,
           mla_q_rope_gain, mla_k_nope_gain, mla_k_rope_gain, w_out, ffn_norm, w_gate, w_up,
           w_down):
    batch, seq, _ = x.shape
    n = batch * seq
    depth = w_in.shape[0]
    bf16 = jnp.bfloat16
    x2 = x.reshape(n, D_MODEL)
    pos2 = positions.astype(jnp.float32).reshape(1, n)
    swa_shift, swa_bounded = _swa_score_shift(swa_q_gain, swa_k_gain, rel_bias, swa_sinks)
    score_bound = _mla_score_bound(mla_q_nope_gain, mla_q_rope_gain, mla_k_nope_gain,
                                   mla_k_rope_gain)
    bounded = score_bound <= MLA_MAX_BOUND
    stacked, shared = _pack_params(attn_norm, w_in, swa_q_gain, swa_k_gain, pool_w, pool_scale,
                                   mla_q_a_gain, mla_w_qb, mla_kv_a_gain, mla_w_kvb,
                                   mla_q_nope_gain, mla_q_rope_gain, mla_k_nope_gain,
                                   mla_k_rope_gain, jnp.where(bounded, score_bound, 0.0))
    ffn = (w_out.astype(bf16), ffn_norm[:, None, :], w_gate.astype(bf16), w_up.astype(bf16),
           w_down.astype(bf16))
    for l in range(depth):
        qs, ks, vs, pool_o, qm, km, vm = _in_proj(x2, pos2, l, stacked, shared, batch, seq)
        bias = _bias_table(rel_bias, swa_shift[l:l + 1])
        sink_shift = jnp.concatenate([swa_sinks[l].astype(jnp.float32), swa_shift[l:l + 1]])
        swa_o = lax.cond(swa_bounded[l],
                         functools.partial(_swa, batch=batch, seq=seq, bounded=True),
                         functools.partial(_swa, batch=batch, seq=seq, bounded=False),
                         sink_shift, qs, ks, vs, bias)
        mla_o = lax.cond(bounded[l],
                         functools.partial(_mla_bounded, batch=batch, seq=seq),
                         functools.partial(_mla, batch=batch, seq=seq), qm, km, vm)
        x2 = _out_ffn(x2, swa_o, pool_o, mla_o, l, *ffn)
    return x2.reshape(batch, seq, D_MODEL)
```

```python
import functools
import math

import numpy as np
import jax
import jax.numpy as jnp
from jax import lax
from jax.experimental import pallas as pl
from jax.experimental.pallas import tpu as pltpu

D_MODEL = 1024
HEAD_DIM = 64
SWA_HEADS = 6
SWA_KV_HEADS = 2
SWA_GROUP = SWA_HEADS // SWA_KV_HEADS
WINDOW = 128
POOL_WINDOWS = (2, 4, 8, 16)
POOL_GROUP_DIM = 64
POOL_DIM = POOL_GROUP_DIM * len(POOL_WINDOWS)
MLA_HEADS = 6
Q_LORA = 256
KV_LORA = 128
QK_NOPE = 64
QK_ROPE = 32
V_HEAD = 64
QK_HEAD = QK_NOPE + QK_ROPE
ROPE_THETA = 10000.0
N_BUCKETS = 32
MAX_DISTANCE = 128
SWA_Q_DIM = SWA_HEADS * HEAD_DIM
SWA_KV_DIM = SWA_KV_HEADS * HEAD_DIM
MLA_OUT_DIM = MLA_HEADS * V_HEAD
D_FF = 2816
EPS = 1e-6

LANE = 128
MLA_HEAD_PAD = LANE
MLA_PAD_DIM = MLA_HEADS * MLA_HEAD_PAD
ROPE_HALF = QK_ROPE // 2
MLA_V_ROWS = 80
POOL_HALO = max(POOL_WINDOWS)
NEG = -1e30
LOG2E = math.log2(math.e)

C_QA = 0
C_KA = C_QA + SWA_Q_DIM
C_VA = C_KA + SWA_KV_DIM
C_UB = C_VA + SWA_KV_DIM
C_CQ = C_UB + POOL_DIM
C_CKV = C_CQ + Q_LORA
C_KR = C_CKV + KV_LORA
IN_PAD_DIM = C_KR + LANE

IN_TILE = 1024
IN_SPLIT = 2
SWA_TILE = 2048
MLA_TQ = 1024
MLA_BLK = 1024
MLA_SLAB = 256
MLA_UNROLL = 4
MLA_AHEAD = 2
MLA_MAX_BOUND = 48.0
MLA_BOUND_MARGIN = 1.02
FFN_TILE = 1024
FFN_CHUNKS = ((0, 1536), (1536, 1280))
VMEM_LIMIT = 56 * 1024 * 1024


def _dot(a, b):
    return jnp.dot(a, b, preferred_element_type=jnp.float32)


def _dot_nt(a, b):
    return lax.dot_general(a, b, (((1,), (1,)), ((), ())), preferred_element_type=jnp.float32)


def _rms(x, width):
    return x * lax.rsqrt(jnp.sum(x * x, axis=-1, keepdims=True) * (1.0 / width) + EPS)


def _bias_kernel(rel_ref, shift_ref, bucket_ref, out_ref):
    bucket = bucket_ref[...]
    for h in range(SWA_HEADS):
        acc = jnp.full(bucket.shape, NEG, jnp.float32)
        for b in range(N_BUCKETS):
            acc = jnp.where(bucket == b, rel_ref[b, h] * LOG2E - shift_ref[0], acc)
        out_ref[h] = acc


def _band_buckets():
    q_loc = np.arange(WINDOW)[:, None]
    k_loc = np.arange(2 * WINDOW)[None, :]
    dist = q_loc + WINDOW - k_loc
    band_ok = (dist >= 0) & (dist < WINDOW)
    n = np.maximum(dist, 0)
    max_exact = N_BUCKETS // 2
    nf = np.maximum(n, 1).astype(np.float32)
    large = max_exact + (np.log(nf / max_exact) / math.log(MAX_DISTANCE / max_exact)
                         * (N_BUCKETS - max_exact)).astype(np.int32)
    large = np.minimum(large, N_BUCKETS - 1)
    bucket = np.where(n < max_exact, n, large)
    return np.where(band_ok, bucket, -1).astype(np.int32)


def _bias_table(rel_bias, shift):
    return pl.pallas_call(
        _bias_kernel,
        out_shape=jax.ShapeDtypeStruct((SWA_HEADS, WINDOW, 2 * WINDOW), jnp.float32),
        in_specs=[pl.BlockSpec(memory_space=pltpu.SMEM), pl.BlockSpec(memory_space=pltpu.SMEM),
                  pl.BlockSpec(memory_space=pltpu.VMEM)],
        out_specs=pl.BlockSpec(memory_space=pltpu.VMEM),
        name="bias_table",
    )(rel_bias, shift, jnp.asarray(_band_buckets()))


def _in_proj_kernel(x_ref, pos_ref, g_attn_ref, w1_ref, g384_ref, gsum_ref, gexp_ref, v768_ref,
                    v384_ref, v256_ref, v128_ref, invf_ref, wp_ref, wqb_ref, wkvb_ref,
                    qs_ref, ks_ref, vs_ref, pool_ref, qm_ref, km_ref, vm_ref, carry_ref):
    i = pl.program_id(1)
    tm = x_ref.shape[0]
    bf16 = jnp.bfloat16

    x = x_ref[...]
    h = (_rms(x, D_MODEL) * g_attn_ref[...]).astype(bf16)
    bounds = [(r * tm // IN_SPLIT, (r + 1) * tm // IN_SPLIT) for r in range(IN_SPLIT)]
    projs = [_dot(h[r0:r1], w1_ref[...]) for r0, r1 in bounds]

    m1 = v768_ref[3:4, :]
    m2 = v768_ref[4:5, :]

    def rope(xn, cos_t, sin_t, m1_t, m2_t):
        width = xn.shape[1]
        rot = pltpu.roll(xn, ROPE_HALF, 1) * m1_t + pltpu.roll(xn, width - ROPE_HALF, 1) * m2_t
        return xn * cos_t + rot * sin_t

    def group_sums(v):
        compact = _dot((v * v).astype(bf16), gsum_ref[...])
        return _dot(compact.astype(bf16), gexp_ref[...])

    def prepare(proj, r0, r1):
        rows = r1 - r0
        qa = proj[:, C_QA:C_KA]
        ss = _dot((qa * qa).astype(bf16), g384_ref[...])
        qs_ref[r0:r1, :] = (qa * lax.rsqrt(ss * (1.0 / HEAD_DIM) + EPS) * v384_ref[...]).astype(bf16)
        ka = proj[:, C_KA:C_VA]
        ss = _dot((ka * ka).astype(bf16), g384_ref[0:SWA_KV_DIM, 0:SWA_KV_DIM])
        ks_ref[r0:r1, :] = (ka * lax.rsqrt(ss * (1.0 / HEAD_DIM) + EPS)
                            * v128_ref[0:1, :]).astype(bf16)
        vs_ref[r0:r1, :] = proj[:, C_VA:C_UB].astype(bf16)

        ang_t = invf_ref[...] * pos_ref[:, r0:r1]
        cos_t, sin_t = jnp.cos(ang_t), jnp.sin(ang_t)
        one_t = jnp.ones((QK_NOPE, rows), jnp.float32)
        zero_t = jnp.zeros((QK_NOPE, rows), jnp.float32)
        tail = LANE - QK_HEAD
        cos1 = jnp.concatenate([one_t, cos_t, cos_t, one_t[0:tail]], axis=0).T
        sin1 = jnp.concatenate([zero_t, sin_t, sin_t, zero_t[0:tail]], axis=0).T

        cq = proj[:, C_CQ:C_CKV]
        cqn = (_rms(cq, Q_LORA) * v256_ref[0:1, :]).astype(bf16)
        q = _dot(cqn, wqb_ref[...])
        qn = q * lax.rsqrt(group_sums(q) * v768_ref[1:2, :] + EPS) * v768_ref[0:1, :]
        cos6 = jnp.concatenate([cos1] * MLA_HEADS, axis=1)
        sin6 = jnp.concatenate([sin1] * MLA_HEADS, axis=1)
        qm_ref[r0:r1, :] = (rope(qn, cos6, sin6, m1, m2) + v768_ref[5:6, :]).astype(bf16)

        kr = proj[:, C_KR:IN_PAD_DIM]
        krn = _rms(kr, QK_ROPE) * v128_ref[2:3, :]
        krr = rope(krn, cos1, sin1, m1[:, 0:LANE], m2[:, 0:LANE])
        ckv = proj[:, C_CKV:C_KR]
        ckvn = (_rms(ckv, KV_LORA) * v128_ref[1:2, :]).astype(bf16)
        kv = _dot(ckvn, wkvb_ref[...])
        kn = kv[:, 0:MLA_PAD_DIM]
        kn = kn * lax.rsqrt(group_sums(kn) * (1.0 / QK_NOPE) + EPS) * v768_ref[2:3, :]
        km_ref[r0:r1, :] = (kn + jnp.concatenate([krr] * MLA_HEADS, axis=1)
                            + v768_ref[6:7, :]).astype(bf16)
        vt = kv[:, MLA_PAD_DIM:].T
        fill = (lax.broadcasted_iota(jnp.int32, (MLA_V_ROWS - V_HEAD, rows), 0) == 0
                ).astype(jnp.float32)
        pieces = []
        for hd in range(MLA_HEADS):
            pieces += [vt[hd * V_HEAD:(hd + 1) * V_HEAD], fill]
        vm_ref[:, r0:r1] = jnp.concatenate(pieces, axis=0).astype(bf16)

    for (r0, r1), proj in zip(bounds, projs):
        prepare(proj, r0, r1)

    u = jnp.concatenate([proj[:, C_UB:C_CQ] for proj in projs], axis=0)

    @pl.when(i == 0)
    def _():
        carry_ref[...] = jnp.zeros_like(carry_ref)

    uext = jnp.concatenate([carry_ref[...], u], axis=0)
    a1 = uext + pltpu.roll(uext, 1, 0)
    a2 = a1 + pltpu.roll(a1, 2, 0)
    a3 = a2 + pltpu.roll(a2, 4, 0)
    a4 = a3 + pltpu.roll(a3, 8, 0)
    lane = lax.broadcasted_iota(jnp.int32, (1, POOL_DIM), 1)
    wsum = jnp.where(lane < POOL_GROUP_DIM, a1,
                     jnp.where(lane < 2 * POOL_GROUP_DIM, a2,
                               jnp.where(lane < 3 * POOL_GROUP_DIM, a3, a4)))[POOL_HALO:]
    t = (i * tm + lax.broadcasted_iota(jnp.int32, (tm, 1), 0) + 1).astype(jnp.float32)
    count = jnp.minimum(t, v256_ref[2:3, :])
    d = wsum / count - u
    pool_ref[...] = (_dot(d.astype(bf16), wp_ref[...]) * v256_ref[1:2, :]).astype(bf16)
    carry_ref[...] = u[tm - POOL_HALO:, :]


def _const_spec(shape):
    nd = len(shape)
    return pl.BlockSpec(shape, lambda *_: (0,) * nd)


def _layer_spec(shape, l, single_buffer=False):
    nd = len(shape)
    mode = dict(pipeline_mode=pl.Buffered(1)) if single_buffer else {}
    return pl.BlockSpec((None,) + tuple(shape[1:]), lambda *_: (l,) + (0,) * (nd - 1), **mode)


def _in_proj(x2, pos2, l, stacked, shared, batch, seq):
    g_attn, w1, v768, v384, v256, v128, wp, wqb, wkvb = stacked
    g384, gsum, gexp, invf = shared
    consts = [g_attn, w1, g384, gsum, gexp, v768, v384, v256, v128, invf, wp, wqb, wkvb]
    is_stacked = [True, True, False, False, False, True, True, True, True, False, True, True, True]
    const_specs = [_layer_spec(c.shape, l) if st else _const_spec(c.shape)
                   for c, st in zip(consts, is_stacked)]
    n = x2.shape[0]
    tm = min(IN_TILE, seq)
    nt = seq // tm
    row = lambda b, i: (b * nt + i, 0)
    bf16 = jnp.bfloat16
    col = lambda b, i: (0, b * nt + i)
    outs = [(SWA_Q_DIM, bf16), (SWA_KV_DIM, bf16), (SWA_KV_DIM, bf16), (POOL_DIM, bf16),
            (MLA_PAD_DIM, bf16), (MLA_PAD_DIM, bf16)]
    vt_rows = MLA_HEADS * MLA_V_ROWS
    return pl.pallas_call(
        _in_proj_kernel,
        grid=(batch, nt),
        in_specs=[pl.BlockSpec((tm, D_MODEL), row), pl.BlockSpec((1, tm), col)]
                 + const_specs,
        out_specs=[pl.BlockSpec((tm, w), row) for w, _ in outs]
                  + [pl.BlockSpec((vt_rows, tm), col)],
        out_shape=[jax.ShapeDtypeStruct((n, w), dt) for w, dt in outs]
                  + [jax.ShapeDtypeStruct((vt_rows, n), bf16)],
        scratch_shapes=[pltpu.VMEM((POOL_HALO, POOL_DIM), jnp.float32)],
        compiler_params=pltpu.CompilerParams(
            dimension_semantics=("arbitrary", "arbitrary"), vmem_limit_bytes=VMEM_LIMIT),
        name="in_proj",
    )(x2, pos2, *consts)


def _swa_kernel(sink_ref, q_ref, kc_ref, kp_ref, vc_ref, vp_ref, bias_ref, o_ref, *, bounded):
    i = pl.program_id(1)
    ts = q_ref.shape[0]
    nwin = ts // WINDOW
    bf16 = jnp.bfloat16
    q = q_ref[...]
    kext = jnp.concatenate([kp_ref[...], kc_ref[...]], axis=0)
    vext = jnp.concatenate([vp_ref[...], vc_ref[...]], axis=0)
    col = lax.broadcasted_iota(jnp.int32, (1, 2 * WINDOW), 1)
    no_prev = jnp.logical_and(i == 0, col < WINDOW)
    ones = jnp.ones((2 * WINDOW, LANE), bf16)
    heads = range(SWA_HEADS)
    sinks = [sink_ref[h] * LOG2E - sink_ref[SWA_HEADS] for h in heads]

    def scores(c, h):
        g = h // SWA_GROUP
        qh = q[c * WINDOW:(c + 1) * WINDOW, h * HEAD_DIM:(h + 1) * HEAD_DIM]
        kb = kext[c * WINDOW:(c + 2) * WINDOW, g * HEAD_DIM:(g + 1) * HEAD_DIM]
        s = _dot_nt(qh, kb) + bias_ref[h]
        if c == 0:
            s = jnp.where(no_prev, NEG, s)
        return s

    pending = [scores(0, h) for h in heads]
    for c in range(nwin):
        s_all = pending
        if bounded:
            ms = [0.0 for h in heads]
        else:
            ms = [jnp.maximum(jnp.max(s_all[h], axis=-1, keepdims=True), sinks[h]) for h in heads]
        pending = [scores(c + 1, h) for h in heads] if c + 1 < nwin else []
        ps = [jnp.exp2(s_all[h] - ms[h]).astype(bf16) for h in heads]
        vaug = jnp.concatenate([vext[c * WINDOW:(c + 2) * WINDOW], ones], axis=1)
        os_ = [_dot(ps[h], vaug) for h in heads]
        outs = []
        for h in heads:
            lo = (h // SWA_GROUP) * HEAD_DIM
            denom = os_[h][:, SWA_KV_DIM + lo:SWA_KV_DIM + lo + HEAD_DIM] + jnp.exp2(sinks[h] - ms[h])
            outs.append(os_[h][:, lo:lo + HEAD_DIM] / denom)
        o_ref[c * WINDOW:(c + 1) * WINDOW, :] = jnp.concatenate(outs, axis=1).astype(bf16)


def _swa(sinks, qs, ks, vs, bias, batch, seq, bounded):
    n = qs.shape[0]
    ts = min(SWA_TILE, seq)
    nt = seq // ts
    wpt = ts // WINDOW
    wps = seq // WINDOW
    row = lambda b, i: (b * nt + i, 0)
    prev = lambda b, i: (b * wps + jnp.maximum(i * wpt - 1, 0), 0)
    return pl.pallas_call(
        functools.partial(_swa_kernel, bounded=bounded),
        grid=(batch, nt),
        in_specs=[pl.BlockSpec(memory_space=pltpu.SMEM),
                  pl.BlockSpec((ts, SWA_Q_DIM), row),
                  pl.BlockSpec((ts, SWA_KV_DIM), row), pl.BlockSpec((WINDOW, SWA_KV_DIM), prev),
                  pl.BlockSpec((ts, SWA_KV_DIM), row), pl.BlockSpec((WINDOW, SWA_KV_DIM), prev),
                  _const_spec(bias.shape)],
        out_specs=pl.BlockSpec((ts, SWA_Q_DIM), row),
        out_shape=jax.ShapeDtypeStruct((n, SWA_Q_DIM), jnp.bfloat16),
        compiler_params=pltpu.CompilerParams(
            dimension_semantics=("arbitrary", "arbitrary"), vmem_limit_bytes=VMEM_LIMIT),
        name="swa_bounded" if bounded else "swa",
    )(sinks, qs, ks, ks, vs, vs, bias)


def _swa_score_shift(swa_q_gain, swa_k_gain, rel_bias, swa_sinks):
    gmax = lambda g: jnp.max(jnp.abs(g.astype(jnp.float32)), axis=-1)
    score = (HEAD_DIM ** 0.5 * gmax(swa_q_gain) * gmax(swa_k_gain) * MLA_BOUND_MARGIN
             + jnp.max(jnp.abs(rel_bias.astype(jnp.float32)))) * LOG2E
    shift = jnp.maximum(score, gmax(swa_sinks) * LOG2E)
    bounded = shift <= MLA_MAX_BOUND
    return jnp.where(bounded, shift, 0.0), bounded


def _mla_kernel(q_ref, k_ref, vt_ref, o_ref, m_ref, acc_ref, sa_ref, sb_ref, ma_ref, mb_ref):
    i = pl.program_id(2)
    tq = q_ref.shape[0]
    blk = max(MLA_BLK, tq)
    ratio = blk // tq
    bf16 = jnp.bfloat16
    chains = [(a, c) for a in range(2) for c in range(tq // MLA_SLAB)]
    m_ref[...] = jnp.full(m_ref.shape, NEG, jnp.float32)
    acc_ref[...] = jnp.zeros(acc_ref.shape, jnp.float32)

    def score(a, c, start, nk):
        lanes = slice(a * LANE, (a + 1) * LANE)
        return _dot_nt(k_ref[pl.ds(start, nk), lanes],
                       q_ref[c * MLA_SLAB:(c + 1) * MLA_SLAB, lanes])

    def full_score(ch, j, s_ref, mx_ref):
        a, c = chains[ch]
        s = score(a, c, pl.multiple_of(j * blk, blk), blk)
        s_ref[ch] = s
        mx_ref[ch] = jnp.max(s, axis=0, keepdims=True)

    def tail_score(ch, start, nfree):
        a, c = chains[ch]
        nk = nfree + (c + 1) * MLA_SLAB
        s = score(a, c, start, nk)
        row = lax.broadcasted_iota(jnp.int32, (nk, MLA_SLAB), 0)
        col = lax.broadcasted_iota(jnp.int32, (nk, MLA_SLAB), 1) + (c * MLA_SLAB + nfree)
        s = jnp.where(row <= col, s, NEG)
        return s, jnp.max(s, axis=0, keepdims=True)

    def soft_pv(ch, s, mx, start):
        a, c = chains[ch]
        cols = slice(c * MLA_SLAB, (c + 1) * MLA_SLAB)
        m_prev = m_ref[a, :, cols]
        m_new = jnp.maximum(m_prev, mx)
        alpha = jnp.exp2(m_prev - m_new)
        p = jnp.exp2(s - m_new).astype(bf16)
        vt = vt_ref[a * MLA_V_ROWS:(a + 1) * MLA_V_ROWS, pl.ds(start, s.shape[0])]
        acc_ref[a, :, cols] = alpha * acc_ref[a, :, cols] + _dot(vt, p)
        m_ref[a, :, cols] = m_new

    bufs = ((sa_ref, ma_ref), (sb_ref, mb_ref))
    nchain = len(chains)

    def pipe_step(j, cur, nxt):
        s_cur, mx_cur = bufs[cur]
        full_score(0, j + 1, *bufs[nxt])
        for ch in range(nchain):
            if ch + 1 < nchain:
                full_score(ch + 1, j + 1, *bufs[nxt])
            soft_pv(ch, s_cur[ch], mx_cur[ch], pl.multiple_of(j * blk, blk))

    nfull = i // ratio

    @pl.when(nfull > 0)
    def _():
        for ch in range(nchain):
            full_score(ch, 0, *bufs[0])

    def pair_body(t, carry):
        pipe_step(2 * t, 0, 1)
        pipe_step(2 * t + 1, 1, 0)
        return carry

    lax.fori_loop(0, jnp.maximum(nfull - 1, 0) // 2, pair_body, 0)

    def finish(has_full, last, nfree):
        if has_full and last == 1:
            pipe_step(nfull - 2, 0, 1)
        tail_start = pl.multiple_of(nfull * blk, blk)
        tail = [tail_score(0, tail_start, nfree)]
        for ch in range(nchain):
            if ch + 1 < nchain:
                tail.append(tail_score(ch + 1, tail_start, nfree))
            if has_full:
                s_last, mx_last = bufs[last]
                soft_pv(ch, s_last[ch], mx_last[ch], pl.multiple_of((nfull - 1) * blk, blk))
        for ch in range(nchain):
            soft_pv(ch, *tail[ch], tail_start)

    for r in range(ratio):
        nfree = r * tq
        is_odd = (i % ratio) == r
        pl.when(jnp.logical_and(nfull == 0, is_odd))(functools.partial(finish, False, 0, nfree))
        for last in (0, 1):
            cond = jnp.logical_and(jnp.logical_and(nfull > 0, (nfull - 1) % 2 == last), is_odd)
            pl.when(cond)(functools.partial(finish, True, last, nfree))

    outs = []
    for a in range(2):
        acc = acc_ref[a]
        outs.append(acc[0:V_HEAD] / acc[V_HEAD:V_HEAD + 1])
    o_ref[...] = jnp.concatenate(outs, axis=0).T.astype(bf16)


def _mla(qm, km, vm, batch, seq):
    n = qm.shape[0]
    tq = min(MLA_TQ, seq)
    nq = seq // tq
    pairs = MLA_HEADS // 2
    nchain = 2 * (tq // MLA_SLAB)
    return pl.pallas_call(
        _mla_kernel,
        grid=(batch, pairs, nq),
        in_specs=[pl.BlockSpec((tq, 2 * LANE), lambda b, p, i: (b * nq + i, p)),
                  pl.BlockSpec((seq, 2 * LANE), lambda b, p, i: (b, p)),
                  pl.BlockSpec((2 * MLA_V_ROWS, seq), lambda b, p, i: (p, b))],
        out_specs=pl.BlockSpec((tq, LANE), lambda b, p, i: (b * nq + i, p)),
        out_shape=jax.ShapeDtypeStruct((n, MLA_OUT_DIM), jnp.bfloat16),
        scratch_shapes=[pltpu.VMEM((2, 1, tq), jnp.float32),
                        pltpu.VMEM((2, MLA_V_ROWS, tq), jnp.float32)]
                       + [pltpu.VMEM((nchain, max(MLA_BLK, tq), MLA_SLAB), jnp.float32)] * 2
                       + [pltpu.VMEM((nchain, 1, MLA_SLAB), jnp.float32)] * 2,
        compiler_params=pltpu.CompilerParams(
            dimension_semantics=("arbitrary", "arbitrary", "arbitrary"),
            vmem_limit_bytes=VMEM_LIMIT),
        name="mla",
    )(qm, km, vm)


def _mla_bounded_kernel(q_ref, k_ref, vt_ref, o_ref, acc_ref):
    i = pl.program_id(2)
    tq = q_ref.shape[0]
    blk = tq
    bf16 = jnp.bfloat16
    chains = [(a, c) for a in range(2) for c in range(tq // MLA_SLAB)]
    nchain = len(chains)
    acc_ref[...] = jnp.zeros(acc_ref.shape, jnp.float32)

    def score(ch, start, nk):
        a, c = chains[ch]
        lanes = slice(a * LANE, (a + 1) * LANE)
        return _dot_nt(k_ref[pl.ds(start, nk), lanes],
                       q_ref[c * MLA_SLAB:(c + 1) * MLA_SLAB, lanes])

    def weigh(ch, s, start):
        a, c = chains[ch]
        cols = slice(c * MLA_SLAB, (c + 1) * MLA_SLAB)
        p = jnp.exp2(s).astype(bf16)
        vt = vt_ref[a * MLA_V_ROWS:(a + 1) * MLA_V_ROWS, pl.ds(start, s.shape[0])]
        acc_ref[a, :, cols] += _dot(vt, p)

    def blocks(starts, masked_last):
        work = [(ch, st, masked_last and b == len(starts) - 1)
                for b, st in enumerate(starts) for ch in range(nchain)]

        def chain_scores(ch, start, masked):
            if not masked:
                return score(ch, start, blk)
            c = chains[ch][1]
            nk = (c + 1) * MLA_SLAB
            s = score(ch, start, nk)
            row = lax.broadcasted_iota(jnp.int32, (nk, MLA_SLAB), 0)
            col = lax.broadcasted_iota(jnp.int32, (nk, MLA_SLAB), 1) + c * MLA_SLAB
            return jnp.where(row <= col, s, NEG)

        ahead = min(MLA_AHEAD, len(work))
        pend = [chain_scores(*work[w]) for w in range(ahead)]
        for w in range(len(work)):
            if w + ahead < len(work):
                pend.append(chain_scores(*work[w + ahead]))
            weigh(work[w][0], pend.pop(0), work[w][1])

    nb = MLA_UNROLL

    def body(t, carry):
        st = pl.multiple_of(nb * t * blk, blk)
        blocks([st + k * blk for k in range(nb)], False)
        return carry

    lax.fori_loop(0, i // nb, body, 0)
    for r in range(nb):
        def tail(r=r):
            first = pl.multiple_of((i - r) * blk, blk)
            blocks([first + k * blk for k in range(r + 1)], True)
        pl.when(i % nb == r)(tail)

    outs = []
    for a in range(2):
        acc = acc_ref[a]
        outs.append(acc[0:V_HEAD] / acc[V_HEAD:V_HEAD + 1])
    o_ref[...] = jnp.concatenate(outs, axis=0).T.astype(bf16)


def _mla_bounded(qm, km, vm, batch, seq):
    n = qm.shape[0]
    tq = min(MLA_TQ, seq)
    nq = seq // tq
    pairs = MLA_HEADS // 2
    return pl.pallas_call(
        _mla_bounded_kernel,
        grid=(batch, pairs, nq),
        in_specs=[pl.BlockSpec((tq, 2 * LANE), lambda b, p, i: (b * nq + i, p)),
                  pl.BlockSpec((seq, 2 * LANE), lambda b, p, i: (b, p)),
                  pl.BlockSpec((2 * MLA_V_ROWS, seq), lambda b, p, i: (p, b))],
        out_specs=pl.BlockSpec((tq, LANE), lambda b, p, i: (b * nq + i, p)),
        out_shape=jax.ShapeDtypeStruct((n, MLA_OUT_DIM), jnp.bfloat16),
        scratch_shapes=[pltpu.VMEM((2, MLA_V_ROWS, tq), jnp.float32)],
        compiler_params=pltpu.CompilerParams(
            dimension_semantics=("arbitrary", "arbitrary", "arbitrary"),
            vmem_limit_bytes=VMEM_LIMIT),
        name="mla_bounded",
    )(qm, km, vm)


def _mla_score_bound(q_nope_gain, q_rope_gain, k_nope_gain, k_rope_gain):
    gmax = lambda g: jnp.max(jnp.abs(g.astype(jnp.float32)), axis=-1)
    dots = (QK_NOPE * gmax(q_nope_gain) * gmax(k_nope_gain)
            + QK_ROPE * gmax(q_rope_gain) * gmax(k_rope_gain))
    return dots * (QK_HEAD ** -0.5 * LOG2E * MLA_BOUND_MARGIN)


def _out_ffn_kernel(x_ref, a_ref, b_ref, c_ref, wo_ref, g_ref, wg_ref, wu_ref, wd_ref, o_ref):
    bf16 = jnp.bfloat16
    mixed = jnp.concatenate([a_ref[...], b_ref[...], c_ref[...]], axis=1)
    x1 = x_ref[...] + _dot(mixed, wo_ref[...])
    h = (_rms(x1, D_MODEL) * g_ref[...]).astype(bf16)
    acc = x1
    for start, size in FFN_CHUNKS:
        gate = _dot(h, wg_ref[:, start:start + size])
        up = _dot(h, wu_ref[:, start:start + size])
        act = (gate * jax.nn.sigmoid(gate) * up).astype(bf16)
        acc = acc + _dot(act, wd_ref[start:start + size, :])
    o_ref[...] = acc


def _out_ffn(x2, oa, ob, oc, l, wo, g, wg, wu, wd):
    n = x2.shape[0]
    tm = min(FFN_TILE, n)
    row = lambda i: (i, 0)
    return pl.pallas_call(
        _out_ffn_kernel,
        grid=(n // tm,),
        in_specs=[pl.BlockSpec((tm, D_MODEL), row), pl.BlockSpec((tm, SWA_Q_DIM), row),
                  pl.BlockSpec((tm, POOL_DIM), row), pl.BlockSpec((tm, MLA_OUT_DIM), row)]
                 + [_layer_spec(w.shape, l, single_buffer=True) for w in (wo, g, wg, wu, wd)],
        out_specs=pl.BlockSpec((tm, D_MODEL), row),
        out_shape=jax.ShapeDtypeStruct((n, D_MODEL), jnp.float32),
        compiler_params=pltpu.CompilerParams(
            dimension_semantics=("arbitrary",), vmem_limit_bytes=VMEM_LIMIT),
        name="out_ffn",
    )(x2, oa, ob, oc, wo, g, wg, wu, wd)


def _block_diag_ones(groups, width):
    g = np.zeros((width, width), np.float32)
    for lo, hi in groups:
        g[lo:hi, lo:hi] = 1.0
    return jnp.asarray(g, jnp.bfloat16)


def _lane_rows(rows):
    rows = [r.astype(jnp.float32) for r in rows]
    rows += [jnp.zeros_like(rows[0])] * (8 - len(rows))
    return jnp.stack(rows, axis=1)


def _head_lanes(nope, rope):
    pad = jnp.zeros(nope.shape[:-1] + (MLA_HEAD_PAD - QK_HEAD,), jnp.float32)
    one = jnp.concatenate([nope.astype(jnp.float32), rope.astype(jnp.float32), pad], axis=-1)
    return jnp.tile(one, (1,) * (one.ndim - 1) + (MLA_HEADS,))


def _pack_params(attn_norm, w_in, swa_q_gain, swa_k_gain, pool_w, pool_scale, mla_q_a_gain,
                 mla_w_qb, mla_kv_a_gain, mla_w_kvb, mla_q_nope_gain, mla_q_rope_gain,
                 mla_k_nope_gain, mla_k_rope_gain, score_shift):
    f32, bf16 = jnp.float32, jnp.bfloat16
    depth = w_in.shape[0]
    zeros = lambda c: jnp.zeros((depth, D_MODEL, c), f32)
    w1 = jnp.concatenate([w_in[:, :, :C_KR], zeros(QK_NOPE), w_in[:, :, C_KR:],
                          zeros(LANE - QK_HEAD)], axis=2).astype(bf16)
    wqb = mla_w_qb.reshape(depth, Q_LORA, MLA_HEADS, QK_HEAD)
    wqb = jnp.pad(wqb, ((0, 0), (0, 0), (0, 0), (0, MLA_HEAD_PAD - QK_HEAD)))
    wqb = wqb.reshape(depth, Q_LORA, MLA_PAD_DIM).astype(bf16)
    wkvb = mla_w_kvb.reshape(depth, KV_LORA, MLA_HEADS, QK_NOPE + V_HEAD)
    wk = jnp.pad(wkvb[..., :QK_NOPE], ((0, 0), (0, 0), (0, 0), (0, MLA_HEAD_PAD - QK_NOPE)))
    wkvb_p = jnp.concatenate([wk.reshape(depth, KV_LORA, MLA_PAD_DIM),
                              wkvb[..., QK_NOPE:].reshape(depth, KV_LORA, MLA_OUT_DIM)],
                             axis=2).astype(bf16)
    ngroup = len(POOL_WINDOWS)
    eye = jnp.eye(ngroup, dtype=f32)[None, :, None, :, None]
    wp = (eye * pool_w[:, :, :, None, :]).reshape(depth, POOL_DIM, POOL_DIM).astype(bf16)
    ones_n, ones_r = jnp.ones((depth, QK_NOPE), f32), jnp.ones((depth, QK_ROPE), f32)
    half = jnp.concatenate([jnp.zeros((depth, ROPE_HALF), f32), jnp.ones((depth, ROPE_HALF), f32)],
                           axis=1)
    pad_lane = lambda v: jnp.tile(
        jnp.pad(v, ((0, 0), (QK_HEAD, MLA_HEAD_PAD - QK_HEAD - 1))), (1, MLA_HEADS))
    v768 = _lane_rows([
        _head_lanes(mla_q_nope_gain, mla_q_rope_gain) * (QK_HEAD ** -0.5 * LOG2E),
        _head_lanes(ones_n / QK_NOPE, ones_r / QK_ROPE),
        _head_lanes(mla_k_nope_gain, 0.0 * ones_r),
        _head_lanes(0.0 * ones_n, half),
        _head_lanes(0.0 * ones_n, half - 1.0),
        pad_lane(jnp.ones((depth, 1), f32)),
        pad_lane(-score_shift[:, None]),
    ])
    v384 = (jnp.tile(swa_q_gain, (1, SWA_HEADS)) * (HEAD_DIM ** -0.5 * LOG2E))[:, None, :]
    windows = jnp.broadcast_to(jnp.repeat(jnp.asarray(POOL_WINDOWS, f32), POOL_GROUP_DIM),
                               (depth, POOL_DIM))
    v256 = _lane_rows([mla_q_a_gain, pool_scale, windows])
    v128 = _lane_rows([jnp.tile(swa_k_gain, (1, SWA_KV_HEADS)), mla_kv_a_gain,
                       _head_lanes(0.0 * ones_n, mla_k_rope_gain)[:, :LANE]])
    stacked = (attn_norm[:, None, :], w1, v768, v384, v256, v128, wp, wqb, wkvb_p)

    inv_freq = ROPE_THETA ** (-jnp.arange(0, QK_ROPE, 2, dtype=f32) / QK_ROPE)
    g384 = _block_diag_ones([(h * HEAD_DIM, (h + 1) * HEAD_DIM) for h in range(SWA_HEADS)],
                            SWA_Q_DIM)
    gsum = np.zeros((MLA_PAD_DIM, LANE), np.float32)
    for h in range(MLA_HEADS):
        base = h * MLA_HEAD_PAD
        gsum[base:base + QK_NOPE, 2 * h] = 1.0
        gsum[base + QK_NOPE:base + QK_HEAD, 2 * h + 1] = 1.0
    shared = (g384, jnp.asarray(gsum, bf16), jnp.asarray(gsum.T, bf16),
              inv_freq.reshape(ROPE_HALF, 1))
    return stacked, shared


def kernel(x, positions, rel_bias, attn_norm, w_in, swa_q_gain, swa_k_gain, swa_sinks, pool_w,
           pool_scale, mla_q_a_gain, mla_w_qb, mla_kv_a_gain, mla_w_kvb, mla_q_nope_gain,
           mla_q_rope_gain, mla_k_nope_gain, mla_k_rope_gain, w_out, ffn_norm, w_gate, w_up,
           w_down):
    batch, seq, _ = x.shape
    n = batch * seq
    depth = w_in.shape[0]
    bf16 = jnp.bfloat16
    x2 = x.reshape(n, D_MODEL)
    pos2 = positions.astype(jnp.float32).reshape(1, n)
    swa_shift, swa_bounded = _swa_score_shift(swa_q_gain, swa_k_gain, rel_bias, swa_sinks)
    score_bound = _mla_score_bound(mla_q_nope_gain, mla_q_rope_gain, mla_k_nope_gain,
                                   mla_k_rope_gain)
    bounded = score_bound <= MLA_MAX_BOUND
    stacked, shared = _pack_params(attn_norm, w_in, swa_q_gain, swa_k_gain, pool_w, pool_scale,
                                   mla_q_a_gain, mla_w_qb, mla_kv_a_gain, mla_w_kvb,
                                   mla_q_nope_gain, mla_q_rope_gain, mla_k_nope_gain,
                                   mla_k_rope_gain, jnp.where(bounded, score_bound, 0.0))
    ffn = (w_out.astype(bf16), ffn_norm[:, None, :], w_gate.astype(bf16), w_up.astype(bf16),
           w_down.astype(bf16))
    for l in range(depth):
        qs, ks, vs, pool_o, qm, km, vm = _in_proj(x2, pos2, l, stacked, shared, batch, seq)
        bias = _bias_table(rel_bias, swa_shift[l:l + 1])
        sink_shift = jnp.concatenate([swa_sinks[l].astype(jnp.float32), swa_shift[l:l + 1]])
        swa_o = lax.cond(swa_bounded[l],
                         functools.partial(_swa, batch=batch, seq=seq, bounded=True),
                         functools.partial(_swa, batch=batch, seq=seq, bounded=False),
                         sink_shift, qs, ks, vs, bias)
        mla_o = lax.cond(bounded[l],
                         functools.partial(_mla_bounded, batch=batch, seq=seq),
                         functools.partial(_mla, batch=batch, seq=seq), qm, km, vm)
        x2 = _out_ffn(x2, swa_o, pool_o, mla_o, l, *ffn)
    return x2.reshape(batch, seq, D_MODEL)
```
